```python
import jax, jax.numpy as jnp
from jax import lax
import numpy as np

D_MODEL = 1024
BATCH = 8
SEQ = 4096
DEPTH = 1
DEC_BATCH = 32
DEC_SEQ = 16
PAST_LEN = 4096

CHUNK = 64
N_HEADS = 8
HEAD_DIM = 64
ATTN_WIDTH = N_HEADS * HEAD_DIM
CONV_WIDTH = 512
CONV_K = 3
Q_BLOCK = 128
N_GROUPS = 4
EXPERTS_PER_GROUP = 4
N_EXPERTS = N_GROUPS * EXPERTS_PER_GROUP
TOP_K = 2
D_EXPERT = 512
PLE_DIM = 256
EPS = 1e-6
IN_SIZES = (ATTN_WIDTH, ATTN_WIDTH, ATTN_WIDTH, CONV_WIDTH, CONV_WIDTH, CONV_WIDTH, D_MODEL, D_MODEL)
IN_TOTAL = sum(IN_SIZES)

kernel_name = "stickbreak_shortconv_hmoe_stream_step"


def rmsnorm(x, g):
    xf = x.astype(jnp.float32)
    y = xf * lax.rsqrt(jnp.mean(xf * xf, axis=-1, keepdims=True) + EPS) * g.astype(jnp.float32)
    return y.astype(x.dtype)


def mix_projections(x, ln_mix, w_in, q_norm, k_norm):
    h = rmsnorm(x, ln_mix)
    z = h @ w_in
    cuts = np.cumsum(IN_SIZES)[:-1].tolist()
    q, k, v, cb, cc, cx, ga, gb = jnp.split(z, cuts, axis=-1)
    hs = x.shape[:-1] + (N_HEADS, HEAD_DIM)
    q = rmsnorm(q.reshape(hs), q_norm)
    k = rmsnorm(k.reshape(hs), k_norm)
    v = v.reshape(hs)
    u = cc * cx
    return q, k, v, cb, u, ga, gb


def stick_breaking(q, k, v, q_pos, k_pos):
    z = jnp.einsum('bqhd,bkhd->bhqk', q.astype(jnp.float32), k.astype(jnp.float32)) * (HEAD_DIM ** -0.5)
    mask = k_pos[None, :] < q_pos[:, None]
    log_keep = jnp.where(mask, jax.nn.log_sigmoid(-z), 0.0)
    after = lax.cumsum(log_keep, axis=3, reverse=True) - log_keep
    a = jnp.where(mask, jnp.exp(jax.nn.log_sigmoid(z) + after), 0.0)
    return jnp.einsum('bhqk,bkhd->bqhd', a, v.astype(jnp.float32)).astype(v.dtype)


def stick_breaking_blocked(q, k, v):
    b, s = q.shape[0], q.shape[1]
    k_pos = jnp.arange(s)

    def one_block(i):
        qs = lax.dynamic_slice_in_dim(q, i * Q_BLOCK, Q_BLOCK, axis=1)
        return stick_breaking(qs, k, v, i * Q_BLOCK + jnp.arange(Q_BLOCK), k_pos)

    out = lax.map(one_block, jnp.arange(s // Q_BLOCK))
    return out.transpose(1, 0, 2, 3, 4).reshape(b, s, N_HEADS, HEAD_DIM)


def short_conv(u, u_past, conv_w):
    t = u.shape[1]
    ext = jnp.concatenate([u_past.astype(u.dtype), u], axis=1)
    y = sum(conv_w[j] * ext[:, j:j + t, :] for j in range(CONV_K))
    return y, ext[:, -(CONV_K - 1):, :]


def hier_moe(h, w_rg, b_rg, w_re, b_re, w_eg, w_eu, w_ed):
    lg = (h @ w_rg + b_rg).astype(jnp.float32)
    pg = jax.nn.softmax(lg, axis=-1)
    g_sel = jnp.argmax(lg, axis=-1)
    p_sel = jnp.max(pg, axis=-1)
    le = (h @ w_re + b_re).astype(jnp.float32).reshape(h.shape[:-1] + (N_GROUPS, EXPERTS_PER_GROUP))
    le_sel = jnp.take_along_axis(le, g_sel[..., None, None], axis=-2)[..., 0, :]
    vals, idx = lax.top_k(le_sel, TOP_K)
    w2 = jax.nn.softmax(vals, axis=-1)
    eidx = g_sel[..., None] * EXPERTS_PER_GROUP + idx
    gate = p_sel[..., None] * jnp.sum(jax.nn.one_hot(eidx, N_EXPERTS, dtype=jnp.float32) * w2[..., None], axis=-2)
    gate = gate.astype(h.dtype)
    out = jnp.zeros_like(h)
    for e in range(N_EXPERTS):
        y = (jax.nn.silu(h @ w_eg[e]) * (h @ w_eu[e])) @ w_ed[e]
        out = out + gate[..., e:e + 1] * y
    return out


def merge_and_channel(x, attn_out, cb, conv_y, ga, gb, p, lw):
    ya = attn_out.reshape(attn_out.shape[:-2] + (ATTN_WIDTH,)) @ lw['w_attn_branch']
    yb = (cb * conv_y) @ lw['w_conv_branch']
    m = jax.nn.sigmoid(ga) * ya + jax.nn.sigmoid(gb) * yb
    x = x + m @ lw['w_out']
    x = x + hier_moe(rmsnorm(x, lw['ln_ffn']), lw['w_router_group'], lw['b_router_group'],
                     lw['w_router_expert'], lw['b_router_expert'],
                     lw['w_exp_gate'], lw['w_exp_up'], lw['w_exp_down'])
    gate = jax.nn.sigmoid(rmsnorm(x, lw['ln_ple']) @ lw['w_ple_gate'])
    return x + gate * (p @ lw['w_ple_proj'])


def setup_inputs(seed: int = 0) -> dict:
    key = jax.random.key(seed)
    ks = jax.random.split(key, 32)

    def nrm(k, shape, scale=1.0):
        return jax.random.normal(k, shape, jnp.float32) * scale

    L = DEPTH
    return {
        'x_prompt': nrm(ks[0], (BATCH, SEQ, D_MODEL)),
        'x_sample': nrm(ks[1], (DEC_BATCH, DEC_SEQ, D_MODEL)),
        'cache_k': nrm(ks[2], (L, DEC_BATCH, PAST_LEN, N_HEADS, HEAD_DIM)),
        'cache_v': nrm(ks[3], (L, DEC_BATCH, PAST_LEN, N_HEADS, HEAD_DIM)),
        'state_conv': nrm(ks[4], (L, DEC_BATCH, CONV_K - 1, CONV_WIDTH)),
        'p_prompt': nrm(ks[5], (L, BATCH, SEQ, PLE_DIM)),
        'p_sample': nrm(ks[6], (L, DEC_BATCH, DEC_SEQ, PLE_DIM)),
        'ln_mix': 1.0 + nrm(ks[7], (L, D_MODEL), 0.05),
        'w_in': nrm(ks[8], (L, D_MODEL, IN_TOTAL), D_MODEL ** -0.5),
        'q_norm': 1.0 + nrm(ks[9], (L, HEAD_DIM), 0.05),
        'k_norm': 1.0 + nrm(ks[10], (L, HEAD_DIM), 0.05),
        'conv_w': nrm(ks[11], (L, CONV_K, CONV_WIDTH), CONV_K ** -0.5),
        'w_attn_branch': nrm(ks[12], (L, ATTN_WIDTH, D_MODEL), ATTN_WIDTH ** -0.5),
        'w_conv_branch': nrm(ks[13], (L, CONV_WIDTH, D_MODEL), CONV_WIDTH ** -0.5),
        'w_out': nrm(ks[14], (L, D_MODEL, D_MODEL), D_MODEL ** -0.5),
        'ln_ffn': 1.0 + nrm(ks[15], (L, D_MODEL), 0.05),
        'w_router_group': nrm(ks[16], (L, D_MODEL, N_GROUPS), D_MODEL ** -0.5),
        'b_router_group': nrm(ks[17], (L, N_GROUPS), 0.01),
        'w_router_expert': nrm(ks[18], (L, D_MODEL, N_EXPERTS), D_MODEL ** -0.5),
        'b_router_expert': nrm(ks[19], (L, N_EXPERTS), 0.01),
        'w_exp_gate': nrm(ks[20], (L, N_EXPERTS, D_MODEL, D_EXPERT), D_MODEL ** -0.5),
        'w_exp_up': nrm(ks[21], (L, N_EXPERTS, D_MODEL, D_EXPERT), D_MODEL ** -0.5),
        'w_exp_down': nrm(ks[22], (L, N_EXPERTS, D_EXPERT, D_MODEL), D_EXPERT ** -0.5),
        'ln_ple': 1.0 + nrm(ks[23], (L, D_MODEL), 0.05),
        'w_ple_gate': nrm(ks[24], (L, D_MODEL, D_MODEL), D_MODEL ** -0.5),
        'w_ple_proj': nrm(ks[25], (L, PLE_DIM, D_MODEL), PLE_DIM ** -0.5),
    }


def reference(x_prompt, x_sample, cache_k, cache_v, state_conv, p_prompt, p_sample,
              ln_mix, w_in, q_norm, k_norm, conv_w, w_attn_branch, w_conv_branch, w_out,
              ln_ffn, w_router_group, b_router_group, w_router_expert, b_router_expert,
              w_exp_gate, w_exp_up, w_exp_down, ln_ple, w_ple_gate, w_ple_proj):
    xp, xs = x_prompt, x_sample
    kp_l, vp_l, cp_l, ks_l, vs_l, cs_l = [], [], [], [], [], []
    for i in range(DEPTH):
        lw = dict(w_attn_branch=w_attn_branch[i], w_conv_branch=w_conv_branch[i], w_out=w_out[i],
                  ln_ffn=ln_ffn[i], w_router_group=w_router_group[i], b_router_group=b_router_group[i],
                  w_router_expert=w_router_expert[i], b_router_expert=b_router_expert[i],
                  w_exp_gate=w_exp_gate[i], w_exp_up=w_exp_up[i], w_exp_down=w_exp_down[i],
                  ln_ple=ln_ple[i], w_ple_gate=w_ple_gate[i], w_ple_proj=w_ple_proj[i])

        q, k, v, cb, u, ga, gb = mix_projections(xp, ln_mix[i], w_in[i], q_norm[i], k_norm[i])
        attn = stick_breaking_blocked(q, k, v)
        u0 = jnp.zeros((u.shape[0], CONV_K - 1, CONV_WIDTH), u.dtype)
        conv_y, conv_new = short_conv(u, u0, conv_w[i])
        xp = merge_and_channel(xp, attn, cb, conv_y, ga, gb, p_prompt[i], lw)
        kp_l.append(k); vp_l.append(v); cp_l.append(conv_new)

        q, k, v, cb, u, ga, gb = mix_projections(xs, ln_mix[i], w_in[i], q_norm[i], k_norm[i])
        past = cache_k.shape[2]
        t = xs.shape[1]
        k_all = jnp.concatenate([cache_k[i].astype(k.dtype), k], axis=1)
        v_all = jnp.concatenate([cache_v[i].astype(v.dtype), v], axis=1)
        attn = stick_breaking(q, k_all, v_all, past + jnp.arange(t), jnp.arange(past + t))
        conv_y, conv_new = short_conv(u, state_conv[i], conv_w[i])
        xs = merge_and_channel(xs, attn, cb, conv_y, ga, gb, p_sample[i], lw)
        ks_l.append(k); vs_l.append(v); cs_l.append(conv_new)

    new_k_prompt = jnp.stack(kp_l)
    new_v_prompt = jnp.stack(vp_l)
    new_conv_prompt = jnp.stack(cp_l)
    new_k_sample = jnp.stack(ks_l)
    new_v_sample = jnp.stack(vs_l)
    new_conv_sample = jnp.stack(cs_l)
    return (xp, xs, new_k_prompt, new_v_prompt, new_conv_prompt, new_k_sample, new_v_sample, new_conv_sample)
```

```python
import functools

import jax
import jax.numpy as jnp
from jax import lax
from jax.experimental import pallas as pl
from jax.experimental.pallas import tpu as pltpu

F32 = jnp.float32
BF16 = jnp.bfloat16

EPS = 1e-6
N_HEADS = 8
HEAD_DIM = 64
ATTN_WIDTH = N_HEADS * HEAD_DIM
N_GROUPS = 4
EXPERTS_PER_GROUP = 4
LANES = 128
ATTN_BLOCK = 256
VMEM_LIMIT = 56 * 1024 * 1024


def _dot(a, b):
    return jnp.dot(a, b, preferred_element_type=F32)


def _dot_nt(a, b):
    return lax.dot_general(a, b, (((1,), (1,)), ((), ())), preferred_element_type=F32)


def _rms_scale(x):
    return lax.rsqrt(jnp.mean(x * x, axis=-1, keepdims=True) + EPS)


def _const_spec(shape):
    nd = len(shape)
    return pl.BlockSpec(shape, lambda *_: (0,) * nd, pipeline_mode=pl.Buffered(1))


def _head_norm(z, gain, hsum_ref):
    ss = _dot((z * z).astype(BF16), hsum_ref[...])
    return z * lax.rsqrt(ss * (1.0 / HEAD_DIM) + EPS) * gain


def _input_stage_body(x, past1, past2, lnmix_ref, win_ref, qn_ref, hsum_ref, convw_ref, wcb_ref,
                      q_ref, sa_ref, mb_ref):
    aw = ATTN_WIDTH
    cw = convw_ref.shape[1]
    d = x.shape[1]
    h = (x * _rms_scale(x) * lnmix_ref[...]).astype(BF16)

    def proj(lo, width):
        return _dot(h, win_ref[:, lo:lo + width])

    q = _head_norm(proj(0, aw), qn_ref[...], hsum_ref)
    q_ref[...] = (q * (HEAD_DIM ** -0.5)).astype(BF16)

    cb = proj(3 * aw, cw)
    u = proj(3 * aw + cw, cw) * proj(3 * aw + 2 * cw, cw)
    u1, u2 = past1(u), past2(u)
    conv_y = convw_ref[0:1, :] * u2 + convw_ref[1:2, :] * u1 + convw_ref[2:3, :] * u
    yb = _dot((cb * conv_y).astype(BF16), wcb_ref[...])
    ga = proj(3 * aw + 3 * cw, d)
    gb = proj(3 * aw + 3 * cw + d, d)
    sa_ref[...] = jax.nn.sigmoid(ga).astype(BF16)
    mb_ref[...] = (jax.nn.sigmoid(gb) * yb).astype(BF16)
    return h, proj, u


def _input_stage_prompt_kernel(x_ref, lnmix_ref, win_ref, wkvt_ref, qn_ref, knt_ref, hsum_ref,
                               convw_ref, wcb_ref, q_ref, kt_ref, vt_ref, ktb_ref, vtb_ref,
                               sa_ref, mb_ref, cnew_ref, tail_ref, *, tiles_per_seq):
    t = x_ref.shape[0]
    aw = ATTN_WIDTH

    @pl.when(pl.program_id(0) % tiles_per_seq == 0)
    def _():
        tail_ref[...] = jnp.zeros_like(tail_ref)

    def shifted(u, n):
        ext = jnp.concatenate([tail_ref[...], u], axis=0)
        return pltpu.roll(ext, n, axis=0)[8:, :]

    h, _, u = _input_stage_body(x_ref[...], lambda u: shifted(u, 1), lambda u: shifted(u, 2),
                                lnmix_ref, win_ref, qn_ref, hsum_ref, convw_ref, wcb_ref,
                                q_ref, sa_ref, mb_ref)
    tail_ref[...] = u[t - 8:, :]
    cnew_ref[0] = u[t - 8:, :]

    kt = _dot_nt(wkvt_ref[0:aw, :], h)
    k3 = kt.reshape(N_HEADS, HEAD_DIM, t)
    scale = lax.rsqrt(jnp.mean(k3 * k3, axis=1, keepdims=True) + EPS)
    kt = (k3 * scale).reshape(aw, t) * jnp.tile(knt_ref[...], (1, t // LANES))
    kt_ref[0] = kt
    ktb_ref[0] = kt.astype(BF16)
    vt = _dot_nt(wkvt_ref[aw:2 * aw, :], h)
    vt_ref[0] = vt
    vtb_ref[0] = vt.astype(BF16)


def _input_stage_sample_kernel(x_ref, pe0_ref, pe1_ref, lnmix_ref, win_ref, qn_ref, kn_ref,
                               hsum_ref, convw_ref, wcb_ref, q_ref, k_ref, v_ref, kb_ref,
                               vb_ref, sa_ref, mb_ref, u_ref, *, seq):
    t = x_ref.shape[0]
    aw = ATTN_WIDTH
    pos = lax.broadcasted_iota(jnp.int32, (t, 1), 0) % seq

    def past1(u):
        return jnp.where(pos == 0, pe1_ref[...], pltpu.roll(u, 1, axis=0))

    def past2(u):
        return jnp.where(pos == 0, pe0_ref[...],
                         jnp.where(pos == 1, pe1_ref[...], pltpu.roll(u, 2, axis=0)))

    _, proj, u = _input_stage_body(x_ref[...], past1, past2, lnmix_ref, win_ref, qn_ref,
                                   hsum_ref, convw_ref, wcb_ref, q_ref, sa_ref, mb_ref)
    u_ref[...] = u
    k = _head_norm(proj(aw, aw), kn_ref[...], hsum_ref)
    k_ref[...] = k
    kb_ref[...] = k.astype(BF16)
    v = proj(2 * aw, aw)
    v_ref[...] = v
    vb_ref[...] = v.astype(BF16)


def _input_stage_prompt(x, wts, *, seq, tile):
    n, d = x.shape
    nb = n // seq
    aw = ATTN_WIDTH
    cw = wts['conv_w'].shape[1]
    tiles_per_seq = seq // tile
    row = lambda w: pl.BlockSpec((tile, w), lambda i: (i, 0))
    feat = pl.BlockSpec((1, aw, tile), lambda i: (i // tiles_per_seq, 0, i % tiles_per_seq))
    consts = [wts['ln_mix'], wts['w_in'], wts['w_kv_t'], wts['q_norm'], wts['k_norm_t'],
              wts['head_sum'], wts['conv_w'], wts['w_conv_branch']]
    sds = jax.ShapeDtypeStruct
    outs = pl.pallas_call(
        functools.partial(_input_stage_prompt_kernel, tiles_per_seq=tiles_per_seq),
        grid=(n // tile,),
        in_specs=[row(d)] + [_const_spec(c.shape) for c in consts],
        out_specs=[row(aw), feat, feat, feat, feat, row(d), row(d),
                   pl.BlockSpec((1, 8, cw), lambda i: (i // tiles_per_seq, 0, 0))],
        out_shape=[sds((n, aw), BF16), sds((nb, aw, seq), F32), sds((nb, aw, seq), F32),
                   sds((nb, aw, seq), BF16), sds((nb, aw, seq), BF16), sds((n, d), BF16),
                   sds((n, d), BF16), sds((nb, 8, cw), F32)],
        scratch_shapes=[pltpu.VMEM((8, cw), F32)],
        compiler_params=pltpu.CompilerParams(dimension_semantics=("arbitrary",),
                                             vmem_limit_bytes=VMEM_LIMIT),
        name="input_stage_prompt",
    )(x, *consts)
    return outs[:7], outs[7][:, 6:, :]


def _input_stage_sample(x, state, wts, *, seq):
    n, d = x.shape
    aw = ATTN_WIDTH
    cw = wts['conv_w'].shape[1]
    pe0 = jnp.repeat(state[:, 0, :], seq, axis=0)
    pe1 = jnp.repeat(state[:, 1, :], seq, axis=0)
    consts = [wts['ln_mix'], wts['w_in'], wts['q_norm'], wts['k_norm'], wts['head_sum'],
              wts['conv_w'], wts['w_conv_branch']]
    full = lambda a: pl.BlockSpec(a.shape, lambda i: (0,) * a.ndim)
    sds = jax.ShapeDtypeStruct
    out_shape = [sds((n, aw), BF16), sds((n, aw), F32), sds((n, aw), F32), sds((n, aw), BF16),
                 sds((n, aw), BF16), sds((n, d), BF16), sds((n, d), BF16), sds((n, cw), F32)]
    outs = pl.pallas_call(
        functools.partial(_input_stage_sample_kernel, seq=seq),
        grid=(1,),
        in_specs=[full(x), full(pe0), full(pe1)] + [full(c) for c in consts],
        out_specs=[full(o) for o in out_shape],
        out_shape=out_shape,
        compiler_params=pltpu.CompilerParams(dimension_semantics=("arbitrary",),
                                             vmem_limit_bytes=VMEM_LIMIT),
        name="input_stage_sample",
    )(x, pe0, pe1, *consts)
    conv_new = outs[7].reshape(n // seq, seq, cw)[:, seq - 2:, :]
    return outs[:7], conv_new


def _stick_block(z, carry, tri_ref, keep):
    sp = jnp.maximum(z, 0.0) + jnp.log(1.0 + jnp.exp(-jnp.abs(z)))
    if keep is not None:
        sp = jnp.where(keep, sp, 0.0)
    hi = sp.astype(BF16)
    lo = (sp - hi.astype(F32)).astype(BF16)
    incl = _dot(hi, tri_ref[...]) + _dot(lo, tri_ref[...])
    a = jnp.exp(z - incl - carry)
    if keep is not None:
        a = jnp.where(keep, a, 0.0)
    return a.astype(BF16), carry + jnp.sum(sp, axis=-1, keepdims=True)


def _attn_prompt_kernel(q_ref, kt_ref, vt_ref, tri_ref, o_ref):
    tq = q_ref.shape[0]
    qi = pl.program_id(2)
    lane_head = lax.broadcasted_iota(jnp.int32, (1, LANES), 1) // HEAD_DIM
    row_head = lax.broadcasted_iota(jnp.int32, (LANES, tq), 0) // HEAD_DIM
    row = lax.broadcasted_iota(jnp.int32, (tq, tq), 0)
    col = lax.broadcasted_iota(jnp.int32, (tq, tq), 1)
    keep = col < row
    out = jnp.zeros((tq, LANES), F32)
    for head in range(LANES // HEAD_DIM):
        qm = q_ref[...] * (lane_head == head).astype(BF16)
        vsel = row_head == head

        def step(j, carry, acc, keep):
            start = pl.multiple_of(j * tq, tq)
            kt = kt_ref[0, :, pl.ds(start, tq)]
            vt = vt_ref[0, :, pl.ds(start, tq)]
            vt = jnp.where(vsel, vt, jnp.zeros_like(vt))
            a, carry = _stick_block(_dot(qm, kt), carry, tri_ref, keep)
            return carry, acc + _dot_nt(a, vt)

        carry, acc = step(qi, jnp.zeros((tq, 1), F32), jnp.zeros((tq, LANES), F32), keep)

        def body(jj, state):
            return step(qi - 1 - jj, state[0], state[1], None)

        carry, acc = lax.fori_loop(0, qi, body, (carry, acc))
        out = out + acc
    o_ref[...] = out.astype(o_ref.dtype)


def _attn_prompt(q, ktb, vtb, tri):
    n, aw = q.shape
    nb, _, seq = ktb.shape
    tq = ATTN_BLOCK
    nq = seq // tq
    qspec = pl.BlockSpec((tq, LANES), lambda b, hp, i: (b * nq + i, hp))
    kvspec = pl.BlockSpec((1, LANES, seq), lambda b, hp, i: (b, hp, 0))
    return pl.pallas_call(
        _attn_prompt_kernel,
        grid=(nb, aw // LANES, nq),
        in_specs=[qspec, kvspec, kvspec, pl.BlockSpec(tri.shape, lambda b, hp, i: (0, 0))],
        out_specs=qspec,
        out_shape=jax.ShapeDtypeStruct((n, aw), BF16),
        compiler_params=pltpu.CompilerParams(
            dimension_semantics=("arbitrary", "arbitrary", "arbitrary"),
            vmem_limit_bytes=VMEM_LIMIT),
        name="attn_prompt",
    )(q, ktb, vtb, tri)


def _attn_sample_kernel(q_ref, kn_ref, vn_ref, ck_ref, cv_ref, tri_ref, o_ref,
                        qrow_ref, carry_ref, acc_ref):
    tq, aw = q_ref.shape[1], q_ref.shape[2]
    rows = N_HEADS * tq
    tk = ATTN_BLOCK
    c = pl.program_id(1)

    @pl.when(c == 0)
    def _():
        row_head = lax.broadcasted_iota(jnp.int32, (rows, aw), 0) // tq
        lane_head = lax.broadcasted_iota(jnp.int32, (rows, aw), 1) // HEAD_DIM
        qt = jnp.concatenate([q_ref[0]] * N_HEADS, axis=0)
        qrow = jnp.where(row_head == lane_head, qt, jnp.zeros_like(qt))
        qrow_ref[...] = qrow
        pad = jnp.zeros((tk - tq, aw), BF16)
        kn = jnp.concatenate([kn_ref[0], pad], axis=0)
        vn = jnp.concatenate([vn_ref[0], pad], axis=0)
        qpos = lax.broadcasted_iota(jnp.int32, (rows, tk), 0) % tq
        kpos = lax.broadcasted_iota(jnp.int32, (rows, tk), 1)
        a, carry = _stick_block(_dot_nt(qrow, kn), jnp.zeros((rows, 1), F32), tri_ref,
                                kpos < qpos)
        carry_ref[...] = carry
        acc_ref[...] = _dot(a, vn)

    qrow = qrow_ref[...]
    nsub = ck_ref.shape[2] // tk
    for s in reversed(range(nsub)):
        kt = ck_ref[0, :, s * tk:(s + 1) * tk].astype(BF16)
        vt = cv_ref[0, :, s * tk:(s + 1) * tk].astype(BF16)
        a, carry = _stick_block(_dot(qrow, kt), carry_ref[...], tri_ref, None)
        carry_ref[...] = carry
        acc_ref[...] += _dot_nt(a, vt)

    @pl.when(c == pl.num_programs(1) - 1)
    def _():
        lane_head = lax.broadcasted_iota(jnp.int32, (tq, aw), 1) // HEAD_DIM
        out = jnp.zeros((tq, aw), F32)
        for head in range(N_HEADS):
            out = out + jnp.where(lane_head == head, acc_ref[head * tq:(head + 1) * tq, :], 0.0)
        o_ref[0] = out.astype(o_ref.dtype)


def _attn_sample(q, kn, vn, cache_kt, cache_vt, tri, *, chunk):
    nb, tq, aw = q.shape
    past = cache_kt.shape[2]
    nchunk = past // chunk
    new = pl.BlockSpec((1, tq, aw), lambda b, c: (b, 0, 0))
    cache = pl.BlockSpec((1, aw, chunk), lambda b, c: (b, 0, nchunk - 1 - c))
    rows = N_HEADS * tq
    return pl.pallas_call(
        _attn_sample_kernel,
        grid=(nb, nchunk),
        in_specs=[new, new, new, cache, cache, pl.BlockSpec(tri.shape, lambda b, c: (0, 0))],
        out_specs=new,
        out_shape=jax.ShapeDtypeStruct((nb, tq, aw), BF16),
        scratch_shapes=[pltpu.VMEM((rows, aw), BF16), pltpu.VMEM((rows, 1), F32),
                        pltpu.VMEM((rows, aw), F32)],
        compiler_params=pltpu.CompilerParams(dimension_semantics=("arbitrary", "arbitrary"),
                                             vmem_limit_bytes=VMEM_LIMIT),
        name="attn_sample",
    )(q, kn, vn, cache_kt, cache_vt, tri)


def _first_max(vals, lane):
    m = jnp.max(vals, axis=-1, keepdims=True)
    idx = jnp.min(jnp.where(vals == m, lane, float(LANES)), axis=-1, keepdims=True)
    return m, idx


def _router_gates(h, wrg_ref, brg_ref, wre_ref, bre_ref):
    lane = lax.broadcasted_iota(jnp.int32, (1, LANES), 1).astype(F32)
    neg = -jnp.inf
    lg = _dot(h, wrg_ref[...]) + brg_ref[...]
    lg = jnp.where(lane < N_GROUPS, lg, neg)
    mg, gidx = _first_max(lg, lane)
    p_sel = 1.0 / jnp.sum(jnp.exp(lg - mg), axis=-1, keepdims=True)
    le = _dot(h, wre_ref[...]) + bre_ref[...]
    in_group = jnp.floor(lane * (1.0 / EXPERTS_PER_GROUP)) == gidx
    le = jnp.where(in_group, le, neg)
    t1, i1 = _first_max(le, lane)
    t2, i2 = _first_max(jnp.where(lane == i1, neg, le), lane)
    e2 = jnp.exp(t2 - t1)
    w1 = 1.0 / (1.0 + e2)
    return p_sel * (jnp.where(lane == i1, w1, 0.0) + jnp.where(lane == i2, e2 * w1, 0.0))


def _channel_kernel(x_ref, attn_ref, sa_ref, mb_ref, p_ref, wab_ref, wout_ref, lnffn_ref,
                    wrg_ref, brg_ref, wre_ref, bre_ref, weg_ref, weu_ref, wed_ref,
                    lnple_ref, wpg_ref, wpp_ref, o_ref, h_ref, gate_ref):
    e = pl.program_id(1)

    @pl.when(e == 0)
    def _():
        ya = _dot(attn_ref[...], wab_ref[...])
        m = sa_ref[...].astype(F32) * ya + mb_ref[...].astype(F32)
        x1 = x_ref[...] + _dot(m.astype(BF16), wout_ref[...])
        o_ref[...] = x1
        h = (x1 * _rms_scale(x1) * lnffn_ref[...]).astype(BF16)
        h_ref[...] = h
        gate_ref[...] = _router_gates(h, wrg_ref, brg_ref, wre_ref, bre_ref)

    h = h_ref[...]
    lane = lax.broadcasted_iota(jnp.int32, (1, LANES), 1)
    g = jnp.sum(jnp.where(lane == e, gate_ref[...], 0.0), axis=-1, keepdims=True)
    act = jax.nn.silu(_dot(h, weg_ref[0])) * _dot(h, weu_ref[0])
    o_ref[...] += g * _dot(act.astype(BF16), wed_ref[0])

    @pl.when(e == pl.num_programs(1) - 1)
    def _():
        x2 = o_ref[...]
        hp = (x2 * _rms_scale(x2) * lnple_ref[...]).astype(BF16)
        gate = jax.nn.sigmoid(_dot(hp, wpg_ref[...]))
        o_ref[...] = x2 + gate * _dot(p_ref[...].astype(BF16), wpp_ref[...])


def _channel_stage(x, attn, sa, mb, p, wts, *, tile):
    n, d = x.shape
    n_exp, _, d_exp = wts['w_exp_gate'].shape
    row = lambda w: pl.BlockSpec((tile, w), lambda i, e: (i, 0))
    consts_a = [wts['w_attn_branch'], wts['w_out'], wts['ln_ffn'], wts['w_router_group'],
                wts['b_router_group'], wts['w_router_expert'], wts['b_router_expert']]
    consts_b = [wts['ln_ple'], wts['w_ple_gate'], wts['w_ple_proj']]
    return pl.pallas_call(
        _channel_kernel,
        grid=(n // tile, n_exp),
        in_specs=[row(d), row(attn.shape[1]), row(d), row(d), row(p.shape[1])]
        + [_const_spec(c.shape) for c in consts_a]
        + [pl.BlockSpec((1, d, d_exp), lambda i, e: (e, 0, 0)),
           pl.BlockSpec((1, d, d_exp), lambda i, e: (e, 0, 0)),
           pl.BlockSpec((1, d_exp, d), lambda i, e: (e, 0, 0))]
        + [_const_spec(c.shape) for c in consts_b],
        out_specs=row(d),
        out_shape=jax.ShapeDtypeStruct((n, d), F32),
        scratch_shapes=[pltpu.VMEM((tile, d), BF16), pltpu.VMEM((tile, LANES), F32)],
        compiler_params=pltpu.CompilerParams(dimension_semantics=("arbitrary", "arbitrary"),
                                             vmem_limit_bytes=VMEM_LIMIT),
        name="channel_stage",
    )(x, attn, sa, mb, p, *consts_a, wts['w_exp_gate'], wts['w_exp_up'], wts['w_exp_down'],
      *consts_b)


def _layer_weights(i, ln_mix, w_in, q_norm, k_norm, conv_w, w_attn_branch, w_conv_branch, w_out,
                   ln_ffn, w_router_group, b_router_group, w_router_expert, b_router_expert,
                   w_exp_gate, w_exp_up, w_exp_down, ln_ple, w_ple_gate, w_ple_proj):
    aw = ATTN_WIDTH
    pad_cols = lambda a: jnp.pad(a, ((0, 0), (0, LANES - a.shape[1])))
    head_id = jnp.arange(aw) // HEAD_DIM
    k_gain = jnp.tile(k_norm[i], N_HEADS)
    return dict(
        ln_mix=ln_mix[i][None, :], w_in=w_in[i].astype(BF16),
        w_kv_t=w_in[i][:, aw:3 * aw].T.astype(BF16),
        q_norm=jnp.tile(q_norm[i], N_HEADS)[None, :], k_norm=k_gain[None, :],
        k_norm_t=jnp.broadcast_to(k_gain[:, None], (aw, LANES)),
        head_sum=(head_id[:, None] == head_id[None, :]).astype(BF16),
        conv_w=conv_w[i], w_conv_branch=w_conv_branch[i].astype(BF16),
        w_attn_branch=w_attn_branch[i].astype(BF16), w_out=w_out[i].astype(BF16),
        ln_ffn=ln_ffn[i][None, :],
        w_router_group=pad_cols(w_router_group[i]).astype(BF16),
        b_router_group=pad_cols(b_router_group[i][None, :]),
        w_router_expert=pad_cols(w_router_expert[i]).astype(BF16),
        b_router_expert=pad_cols(b_router_expert[i][None, :]),
        w_exp_gate=w_exp_gate[i].astype(BF16), w_exp_up=w_exp_up[i].astype(BF16),
        w_exp_down=w_exp_down[i].astype(BF16),
        ln_ple=ln_ple[i][None, :], w_ple_gate=w_ple_gate[i].astype(BF16),
        w_ple_proj=w_ple_proj[i].astype(BF16))


def _feature_major(a):
    b, s, h, dh = a.shape
    return jnp.transpose(a, (0, 2, 3, 1)).reshape(b, h * dh, s)


def _position_major(a):
    b, _, s = a.shape
    return jnp.transpose(a.reshape(b, N_HEADS, HEAD_DIM, s), (0, 3, 1, 2))


def kernel(x_prompt, x_sample, cache_k, cache_v, state_conv, p_prompt, p_sample, ln_mix, w_in, q_norm, k_norm, conv_w, w_attn_branch, w_conv_branch, w_out, ln_ffn, w_router_group, b_router_group, w_router_expert, b_router_expert, w_exp_gate, w_exp_up, w_exp_down, ln_ple, w_ple_gate, w_ple_proj):
    depth = ln_mix.shape[0]
    nb, seq, d = x_prompt.shape
    nbs, seqs, _ = x_sample.shape
    aw = ATTN_WIDTH
    assert cache_k.shape[3:] == (N_HEADS, HEAD_DIM)
    assert w_router_group.shape[2] == N_GROUPS
    assert w_router_expert.shape[2] == N_GROUPS * EXPERTS_PER_GROUP
    tile = 512
    tri = (jnp.arange(ATTN_BLOCK)[:, None] >= jnp.arange(ATTN_BLOCK)[None, :]).astype(BF16)

    xp = x_prompt.reshape(nb * seq, d)
    xs = x_sample.reshape(nbs * seqs, d)
    outs = [[] for _ in range(6)]
    for i in range(depth):
        wts = _layer_weights(i, ln_mix, w_in, q_norm, k_norm, conv_w, w_attn_branch,
                             w_conv_branch, w_out, ln_ffn, w_router_group, b_router_group,
                             w_router_expert, b_router_expert, w_exp_gate, w_exp_up, w_exp_down,
                             ln_ple, w_ple_gate, w_ple_proj)

        (q, kt, vt, ktb, vtb, sa, mb), conv_new = _input_stage_prompt(xp, wts, seq=seq, tile=tile)
        attn = _attn_prompt(q, ktb, vtb, tri)
        xp = _channel_stage(xp, attn, sa, mb, p_prompt[i].reshape(nb * seq, -1), wts, tile=tile)
        outs[0].append(_position_major(kt))
        outs[1].append(_position_major(vt))
        outs[2].append(conv_new)

        (q, k, v, kb, vb, sa, mb), conv_new = _input_stage_sample(xs, state_conv[i], wts, seq=seqs)
        attn = _attn_sample(q.reshape(nbs, seqs, aw), kb.reshape(nbs, seqs, aw),
                            vb.reshape(nbs, seqs, aw), _feature_major(cache_k[i]),
                            _feature_major(cache_v[i]), tri, chunk=1024)
        xs = _channel_stage(xs, attn.reshape(nbs * seqs, aw), sa, mb,
                            p_sample[i].reshape(nbs * seqs, -1), wts, tile=nbs * seqs)
        outs[3].append(k.reshape(nbs, seqs, N_HEADS, HEAD_DIM))
        outs[4].append(v.reshape(nbs, seqs, N_HEADS, HEAD_DIM))
        outs[5].append(conv_new)

    kp, vp, cp, ks, vs, cs = [jnp.stack(o) for o in outs]
    return (xp.reshape(nb, seq, d), xs.reshape(nbs, seqs, d), kp, vp, cp, ks, vs, cs)
```

```python
import functools

import jax
import jax.numpy as jnp
from jax import lax
from jax.experimental import pallas as pl
from jax.experimental.pallas import tpu as pltpu

F32 = jnp.float32
BF16 = jnp.bfloat16

EPS = 1e-6
N_HEADS = 8
HEAD_DIM = 64
ATTN_WIDTH = N_HEADS * HEAD_DIM
N_GROUPS = 4
EXPERTS_PER_GROUP = 4
LANES = 128
ATTN_BLOCK = 256
Q_SCALE = HEAD_DIM ** -0.5 * 1.4426950408889634
VMEM_LIMIT = 56 * 1024 * 1024


def _dot(a, b):
    return jnp.dot(a, b, preferred_element_type=F32)


def _dot_nt(a, b):
    return lax.dot_general(a, b, (((1,), (1,)), ((), ())), preferred_element_type=F32)


def _rms_scale(x):
    return lax.rsqrt(jnp.mean(x * x, axis=-1, keepdims=True) + EPS)


def _const_spec(shape):
    nd = len(shape)
    return pl.BlockSpec(shape, lambda *_: (0,) * nd, pipeline_mode=pl.Buffered(1))


def _head_norm(z, gain, hsum_ref):
    ss = _dot((z * z).astype(BF16), hsum_ref[...])
    return z * lax.rsqrt(ss * (1.0 / HEAD_DIM) + EPS) * gain


def _input_stage_body(x, past1, past2, lnmix_ref, win_ref, qn_ref, hsum_ref, convw_ref, wcb_ref,
                      q_ref, sa_ref, mb_ref):
    aw = ATTN_WIDTH
    cw = convw_ref.shape[1]
    d = x.shape[1]
    h = (x * _rms_scale(x) * lnmix_ref[...]).astype(BF16)

    def proj(lo, width):
        return _dot(h, win_ref[:, lo:lo + width])

    q = _head_norm(proj(0, aw), qn_ref[...], hsum_ref)
    q_ref[...] = (q * Q_SCALE).astype(BF16)

    cb = proj(3 * aw, cw)
    u = proj(3 * aw + cw, cw) * proj(3 * aw + 2 * cw, cw)
    u1, u2 = past1(u), past2(u)
    conv_y = convw_ref[0:1, :] * u2 + convw_ref[1:2, :] * u1 + convw_ref[2:3, :] * u
    yb = _dot((cb * conv_y).astype(BF16), wcb_ref[...])
    ga = proj(3 * aw + 3 * cw, d)
    gb = proj(3 * aw + 3 * cw + d, d)
    sa_ref[...] = jax.nn.sigmoid(ga).astype(BF16)
    mb_ref[...] = (jax.nn.sigmoid(gb) * yb).astype(BF16)
    return h, proj, u


def _input_stage_prompt_kernel(x_ref, lnmix_ref, win_ref, wkvt_ref, qn_ref, knt_ref, hsum_ref,
                               convw_ref, wcb_ref, q_ref, kt_ref, vt_ref, ktb_ref, vtb_ref,
                               sa_ref, mb_ref, cnew_ref, tail_ref, *, tiles_per_seq):
    t = x_ref.shape[0]
    aw = ATTN_WIDTH

    @pl.when(pl.program_id(0) % tiles_per_seq == 0)
    def _():
        tail_ref[...] = jnp.zeros_like(tail_ref)

    def shifted(u, n):
        ext = jnp.concatenate([tail_ref[...], u], axis=0)
        return pltpu.roll(ext, n, axis=0)[8:, :]

    h, _, u = _input_stage_body(x_ref[...], lambda u: shifted(u, 1), lambda u: shifted(u, 2),
                                lnmix_ref, win_ref, qn_ref, hsum_ref, convw_ref, wcb_ref,
                                q_ref, sa_ref, mb_ref)
    tail_ref[...] = u[t - 8:, :]
    cnew_ref[0] = u[t - 8:, :]

    kt = _dot_nt(wkvt_ref[0:aw, :], h)
    k3 = kt.reshape(N_HEADS, HEAD_DIM, t)
    scale = lax.rsqrt(jnp.mean(k3 * k3, axis=1, keepdims=True) + EPS)
    kt = (k3 * scale).reshape(aw, t) * jnp.tile(knt_ref[...], (1, t // LANES))
    kt_ref[0] = kt
    ktb_ref[0] = kt.astype(BF16)
    vt = _dot_nt(wkvt_ref[aw:2 * aw, :], h)
    vt_ref[0] = vt
    vtb_ref[0] = vt.astype(BF16)


def _input_stage_sample_kernel(x_ref, pe0_ref, pe1_ref, lnmix_ref, win_ref, qn_ref, kn_ref,
                               hsum_ref, convw_ref, wcb_ref, q_ref, k_ref, v_ref, kb_ref,
                               vb_ref, sa_ref, mb_ref, u_ref, *, seq):
    t = x_ref.shape[0]
    aw = ATTN_WIDTH
    pos = lax.broadcasted_iota(jnp.int32, (t, 1), 0) % seq

    def past1(u):
        return jnp.where(pos == 0, pe1_ref[...], pltpu.roll(u, 1, axis=0))

    def past2(u):
        return jnp.where(pos == 0, pe0_ref[...],
                         jnp.where(pos == 1, pe1_ref[...], pltpu.roll(u, 2, axis=0)))

    _, proj, u = _input_stage_body(x_ref[...], past1, past2, lnmix_ref, win_ref, qn_ref,
                                   hsum_ref, convw_ref, wcb_ref, q_ref, sa_ref, mb_ref)
    u_ref[...] = u
    k = _head_norm(proj(aw, aw), kn_ref[...], hsum_ref)
    k_ref[...] = k
    kb_ref[...] = k.astype(BF16)
    v = proj(2 * aw, aw)
    v_ref[...] = v
    vb_ref[...] = v.astype(BF16)


def _input_stage_prompt(x, wts, *, seq, tile):
    n, d = x.shape
    nb = n // seq
    aw = ATTN_WIDTH
    cw = wts['conv_w'].shape[1]
    tiles_per_seq = seq // tile
    row = lambda w: pl.BlockSpec((tile, w), lambda i: (i, 0))
    feat = pl.BlockSpec((1, aw, tile), lambda i: (i // tiles_per_seq, 0, i % tiles_per_seq))
    consts = [wts['ln_mix'], wts['w_in'], wts['w_kv_t'], wts['q_norm'], wts['k_norm_t'],
              wts['head_sum'], wts['conv_w'], wts['w_conv_branch']]
    sds = jax.ShapeDtypeStruct
    outs = pl.pallas_call(
        functools.partial(_input_stage_prompt_kernel, tiles_per_seq=tiles_per_seq),
        grid=(n // tile,),
        in_specs=[row(d)] + [_const_spec(c.shape) for c in consts],
        out_specs=[row(aw), feat, feat, feat, feat, row(d), row(d),
                   pl.BlockSpec((1, 8, cw), lambda i: (i // tiles_per_seq, 0, 0))],
        out_shape=[sds((n, aw), BF16), sds((nb, aw, seq), F32), sds((nb, aw, seq), F32),
                   sds((nb, aw, seq), BF16), sds((nb, aw, seq), BF16), sds((n, d), BF16),
                   sds((n, d), BF16), sds((nb, 8, cw), F32)],
        scratch_shapes=[pltpu.VMEM((8, cw), F32)],
        compiler_params=pltpu.CompilerParams(dimension_semantics=("arbitrary",),
                                             vmem_limit_bytes=VMEM_LIMIT),
        name="input_stage_prompt",
    )(x, *consts)
    return outs[:7], outs[7][:, 6:, :]


def _input_stage_sample(x, state, wts, *, seq):
    n, d = x.shape
    aw = ATTN_WIDTH
    cw = wts['conv_w'].shape[1]
    pe0 = jnp.repeat(state[:, 0, :], seq, axis=0)
    pe1 = jnp.repeat(state[:, 1, :], seq, axis=0)
    consts = [wts['ln_mix'], wts['w_in'], wts['q_norm'], wts['k_norm'], wts['head_sum'],
              wts['conv_w'], wts['w_conv_branch']]
    full = lambda a: pl.BlockSpec(a.shape, lambda i: (0,) * a.ndim)
    sds = jax.ShapeDtypeStruct
    out_shape = [sds((n, aw), BF16), sds((n, aw), F32), sds((n, aw), F32), sds((n, aw), BF16),
                 sds((n, aw), BF16), sds((n, d), BF16), sds((n, d), BF16), sds((n, cw), F32)]
    outs = pl.pallas_call(
        functools.partial(_input_stage_sample_kernel, seq=seq),
        grid=(1,),
        in_specs=[full(x), full(pe0), full(pe1)] + [full(c) for c in consts],
        out_specs=[full(o) for o in out_shape],
        out_shape=out_shape,
        compiler_params=pltpu.CompilerParams(dimension_semantics=("arbitrary",),
                                             vmem_limit_bytes=VMEM_LIMIT),
        name="input_stage_sample",
    )(x, pe0, pe1, *consts)
    conv_new = outs[7].reshape(n // seq, seq, cw)[:, seq - 2:, :]
    return outs[:7], conv_new


def _stick_block(z, carry, tri_ref, keep):
    minus_abs = lax.bitcast_convert_type(
        lax.bitcast_convert_type(z, jnp.uint32) | jnp.uint32(0x80000000), F32)
    sp = jnp.maximum(z, 0.0) + jnp.log2(1.0 + jnp.exp2(minus_abs))
    if keep is not None:
        sp = jnp.where(keep, sp, 0.0)
    hi = sp.astype(BF16)
    lo = (sp - hi.astype(F32)).astype(BF16)
    incl = _dot(jnp.concatenate([hi, lo], axis=1), tri_ref[...])
    a = jnp.exp2(z - incl - carry)
    if keep is not None:
        a = jnp.where(keep, a, 0.0)
    return a.astype(BF16), carry + jnp.sum(sp, axis=-1, keepdims=True)


def _attn_prompt_kernel(q_ref, kt_ref, vt_ref, tri_ref, o_ref, acc_ref):
    tq = q_ref.shape[0]
    qi = pl.program_id(2)
    heads = range(LANES // HEAD_DIM)
    lane_head = lax.broadcasted_iota(jnp.int32, (1, LANES), 1) // HEAD_DIM
    row_head = lax.broadcasted_iota(jnp.int32, (LANES, tq), 0) // HEAD_DIM
    row = lax.broadcasted_iota(jnp.int32, (tq, tq), 0)
    col = lax.broadcasted_iota(jnp.int32, (tq, tq), 1)
    qm = [q_ref[...] * (lane_head == h).astype(BF16) for h in heads]

    def step(blocks, carries, keep):
        carries = list(carries)
        a_parts, vt_parts = [], []
        for j in blocks:
            start = pl.multiple_of(j * tq, tq)
            kt = kt_ref[0, :, pl.ds(start, tq)]
            vt = vt_ref[0, :, pl.ds(start, tq)]
            for h in heads:
                a, carries[h] = _stick_block(_dot(qm[h], kt), carries[h], tri_ref, keep)
                a_parts.append(a)
                vt_parts.append(jnp.where(row_head == h, vt, jnp.zeros_like(vt)))
        acc_ref[...] += _dot_nt(jnp.concatenate(a_parts, axis=1),
                                jnp.concatenate(vt_parts, axis=1))
        return tuple(carries)

    acc_ref[...] = jnp.zeros_like(acc_ref)
    carries = step([qi], [jnp.zeros((tq, 1), F32) for _ in heads], col < row)
    carries = lax.fori_loop(
        0, qi // 2, lambda p, c: step([qi - 1 - 2 * p, qi - 2 - 2 * p], c, None), carries)

    @pl.when(qi % 2 == 1)
    def _():
        step([0], carries, None)

    o_ref[...] = acc_ref[...].astype(o_ref.dtype)


def _attn_prompt(q, ktb, vtb, tri):
    n, aw = q.shape
    nb, _, seq = ktb.shape
    tq = ATTN_BLOCK
    nq = seq // tq
    qspec = pl.BlockSpec((tq, LANES), lambda b, hp, i: (b * nq + i, hp))
    kvspec = pl.BlockSpec((1, LANES, seq), lambda b, hp, i: (b, hp, 0))
    return pl.pallas_call(
        _attn_prompt_kernel,
        grid=(nb, aw // LANES, nq),
        in_specs=[qspec, kvspec, kvspec, pl.BlockSpec(tri.shape, lambda b, hp, i: (0, 0))],
        out_specs=qspec,
        out_shape=jax.ShapeDtypeStruct((n, aw), BF16),
        scratch_shapes=[pltpu.VMEM((tq, LANES), F32)],
        compiler_params=pltpu.CompilerParams(
            dimension_semantics=("arbitrary", "arbitrary", "arbitrary"),
            vmem_limit_bytes=VMEM_LIMIT),
        name="attn_prompt",
    )(q, ktb, vtb, tri)


def _attn_sample_kernel(q_ref, kn_ref, vn_ref, ck_ref, cv_ref, tri_ref, o_ref,
                        qrow_ref, carry_ref, acc_ref):
    tq, aw = q_ref.shape[1], q_ref.shape[2]
    rows = N_HEADS * tq
    tk = ATTN_BLOCK
    c = pl.program_id(1)

    @pl.when(c == 0)
    def _():
        row_head = lax.broadcasted_iota(jnp.int32, (rows, aw), 0) // tq
        lane_head = lax.broadcasted_iota(jnp.int32, (rows, aw), 1) // HEAD_DIM
        qt = jnp.concatenate([q_ref[0]] * N_HEADS, axis=0)
        qrow = jnp.where(row_head == lane_head, qt, jnp.zeros_like(qt))
        qrow_ref[...] = qrow
        pad = jnp.zeros((tk - tq, aw), BF16)
        kn = jnp.concatenate([kn_ref[0], pad], axis=0)
        vn = jnp.concatenate([vn_ref[0], pad], axis=0)
        qpos = lax.broadcasted_iota(jnp.int32, (rows, tk), 0) % tq
        kpos = lax.broadcasted_iota(jnp.int32, (rows, tk), 1)
        a, carry = _stick_block(_dot_nt(qrow, kn), jnp.zeros((rows, 1), F32), tri_ref,
                                kpos < qpos)
        carry_ref[...] = carry
        acc_ref[...] = _dot(a, vn)

    qrow = qrow_ref[...]
    nsub = ck_ref.shape[2] // tk
    for s in reversed(range(nsub)):
        kt = ck_ref[0, :, s * tk:(s + 1) * tk].astype(BF16)
        vt = cv_ref[0, :, s * tk:(s + 1) * tk].astype(BF16)
        a, carry = _stick_block(_dot(qrow, kt), carry_ref[...], tri_ref, None)
        carry_ref[...] = carry
        acc_ref[...] += _dot_nt(a, vt)

    @pl.when(c == pl.num_programs(1) - 1)
    def _():
        lane_head = lax.broadcasted_iota(jnp.int32, (tq, aw), 1) // HEAD_DIM
        out = jnp.zeros((tq, aw), F32)
        for head in range(N_HEADS):
            out = out + jnp.where(lane_head == head, acc_ref[head * tq:(head + 1) * tq, :], 0.0)
        o_ref[0] = out.astype(o_ref.dtype)


def _attn_sample(q, kn, vn, cache_kt, cache_vt, tri, *, chunk):
    nb, tq, aw = q.shape
    past = cache_kt.shape[2]
    nchunk = past // chunk
    new = pl.BlockSpec((1, tq, aw), lambda b, c: (b, 0, 0))
    cache = pl.BlockSpec((1, aw, chunk), lambda b, c: (b, 0, nchunk - 1 - c))
    rows = N_HEADS * tq
    return pl.pallas_call(
        _attn_sample_kernel,
        grid=(nb, nchunk),
        in_specs=[new, new, new, cache, cache, pl.BlockSpec(tri.shape, lambda b, c: (0, 0))],
        out_specs=new,
        out_shape=jax.ShapeDtypeStruct((nb, tq, aw), BF16),
        scratch_shapes=[pltpu.VMEM((rows, aw), BF16), pltpu.VMEM((rows, 1), F32),
                        pltpu.VMEM((rows, aw), F32)],
        compiler_params=pltpu.CompilerParams(dimension_semantics=("arbitrary", "arbitrary"),
                                             vmem_limit_bytes=VMEM_LIMIT),
        name="attn_sample",
    )(q, kn, vn, cache_kt, cache_vt, tri)


def _first_max(vals, lane):
    m = jnp.max(vals, axis=-1, keepdims=True)
    idx = jnp.min(jnp.where(vals == m, lane, float(LANES)), axis=-1, keepdims=True)
    return m, idx


def _router_gates(h, wrg_ref, brg_ref, wre_ref, bre_ref):
    lane = lax.broadcasted_iota(jnp.int32, (1, LANES), 1).astype(F32)
    neg = -jnp.inf
    lg = _dot(h, wrg_ref[...]) + brg_ref[...]
    lg = jnp.where(lane < N_GROUPS, lg, neg)
    mg, gidx = _first_max(lg, lane)
    p_sel = 1.0 / jnp.sum(jnp.exp(lg - mg), axis=-1, keepdims=True)
    le = _dot(h, wre_ref[...]) + bre_ref[...]
    in_group = jnp.floor(lane * (1.0 / EXPERTS_PER_GROUP)) == gidx
    le = jnp.where(in_group, le, neg)
    t1, i1 = _first_max(le, lane)
    t2, i2 = _first_max(jnp.where(lane == i1, neg, le), lane)
    e2 = jnp.exp(t2 - t1)
    w1 = 1.0 / (1.0 + e2)
    return p_sel * (jnp.where(lane == i1, w1, 0.0) + jnp.where(lane == i2, e2 * w1, 0.0))


def _channel_kernel(x_ref, attn_ref, sa_ref, mb_ref, p_ref, wab_ref, wout_ref, lnffn_ref,
                    wrg_ref, brg_ref, wre_ref, bre_ref, weg_ref, weu_ref, wed_ref,
                    lnple_ref, wpg_ref, wpp_ref, o_ref, h_ref, gate_ref):
    e = pl.program_id(1)

    @pl.when(e == 0)
    def _():
        ya = _dot(attn_ref[...], wab_ref[...])
        m = sa_ref[...].astype(F32) * ya + mb_ref[...].astype(F32)
        x1 = x_ref[...] + _dot(m.astype(BF16), wout_ref[...])
        o_ref[...] = x1
        h = (x1 * _rms_scale(x1) * lnffn_ref[...]).astype(BF16)
        h_ref[...] = h
        gate_ref[...] = _router_gates(h, wrg_ref, brg_ref, wre_ref, bre_ref)

    h = h_ref[...]
    lane = lax.broadcasted_iota(jnp.int32, (1, LANES), 1)
    g = jnp.sum(jnp.where(lane == e, gate_ref[...], 0.0), axis=-1, keepdims=True)
    act = jax.nn.silu(_dot(h, weg_ref[0])) * _dot(h, weu_ref[0])
    o_ref[...] += g * _dot(act.astype(BF16), wed_ref[0])

    @pl.when(e == pl.num_programs(1) - 1)
    def _():
        x2 = o_ref[...]
        hp = (x2 * _rms_scale(x2) * lnple_ref[...]).astype(BF16)
        gate = jax.nn.sigmoid(_dot(hp, wpg_ref[...]))
        o_ref[...] = x2 + gate * _dot(p_ref[...].astype(BF16), wpp_ref[...])


def _channel_stage(x, attn, sa, mb, p, wts, *, tile):
    n, d = x.shape
    n_exp, _, d_exp = wts['w_exp_gate'].shape
    row = lambda w: pl.BlockSpec((tile, w), lambda i, e: (i, 0))
    consts_a = [wts['w_attn_branch'], wts['w_out'], wts['ln_ffn'], wts['w_router_group'],
                wts['b_router_group'], wts['w_router_expert'], wts['b_router_expert']]
    consts_b = [wts['ln_ple'], wts['w_ple_gate'], wts['w_ple_proj']]
    return pl.pallas_call(
        _channel_kernel,
        grid=(n // tile, n_exp),
        in_specs=[row(d), row(attn.shape[1]), row(d), row(d), row(p.shape[1])]
        + [_const_spec(c.shape) for c in consts_a]
        + [pl.BlockSpec((1, d, d_exp), lambda i, e: (e, 0, 0)),
           pl.BlockSpec((1, d, d_exp), lambda i, e: (e, 0, 0)),
           pl.BlockSpec((1, d_exp, d), lambda i, e: (e, 0, 0))]
        + [_const_spec(c.shape) for c in consts_b],
        out_specs=row(d),
        out_shape=jax.ShapeDtypeStruct((n, d), F32),
        scratch_shapes=[pltpu.VMEM((tile, d), BF16), pltpu.VMEM((tile, LANES), F32)],
        compiler_params=pltpu.CompilerParams(dimension_semantics=("arbitrary", "arbitrary"),
                                             vmem_limit_bytes=VMEM_LIMIT),
        name="channel_stage",
    )(x, attn, sa, mb, p, *consts_a, wts['w_exp_gate'], wts['w_exp_up'], wts['w_exp_down'],
      *consts_b)


def _layer_weights(i, ln_mix, w_in, q_norm, k_norm, conv_w, w_attn_branch, w_conv_branch, w_out,
                   ln_ffn, w_router_group, b_router_group, w_router_expert, b_router_expert,
                   w_exp_gate, w_exp_up, w_exp_down, ln_ple, w_ple_gate, w_ple_proj):
    aw = ATTN_WIDTH
    pad_cols = lambda a: jnp.pad(a, ((0, 0), (0, LANES - a.shape[1])))
    head_id = jnp.arange(aw) // HEAD_DIM
    k_gain = jnp.tile(k_norm[i], N_HEADS)
    return dict(
        ln_mix=ln_mix[i][None, :], w_in=w_in[i].astype(BF16),
        w_kv_t=w_in[i][:, aw:3 * aw].T.astype(BF16),
        q_norm=jnp.tile(q_norm[i], N_HEADS)[None, :], k_norm=k_gain[None, :],
        k_norm_t=jnp.broadcast_to(k_gain[:, None], (aw, LANES)),
        head_sum=(head_id[:, None] == head_id[None, :]).astype(BF16),
        conv_w=conv_w[i], w_conv_branch=w_conv_branch[i].astype(BF16),
        w_attn_branch=w_attn_branch[i].astype(BF16), w_out=w_out[i].astype(BF16),
        ln_ffn=ln_ffn[i][None, :],
        w_router_group=pad_cols(w_router_group[i]).astype(BF16),
        b_router_group=pad_cols(b_router_group[i][None, :]),
        w_router_expert=pad_cols(w_router_expert[i]).astype(BF16),
        b_router_expert=pad_cols(b_router_expert[i][None, :]),
        w_exp_gate=w_exp_gate[i].astype(BF16), w_exp_up=w_exp_up[i].astype(BF16),
        w_exp_down=w_exp_down[i].astype(BF16),
        ln_ple=ln_ple[i][None, :], w_ple_gate=w_ple_gate[i].astype(BF16),
        w_ple_proj=w_ple_proj[i].astype(BF16))


def _feature_major(a):
    b, s, h, dh = a.shape
    return jnp.transpose(a, (0, 2, 3, 1)).reshape(b, h * dh, s)


def _position_major(a):
    b, _, s = a.shape
    return jnp.transpose(a.reshape(b, N_HEADS, HEAD_DIM, s), (0, 3, 1, 2))


def kernel(x_prompt, x_sample, cache_k, cache_v, state_conv, p_prompt, p_sample, ln_mix, w_in, q_norm, k_norm, conv_w, w_attn_branch, w_conv_branch, w_out, ln_ffn, w_router_group, b_router_group, w_router_expert, b_router_expert, w_exp_gate, w_exp_up, w_exp_down, ln_ple, w_ple_gate, w_ple_proj):
    depth = ln_mix.shape[0]
    nb, seq, d = x_prompt.shape
    nbs, seqs, _ = x_sample.shape
    aw = ATTN_WIDTH
    assert cache_k.shape[3:] == (N_HEADS, HEAD_DIM)
    assert w_router_group.shape[2] == N_GROUPS
    assert w_router_expert.shape[2] == N_GROUPS * EXPERTS_PER_GROUP
    tile = 512
    tri = (jnp.arange(ATTN_BLOCK)[:, None] >= jnp.arange(ATTN_BLOCK)[None, :]).astype(BF16)
    tri = jnp.concatenate([tri, tri], axis=0)

    xp = x_prompt.reshape(nb * seq, d)
    xs = x_sample.reshape(nbs * seqs, d)
    outs = [[] for _ in range(6)]
    for i in range(depth):
        wts = _layer_weights(i, ln_mix, w_in, q_norm, k_norm, conv_w, w_attn_branch,
                             w_conv_branch, w_out, ln_ffn, w_router_group, b_router_group,
                             w_router_expert, b_router_expert, w_exp_gate, w_exp_up, w_exp_down,
                             ln_ple, w_ple_gate, w_ple_proj)

        (q, kt, vt, ktb, vtb, sa, mb), conv_new = _input_stage_prompt(xp, wts, seq=seq, tile=tile)
        attn = _attn_prompt(q, ktb, vtb, tri)
        xp = _channel_stage(xp, attn, sa, mb, p_prompt[i].reshape(nb * seq, -1), wts, tile=tile)
        outs[0].append(_position_major(kt))
        outs[1].append(_position_major(vt))
        outs[2].append(conv_new)

        (q, k, v, kb, vb, sa, mb), conv_new = _input_stage_sample(xs, state_conv[i], wts, seq=seqs)
        attn = _attn_sample(q.reshape(nbs, seqs, aw), kb.reshape(nbs, seqs, aw),
                            vb.reshape(nbs, seqs, aw), _feature_major(cache_k[i]),
                            _feature_major(cache_v[i]), tri, chunk=1024)
        xs = _channel_stage(xs, attn.reshape(nbs * seqs, aw), sa, mb,
                            p_sample[i].reshape(nbs * seqs, -1), wts, tile=nbs * seqs)
        outs[3].append(k.reshape(nbs, seqs, N_HEADS, HEAD_DIM))
        outs[4].append(v.reshape(nbs, seqs, N_HEADS, HEAD_DIM))
        outs[5].append(conv_new)

    kp, vp, cp, ks, vs, cs = [jnp.stack(o) for o in outs]
    return (xp.reshape(nb, seq, d), xs.reshape(nbs, seqs, d), kp, vp, cp, ks, vs, cs)
```

```python
import functools

import jax
import jax.numpy as jnp
from jax import lax
from jax.experimental import pallas as pl
from jax.experimental.pallas import tpu as pltpu

F32 = jnp.float32
BF16 = jnp.bfloat16

EPS = 1e-6
N_HEADS = 8
HEAD_DIM = 64
ATTN_WIDTH = N_HEADS * HEAD_DIM
N_GROUPS = 4
EXPERTS_PER_GROUP = 4
GROUP_SHIFT = 20
LANES = 128
ATTN_BLOCK = 256
LOG2E_HI = 1.4426950216293335
LOG2E_LO = 1.925963033500011e-08
VMEM_LIMIT = 56 * 1024 * 1024


def _dot(a, b):
    return jnp.dot(a, b, preferred_element_type=F32)


def _dot_nt(a, b):
    return lax.dot_general(a, b, (((1,), (1,)), ((), ())), preferred_element_type=F32)


def _rms_scale(x):
    return lax.rsqrt(jnp.mean(x * x, axis=-1, keepdims=True) + EPS)


def _const_spec(shape):
    nd = len(shape)
    return pl.BlockSpec(shape, lambda *_: (0,) * nd, pipeline_mode=pl.Buffered(1))


def _head_norm(z, gain, hsum_ref):
    ss = _dot((z * z).astype(BF16), hsum_ref[...])
    return z * lax.rsqrt(ss * (1.0 / HEAD_DIM) + EPS) * gain


def _input_stage_body(x, past1, past2, lnmix_ref, win_ref, qn_ref, hsum_ref, convw_ref, wcb_ref,
                      q_ref, sa_ref, mb_ref):
    aw = ATTN_WIDTH
    cw = convw_ref.shape[1]
    d = x.shape[1]
    h = (x * _rms_scale(x) * lnmix_ref[...]).astype(BF16)

    def proj(lo, width):
        return _dot(h, win_ref[:, lo:lo + width])

    q = _head_norm(proj(0, aw), qn_ref[...], hsum_ref)
    q = q * (HEAD_DIM ** -0.5)
    q_ref[...] = (q * LOG2E_HI + q * LOG2E_LO).astype(BF16)

    cb = proj(3 * aw, cw)
    u = proj(3 * aw + cw, cw) * proj(3 * aw + 2 * cw, cw)
    u1, u2 = past1(u), past2(u)
    conv_y = convw_ref[0:1, :] * u2 + convw_ref[1:2, :] * u1 + convw_ref[2:3, :] * u
    yb = _dot((cb * conv_y).astype(BF16), wcb_ref[...])
    ga = proj(3 * aw + 3 * cw, d)
    gb = proj(3 * aw + 3 * cw + d, d)
    sa_ref[...] = jax.nn.sigmoid(ga).astype(BF16)
    mb_ref[...] = (jax.nn.sigmoid(gb) * yb).astype(BF16)
    return h, proj, u


def _input_stage_prompt_kernel(x_ref, lnmix_ref, win_ref, wkvt_ref, qn_ref, knt_ref, hsum_ref,
                               convw_ref, wcb_ref, q_ref, kt_ref, vt_ref, ktb_ref, vtb_ref,
                               sa_ref, mb_ref, cnew_ref, tail_ref, *, tiles_per_seq):
    t = x_ref.shape[0]
    aw = ATTN_WIDTH

    @pl.when(pl.program_id(0) % tiles_per_seq == 0)
    def _():
        tail_ref[...] = jnp.zeros_like(tail_ref)

    def shifted(u, n):
        ext = jnp.concatenate([tail_ref[...], u], axis=0)
        return pltpu.roll(ext, n, axis=0)[8:, :]

    h, _, u = _input_stage_body(x_ref[...], lambda u: shifted(u, 1), lambda u: shifted(u, 2),
                                lnmix_ref, win_ref, qn_ref, hsum_ref, convw_ref, wcb_ref,
                                q_ref, sa_ref, mb_ref)
    tail_ref[...] = u[t - 8:, :]
    cnew_ref[0] = u[t - 8:, :]

    kt = _dot_nt(wkvt_ref[0:aw, :], h)
    k3 = kt.reshape(N_HEADS, HEAD_DIM, t)
    scale = lax.rsqrt(jnp.mean(k3 * k3, axis=1, keepdims=True) + EPS)
    kt = (k3 * scale).reshape(aw, t) * jnp.tile(knt_ref[...], (1, t // LANES))
    kt_ref[0] = kt
    ktb_ref[0] = kt.astype(BF16)
    vt = _dot_nt(wkvt_ref[aw:2 * aw, :], h)
    vt_ref[0] = vt
    vtb_ref[0] = vt.astype(BF16)


def _input_stage_sample_kernel(x_ref, pe0_ref, pe1_ref, lnmix_ref, win_ref, qn_ref, kn_ref,
                               hsum_ref, convw_ref, wcb_ref, q_ref, k_ref, v_ref, kb_ref,
                               vb_ref, sa_ref, mb_ref, u_ref, *, seq):
    t = x_ref.shape[0]
    aw = ATTN_WIDTH
    pos = lax.broadcasted_iota(jnp.int32, (t, 1), 0) % seq

    def past1(u):
        return jnp.where(pos == 0, pe1_ref[...], pltpu.roll(u, 1, axis=0))

    def past2(u):
        return jnp.where(pos == 0, pe0_ref[...],
                         jnp.where(pos == 1, pe1_ref[...], pltpu.roll(u, 2, axis=0)))

    _, proj, u = _input_stage_body(x_ref[...], past1, past2, lnmix_ref, win_ref, qn_ref,
                                   hsum_ref, convw_ref, wcb_ref, q_ref, sa_ref, mb_ref)
    u_ref[...] = u
    k = _head_norm(proj(aw, aw), kn_ref[...], hsum_ref)
    k_ref[...] = k
    kb_ref[...] = k.astype(BF16)
    v = proj(2 * aw, aw)
    v_ref[...] = v
    vb_ref[...] = v.astype(BF16)


def _input_stage_prompt(x, wts, *, seq, tile):
    n, d = x.shape
    nb = n // seq
    aw = ATTN_WIDTH
    cw = wts['conv_w'].shape[1]
    tiles_per_seq = seq // tile
    row = lambda w: pl.BlockSpec((tile, w), lambda i: (i, 0))
    feat = pl.BlockSpec((1, aw, tile), lambda i: (i // tiles_per_seq, 0, i % tiles_per_seq))
    consts = [wts['ln_mix'], wts['w_in'], wts['w_kv_t'], wts['q_norm'], wts['k_norm_t'],
              wts['head_sum'], wts['conv_w'], wts['w_conv_branch']]
    sds = jax.ShapeDtypeStruct
    outs = pl.pallas_call(
        functools.partial(_input_stage_prompt_kernel, tiles_per_seq=tiles_per_seq),
        grid=(n // tile,),
        in_specs=[row(d)] + [_const_spec(c.shape) for c in consts],
        out_specs=[row(aw), feat, feat, feat, feat, row(d), row(d),
                   pl.BlockSpec((1, 8, cw), lambda i: (i // tiles_per_seq, 0, 0))],
        out_shape=[sds((n, aw), BF16), sds((nb, aw, seq), F32), sds((nb, aw, seq), F32),
                   sds((nb, aw, seq), BF16), sds((nb, aw, seq), BF16), sds((n, d), BF16),
                   sds((n, d), BF16), sds((nb, 8, cw), F32)],
        scratch_shapes=[pltpu.VMEM((8, cw), F32)],
        compiler_params=pltpu.CompilerParams(dimension_semantics=("arbitrary",),
                                             vmem_limit_bytes=VMEM_LIMIT),
        name="input_stage_prompt",
    )(x, *consts)
    return outs[:7], outs[7][:, 6:, :]


def _input_stage_sample(x, state, wts, *, seq):
    n, d = x.shape
    aw = ATTN_WIDTH
    cw = wts['conv_w'].shape[1]
    pe0 = jnp.repeat(state[:, 0, :], seq, axis=0)
    pe1 = jnp.repeat(state[:, 1, :], seq, axis=0)
    consts = [wts['ln_mix'], wts['w_in'], wts['q_norm'], wts['k_norm'], wts['head_sum'],
              wts['conv_w'], wts['w_conv_branch']]
    full = lambda a: pl.BlockSpec(a.shape, lambda i: (0,) * a.ndim)
    sds = jax.ShapeDtypeStruct
    out_shape = [sds((n, aw), BF16), sds((n, aw), F32), sds((n, aw), F32), sds((n, aw), BF16),
                 sds((n, aw), BF16), sds((n, d), BF16), sds((n, d), BF16), sds((n, cw), F32)]
    outs = pl.pallas_call(
        functools.partial(_input_stage_sample_kernel, seq=seq),
        grid=(1,),
        in_specs=[full(x), full(pe0), full(pe1)] + [full(c) for c in consts],
        out_specs=[full(o) for o in out_shape],
        out_shape=out_shape,
        compiler_params=pltpu.CompilerParams(dimension_semantics=("arbitrary",),
                                             vmem_limit_bytes=VMEM_LIMIT),
        name="input_stage_sample",
    )(x, pe0, pe1, *consts)
    conv_new = outs[7].reshape(n // seq, seq, cw)[:, seq - 2:, :]
    return outs[:7], conv_new


def _stick_block(z, carry, tri_ref, keep):
    sp = jnp.maximum(z, 0.0) + jnp.log2(1.0 + jnp.exp2(-jnp.abs(z)))
    if keep is not None:
        sp = jnp.where(keep, sp, 0.0)
    hi = sp.astype(BF16)
    lo = (sp - hi.astype(F32)).astype(BF16)
    incl = _dot(jnp.concatenate([hi, lo], axis=1), tri_ref[...])
    a = jnp.exp2(z - incl - carry)
    if keep is not None:
        a = jnp.where(keep, a, 0.0)
    return a.astype(BF16), carry + jnp.sum(sp, axis=-1, keepdims=True)


def _attn_prompt_kernel(q_ref, kt_ref, vt_ref, tri_ref, o_ref, acc_ref):
    tq = q_ref.shape[0]
    qi = pl.program_id(2)
    heads = range(LANES // HEAD_DIM)
    lane_head = lax.broadcasted_iota(jnp.int32, (1, LANES), 1) // HEAD_DIM
    row_head = lax.broadcasted_iota(jnp.int32, (LANES, tq), 0) // HEAD_DIM
    row = lax.broadcasted_iota(jnp.int32, (tq, tq), 0)
    col = lax.broadcasted_iota(jnp.int32, (tq, tq), 1)
    qm = [q_ref[...] * (lane_head == h).astype(BF16) for h in heads]

    def step(blocks, carries, keep):
        carries = list(carries)
        a_parts, vt_parts = [], []
        for j in blocks:
            start = pl.multiple_of(j * tq, tq)
            kt = kt_ref[0, :, pl.ds(start, tq)]
            vt = vt_ref[0, :, pl.ds(start, tq)]
            for h in heads:
                a, carries[h] = _stick_block(_dot(qm[h], kt), carries[h], tri_ref, keep)
                a_parts.append(a)
                vt_parts.append(jnp.where(row_head == h, vt, jnp.zeros_like(vt)))
        acc_ref[...] += _dot_nt(jnp.concatenate(a_parts, axis=1),
                                jnp.concatenate(vt_parts, axis=1))
        return tuple(carries)

    acc_ref[...] = jnp.zeros_like(acc_ref)
    carries = step([qi], [jnp.zeros((tq, 1), F32) for _ in heads], col < row)
    carries = lax.fori_loop(
        0, qi // 2, lambda p, c: step([qi - 1 - 2 * p, qi - 2 - 2 * p], c, None), carries)

    @pl.when(qi % 2 == 1)
    def _():
        step([0], carries, None)

    o_ref[...] = acc_ref[...].astype(o_ref.dtype)


def _attn_prompt(q, ktb, vtb, tri):
    n, aw = q.shape
    nb, _, seq = ktb.shape
    tq = ATTN_BLOCK
    nq = seq // tq
    qspec = pl.BlockSpec((tq, LANES), lambda b, hp, i: (b * nq + i, hp))
    kvspec = pl.BlockSpec((1, LANES, seq), lambda b, hp, i: (b, hp, 0))
    return pl.pallas_call(
        _attn_prompt_kernel,
        grid=(nb, aw // LANES, nq),
        in_specs=[qspec, kvspec, kvspec, pl.BlockSpec(tri.shape, lambda b, hp, i: (0, 0))],
        out_specs=qspec,
        out_shape=jax.ShapeDtypeStruct((n, aw), BF16),
        scratch_shapes=[pltpu.VMEM((tq, LANES), F32)],
        compiler_params=pltpu.CompilerParams(
            dimension_semantics=("arbitrary", "arbitrary", "arbitrary"),
            vmem_limit_bytes=VMEM_LIMIT),
        name="attn_prompt",
    )(q, ktb, vtb, tri)


def _attn_sample_kernel(q_ref, kn_ref, vn_ref, ck_ref, cv_ref, tri_ref, o_ref,
                        qrow_ref, carry_ref, acc_ref):
    tq, aw = q_ref.shape[1], q_ref.shape[2]
    rows = N_HEADS * tq
    tk = ATTN_BLOCK
    c = pl.program_id(1)

    @pl.when(c == 0)
    def _():
        row_head = lax.broadcasted_iota(jnp.int32, (rows, aw), 0) // tq
        lane_head = lax.broadcasted_iota(jnp.int32, (rows, aw), 1) // HEAD_DIM
        qt = jnp.concatenate([q_ref[0]] * N_HEADS, axis=0)
        qrow = jnp.where(row_head == lane_head, qt, jnp.zeros_like(qt))
        qrow_ref[...] = qrow
        pad = jnp.zeros((tk - tq, aw), BF16)
        kn = jnp.concatenate([kn_ref[0], pad], axis=0)
        vn = jnp.concatenate([vn_ref[0], pad], axis=0)
        qpos = lax.broadcasted_iota(jnp.int32, (rows, tk), 0) % tq
        kpos = lax.broadcasted_iota(jnp.int32, (rows, tk), 1)
        a, carry = _stick_block(_dot_nt(qrow, kn), jnp.zeros((rows, 1), F32), tri_ref,
                                kpos < qpos)
        carry_ref[...] = carry
        acc_ref[...] = _dot(a, vn)

    qrow = qrow_ref[...]
    nsub = ck_ref.shape[2] // tk
    for s in reversed(range(nsub)):
        kt = ck_ref[0, :, s * tk:(s + 1) * tk].astype(BF16)
        vt = cv_ref[0, :, s * tk:(s + 1) * tk].astype(BF16)
        a, carry = _stick_block(_dot(qrow, kt), carry_ref[...], tri_ref, None)
        carry_ref[...] = carry
        acc_ref[...] += _dot_nt(a, vt)

    @pl.when(c == pl.num_programs(1) - 1)
    def _():
        lane_head = lax.broadcasted_iota(jnp.int32, (tq, aw), 1) // HEAD_DIM
        out = jnp.zeros((tq, aw), F32)
        for head in range(N_HEADS):
            out = out + jnp.where(lane_head == head, acc_ref[head * tq:(head + 1) * tq, :], 0.0)
        o_ref[0] = out.astype(o_ref.dtype)


def _attn_sample(q, kn, vn, cache_kt, cache_vt, tri, *, chunk):
    nb, tq, aw = q.shape
    past = cache_kt.shape[2]
    nchunk = past // chunk
    new = pl.BlockSpec((1, tq, aw), lambda b, c: (b, 0, 0))
    cache = pl.BlockSpec((1, aw, chunk), lambda b, c: (b, 0, nchunk - 1 - c))
    rows = N_HEADS * tq
    return pl.pallas_call(
        _attn_sample_kernel,
        grid=(nb, nchunk),
        in_specs=[new, new, new, cache, cache, pl.BlockSpec(tri.shape, lambda b, c: (0, 0))],
        out_specs=new,
        out_shape=jax.ShapeDtypeStruct((nb, tq, aw), BF16),
        scratch_shapes=[pltpu.VMEM((rows, aw), BF16), pltpu.VMEM((rows, 1), F32),
                        pltpu.VMEM((rows, aw), F32)],
        compiler_params=pltpu.CompilerParams(dimension_semantics=("arbitrary", "arbitrary"),
                                             vmem_limit_bytes=VMEM_LIMIT),
        name="attn_sample",
    )(q, kn, vn, cache_kt, cache_vt, tri)


def _first_max(vals, lane):
    m = jnp.max(vals, axis=-1, keepdims=True)
    idx = jnp.min(jnp.where(vals == m, lane, float(LANES)), axis=-1, keepdims=True)
    return m, idx


def _router_gates(h, wrg_ref, brg_ref, wre_ref, bre_ref):
    lane = lax.broadcasted_iota(jnp.int32, (1, LANES), 1).astype(F32)
    neg = -jnp.inf
    lg = _dot(h, wrg_ref[...]) + brg_ref[...]
    lg = jnp.where(lane < N_GROUPS, lg, neg)
    mg, gidx = _first_max(lg, lane)
    p_sel = 1.0 / jnp.sum(jnp.exp(lg - mg), axis=-1, keepdims=True)
    le = _dot(h, wre_ref[...]) + bre_ref[...]
    in_group = jnp.floor(lane * (1.0 / EXPERTS_PER_GROUP)) == gidx
    le = jnp.where(in_group, le, neg)
    t1, i1 = _first_max(le, lane)
    t2, i2 = _first_max(jnp.where(lane == i1, neg, le), lane)
    e2 = jnp.exp(t2 - t1)
    w1 = 1.0 / (1.0 + e2)
    gate = p_sel * (jnp.where(lane == i1, w1, 0.0) + jnp.where(lane == i2, e2 * w1, 0.0))
    return gate, gidx


def _merge_route_kernel(x_ref, attn_ref, sa_ref, mb_ref, wab_ref, wout_ref, lnffn_ref,
                        wrg_ref, brg_ref, wre_ref, bre_ref, ltri_ref,
                        x1_ref, pay_ref, key_ref, cnt_out_ref, cnt_ref):
    t, d = x_ref.shape

    @pl.when(pl.program_id(0) == 0)
    def _():
        cnt_ref[...] = jnp.zeros_like(cnt_ref)

    ya = _dot(attn_ref[...], wab_ref[...])
    m = sa_ref[...].astype(F32) * ya + mb_ref[...].astype(F32)
    x1 = x_ref[...] + _dot(m.astype(BF16), wout_ref[...])
    x1_ref[...] = x1
    h = x1 * _rms_scale(x1) * lnffn_ref[...]
    gate, gidx = _router_gates(h.astype(BF16), wrg_ref, brg_ref, wre_ref, bre_ref)
    pay_ref[:, 0:d] = h
    pay_ref[:, d:d + LANES] = gate

    lane = lax.broadcasted_iota(jnp.int32, (1, LANES), 1).astype(F32)
    onehot = lane == gidx
    before = _dot(ltri_ref[...], onehot.astype(BF16)) + cnt_ref[...]
    rank = jnp.sum(jnp.where(onehot, before, 0.0), axis=-1, keepdims=True)
    cnt = cnt_ref[...] + jnp.sum(onehot.astype(F32), axis=0, keepdims=True)
    cnt_ref[...] = cnt
    cnt_out_ref[...] = cnt.astype(jnp.int32)
    key = gidx * float(1 << GROUP_SHIFT) + rank
    key_ref[0] = jnp.broadcast_to(key, (t, LANES)).T[0:1, :].astype(jnp.int32)


def _merge_route(x, attn, sa, mb, wts, *, tile):
    n, d = x.shape
    assert n < (1 << GROUP_SHIFT) and N_GROUPS << GROUP_SHIFT <= 1 << 24
    row = lambda c: pl.BlockSpec((tile, c), lambda i: (i, 0))
    ltri = (jnp.arange(tile)[:, None] > jnp.arange(tile)[None, :]).astype(BF16)
    consts = [wts['w_attn_branch'], wts['w_out'], wts['ln_ffn'], wts['w_router_group'],
              wts['b_router_group'], wts['w_router_expert'], wts['b_router_expert'], ltri]
    sds = jax.ShapeDtypeStruct
    x1, pay, keys, cnt = pl.pallas_call(
        _merge_route_kernel,
        grid=(n // tile,),
        in_specs=[row(d), row(attn.shape[1]), row(d), row(d)]
        + [_const_spec(c.shape) for c in consts],
        out_specs=[row(d), row(d + LANES), pl.BlockSpec((1, 1, tile), lambda i: (i, 0, 0)),
                   pl.BlockSpec((1, LANES), lambda i: (0, 0))],
        out_shape=[sds((n, d), F32), sds((n, d + LANES), F32),
                   sds((n // tile, 1, tile), jnp.int32), sds((1, LANES), jnp.int32)],
        scratch_shapes=[pltpu.VMEM((1, LANES), F32)],
        compiler_params=pltpu.CompilerParams(dimension_semantics=("arbitrary",),
                                             vmem_limit_bytes=VMEM_LIMIT),
        name="merge_route",
    )(x, attn, sa, mb, *consts)
    return x1, pay, keys, cnt[0, :N_GROUPS]


def _row_positions(key_row, off_ref):
    grp = key_row >> GROUP_SHIFT
    pos = key_row & ((1 << GROUP_SHIFT) - 1)
    for g in range(N_GROUPS):
        pos = pos + jnp.where(grp == g, off_ref[g], 0)
    return pos


def _to_smem(src_ref, dst_ref, sem):
    cp = pltpu.make_async_copy(src_ref, dst_ref, sem)
    cp.start()
    cp.wait()


def _dispatch_kernel(off_ref, cnt_ref, key_ref, pay_ref, xs_ref, posv_ref, pos_smem, zero_ref,
                     sem_s, sem_row, sem_pad, *, tile_m):
    i = pl.program_id(0)
    last = pl.num_programs(0) - 1
    t = key_ref.shape[2]
    slot = i % 2
    posv_ref[...] = _row_positions(key_ref[0], off_ref)
    _to_smem(posv_ref, pos_smem, sem_s)
    base = i * t

    def copy_row(r, _):
        pltpu.make_async_copy(pay_ref.at[pl.ds(base + r, 1), :],
                              xs_ref.at[pl.ds(pos_smem[0, r], 1), :], sem_row.at[slot]).start()
        return 0

    lax.fori_loop(0, t, copy_row, 0, unroll=8)

    def wait_rows(s):
        pltpu.make_async_copy(pay_ref.at[pl.ds(0, t), :], xs_ref.at[pl.ds(0, t), :],
                              sem_row.at[s]).wait()

    @pl.when(i > 0)
    def _():
        wait_rows(1 - slot)

    @pl.when(i == last)
    def _():
        wait_rows(slot)
        zero_ref[...] = jnp.zeros_like(zero_ref)
        pads = []

        def pad(cond, start, size):
            pads.append((cond, pltpu.make_async_copy(
                zero_ref.at[pl.ds(0, size), :], xs_ref.at[pl.ds(start, size), :], sem_pad)))

        used = 0
        for g in range(N_GROUPS):
            c = cnt_ref[g]
            c8 = (c + 7) >> 3 << 3
            end = (c + tile_m - 1) // tile_m * tile_m
            for k in range(7):
                pad(c + k < c8, off_ref[g] + c + k, 1)
            start, size = off_ref[g] + c8, 8
            while size < tile_m:
                cond = ((end - c8) & size) != 0
                pad(cond, pl.multiple_of(start, 8), size)
                start = start + jnp.where(cond, size, 0)
                size *= 2
            used = off_ref[g] + end
        for k in range(N_GROUPS):
            start = used + k * tile_m
            pad(start < xs_ref.shape[0], pl.multiple_of(start, tile_m), tile_m)
        for cond, cp in pads:
            pl.when(cond)(cp.start)
        for cond, cp in pads:
            pl.when(cond)(cp.wait)


def _dispatch(pay, keys, off, counts, *, tile_m):
    n, w = pay.shape
    nt, _, tile = keys.shape
    rows = n + N_GROUPS * tile_m
    return pl.pallas_call(
        functools.partial(_dispatch_kernel, tile_m=tile_m),
        grid_spec=pltpu.PrefetchScalarGridSpec(
            num_scalar_prefetch=2,
            grid=(nt,),
            in_specs=[pl.BlockSpec((1, 1, tile), lambda i, off, cnt: (i, 0, 0)),
                      pl.BlockSpec(memory_space=pl.ANY)],
            out_specs=pl.BlockSpec(memory_space=pl.ANY),
            scratch_shapes=[pltpu.VMEM((1, tile), jnp.int32), pltpu.SMEM((1, tile), jnp.int32),
                            pltpu.VMEM((tile_m, w), F32), pltpu.SemaphoreType.DMA,
                            pltpu.SemaphoreType.DMA((2,)), pltpu.SemaphoreType.DMA]),
        out_shape=jax.ShapeDtypeStruct((rows, w), F32),
        compiler_params=pltpu.CompilerParams(dimension_semantics=("arbitrary",),
                                             vmem_limit_bytes=VMEM_LIMIT),
        name="dispatch",
    )(off, counts, keys, pay)


def _group_experts_kernel(grp_ref, xs_ref, weg_ref, weu_ref, wed_ref, y_ref):
    d = y_ref.shape[1]
    grp = grp_ref[pl.program_id(0)]
    h = xs_ref[:, 0:d].astype(BF16)
    gate = xs_ref[:, d:d + LANES]
    lane = lax.broadcasted_iota(jnp.int32, (1, LANES), 1)
    acts = []
    for j in range(EXPERTS_PER_GROUP):
        g = jnp.sum(jnp.where(lane == grp * EXPERTS_PER_GROUP + j, gate, 0.0),
                    axis=-1, keepdims=True)
        act = jax.nn.silu(_dot(h, weg_ref[0, j])) * _dot(h, weu_ref[0, j])
        acts.append((g * act).astype(BF16))
    y_ref[...] = _dot(jnp.concatenate(acts, axis=1), wed_ref[0])


def _group_experts(xs, tile_grp, wts, *, tile_m):
    d = xs.shape[1] - LANES
    n_exp, _, d_exp = wts['w_exp_gate'].shape
    epg = EXPERTS_PER_GROUP
    weg = wts['w_exp_gate'].reshape(N_GROUPS, epg, d, d_exp)
    weu = wts['w_exp_up'].reshape(N_GROUPS, epg, d, d_exp)
    wed = wts['w_exp_down'].reshape(N_GROUPS, epg * d_exp, d)
    return pl.pallas_call(
        _group_experts_kernel,
        grid_spec=pltpu.PrefetchScalarGridSpec(
            num_scalar_prefetch=1,
            grid=(xs.shape[0] // tile_m,),
            in_specs=[pl.BlockSpec((tile_m, d + LANES), lambda i, gr: (i, 0)),
                      pl.BlockSpec((1, epg, d, d_exp), lambda i, gr: (gr[i], 0, 0, 0)),
                      pl.BlockSpec((1, epg, d, d_exp), lambda i, gr: (gr[i], 0, 0, 0)),
                      pl.BlockSpec((1, epg * d_exp, d), lambda i, gr: (gr[i], 0, 0))],
            out_specs=pl.BlockSpec((tile_m, d), lambda i, gr: (i, 0))),
        out_shape=jax.ShapeDtypeStruct((xs.shape[0], d), F32),
        compiler_params=pltpu.CompilerParams(dimension_semantics=("arbitrary",),
                                             vmem_limit_bytes=VMEM_LIMIT),
        name="group_experts",
    )(tile_grp, xs, weg, weu, wed)


def _combine_ple_kernel(off_ref, x1_ref, p_ref, keyc_ref, keyn_ref, lnple_ref, wpg_ref, wpp_ref,
                        y_ref, o_ref, ybuf_ref, posv_ref, pos_smem, sem_y, sem_s):
    i = pl.program_id(0)
    n = pl.num_programs(0)
    t = x1_ref.shape[0]
    slot = i % 2

    def fetch(key_ref, s):
        posv_ref[...] = _row_positions(key_ref[0], off_ref)
        _to_smem(posv_ref, pos_smem.at[pl.ds(s, 1), :], sem_s)

        def gather_row(r, _):
            pltpu.make_async_copy(y_ref.at[pl.ds(pos_smem[s, r], 1), :],
                                  ybuf_ref.at[s, pl.ds(r, 1), :], sem_y.at[s]).start()
            return 0

        lax.fori_loop(0, t, gather_row, 0, unroll=8)

    @pl.when(i == 0)
    def _():
        fetch(keyc_ref, 0)

    @pl.when(i + 1 < n)
    def _():
        fetch(keyn_ref, 1 - slot)

    pltpu.make_async_copy(y_ref.at[pl.ds(0, t), :], ybuf_ref.at[slot], sem_y.at[slot]).wait()
    x2 = x1_ref[...] + ybuf_ref[slot]
    hp = (x2 * _rms_scale(x2) * lnple_ref[...]).astype(BF16)
    gate = jax.nn.sigmoid(_dot(hp, wpg_ref[...]))
    o_ref[...] = x2 + gate * _dot(p_ref[...].astype(BF16), wpp_ref[...])


def _combine_ple(x1, p, keys, off, y, wts):
    n, d = x1.shape
    nt, _, tile = keys.shape
    row = lambda c: pl.BlockSpec((tile, c), lambda i, off: (i, 0))
    consts = [wts['ln_ple'], wts['w_ple_gate'], wts['w_ple_proj']]
    return pl.pallas_call(
        _combine_ple_kernel,
        grid_spec=pltpu.PrefetchScalarGridSpec(
            num_scalar_prefetch=1,
            grid=(nt,),
            in_specs=[row(d), row(p.shape[1]),
                      pl.BlockSpec((1, 1, tile), lambda i, off: (i, 0, 0)),
                      pl.BlockSpec((1, 1, tile),
                                   lambda i, off: (jnp.minimum(i + 1, nt - 1), 0, 0))]
            + [_const_spec(c.shape) for c in consts] + [pl.BlockSpec(memory_space=pl.ANY)],
            out_specs=row(d),
            scratch_shapes=[pltpu.VMEM((2, tile, d), F32), pltpu.VMEM((1, tile), jnp.int32),
                            pltpu.SMEM((2, tile), jnp.int32), pltpu.SemaphoreType.DMA((2,)),
                            pltpu.SemaphoreType.DMA]),
        out_shape=jax.ShapeDtypeStruct((n, d), F32),
        compiler_params=pltpu.CompilerParams(dimension_semantics=("arbitrary",),
                                             vmem_limit_bytes=VMEM_LIMIT),
        name="combine_ple",
    )(off, x1, p, keys, keys, *consts, y)


def _channel_stage(x, attn, sa, mb, p, wts, *, tile, tile_m):
    n = x.shape[0]
    assert n % tile == 0 and n % tile_m == 0
    x1, pay, keys, counts = _merge_route(x, attn, sa, mb, wts, tile=tile)
    ends = jnp.cumsum((counts + tile_m - 1) // tile_m * tile_m)
    off = jnp.concatenate([jnp.zeros((1,), jnp.int32), ends[:-1]])
    xs = _dispatch(pay, keys, off, counts, tile_m=tile_m)
    tile_start = jnp.arange(xs.shape[0] // tile_m, dtype=jnp.int32) * tile_m
    tile_grp = jnp.minimum(jnp.sum(tile_start[:, None] >= ends[None, :], axis=1), N_GROUPS - 1)
    y = _group_experts(xs, tile_grp.astype(jnp.int32), wts, tile_m=tile_m)
    return _combine_ple(x1, p, keys, off, y, wts)


def _layer_weights(i, ln_mix, w_in, q_norm, k_norm, conv_w, w_attn_branch, w_conv_branch, w_out,
                   ln_ffn, w_router_group, b_router_group, w_router_expert, b_router_expert,
                   w_exp_gate, w_exp_up, w_exp_down, ln_ple, w_ple_gate, w_ple_proj):
    aw = ATTN_WIDTH
    pad_cols = lambda a: jnp.pad(a, ((0, 0), (0, LANES - a.shape[1])))
    head_id = jnp.arange(aw) // HEAD_DIM
    k_gain = jnp.tile(k_norm[i], N_HEADS)
    return dict(
        ln_mix=ln_mix[i][None, :], w_in=w_in[i].astype(BF16),
        w_kv_t=w_in[i][:, aw:3 * aw].T.astype(BF16),
        q_norm=jnp.tile(q_norm[i], N_HEADS)[None, :], k_norm=k_gain[None, :],
        k_norm_t=jnp.broadcast_to(k_gain[:, None], (aw, LANES)),
        head_sum=(head_id[:, None] == head_id[None, :]).astype(BF16),
        conv_w=conv_w[i], w_conv_branch=w_conv_branch[i].astype(BF16),
        w_attn_branch=w_attn_branch[i].astype(BF16), w_out=w_out[i].astype(BF16),
        ln_ffn=ln_ffn[i][None, :],
        w_router_group=pad_cols(w_router_group[i]).astype(BF16),
        b_router_group=pad_cols(b_router_group[i][None, :]),
        w_router_expert=pad_cols(w_router_expert[i]).astype(BF16),
        b_router_expert=pad_cols(b_router_expert[i][None, :]),
        w_exp_gate=w_exp_gate[i].astype(BF16), w_exp_up=w_exp_up[i].astype(BF16),
        w_exp_down=w_exp_down[i].astype(BF16),
        ln_ple=ln_ple[i][None, :], w_ple_gate=w_ple_gate[i].astype(BF16),
        w_ple_proj=w_ple_proj[i].astype(BF16))


def _feature_major(a):
    b, s, h, dh = a.shape
    return jnp.transpose(a, (0, 2, 3, 1)).reshape(b, h * dh, s)


def _position_major(a):
    b, _, s = a.shape
    return jnp.transpose(a.reshape(b, N_HEADS, HEAD_DIM, s), (0, 3, 1, 2))


def kernel(x_prompt, x_sample, cache_k, cache_v, state_conv, p_prompt, p_sample, ln_mix, w_in, q_norm, k_norm, conv_w, w_attn_branch, w_conv_branch, w_out, ln_ffn, w_router_group, b_router_group, w_router_expert, b_router_expert, w_exp_gate, w_exp_up, w_exp_down, ln_ple, w_ple_gate, w_ple_proj):
    depth = ln_mix.shape[0]
    nb, seq, d = x_prompt.shape
    nbs, seqs, _ = x_sample.shape
    aw = ATTN_WIDTH
    assert cache_k.shape[3:] == (N_HEADS, HEAD_DIM)
    assert w_router_group.shape[2] == N_GROUPS
    assert w_router_expert.shape[2] == N_GROUPS * EXPERTS_PER_GROUP
    tile = 512
    tri = (jnp.arange(ATTN_BLOCK)[:, None] >= jnp.arange(ATTN_BLOCK)[None, :]).astype(BF16)
    tri = jnp.concatenate([tri, tri], axis=0)

    xp = x_prompt.reshape(nb * seq, d)
    xs = x_sample.reshape(nbs * seqs, d)
    outs = [[] for _ in range(6)]
    for i in range(depth):
        wts = _layer_weights(i, ln_mix, w_in, q_norm, k_norm, conv_w, w_attn_branch,
                             w_conv_branch, w_out, ln_ffn, w_router_group, b_router_group,
                             w_router_expert, b_router_expert, w_exp_gate, w_exp_up, w_exp_down,
                             ln_ple, w_ple_gate, w_ple_proj)

        (q, kt, vt, ktb, vtb, sa, mb), conv_new = _input_stage_prompt(xp, wts, seq=seq, tile=tile)
        attn = _attn_prompt(q, ktb, vtb, tri)
        xp = _channel_stage(xp, attn, sa, mb, p_prompt[i].reshape(nb * seq, -1), wts, tile=tile,
                            tile_m=512)
        outs[0].append(_position_major(kt))
        outs[1].append(_position_major(vt))
        outs[2].append(conv_new)

        (q, k, v, kb, vb, sa, mb), conv_new = _input_stage_sample(xs, state_conv[i], wts, seq=seqs)
        attn = _attn_sample(q.reshape(nbs, seqs, aw), kb.reshape(nbs, seqs, aw),
                            vb.reshape(nbs, seqs, aw), _feature_major(cache_k[i]),
                            _feature_major(cache_v[i]), tri, chunk=1024)
        xs = _channel_stage(xs, attn.reshape(nbs * seqs, aw), sa, mb,
                            p_sample[i].reshape(nbs * seqs, -1), wts, tile=nbs * seqs,
                            tile_m=min(256, nbs * seqs))
        outs[3].append(k.reshape(nbs, seqs, N_HEADS, HEAD_DIM))
        outs[4].append(v.reshape(nbs, seqs, N_HEADS, HEAD_DIM))
        outs[5].append(conv_new)

    kp, vp, cp, ks, vs, cs = [jnp.stack(o) for o in outs]
    return (xp.reshape(nb, seq, d), xs.reshape(nbs, seqs, d), kp, vp, cp, ks, vs, cs)
```

```python
import functools

import jax
import jax.numpy as jnp
from jax import lax
from jax.experimental import pallas as pl
from jax.experimental.pallas import tpu as pltpu

F32 = jnp.float32
BF16 = jnp.bfloat16

EPS = 1e-6
N_HEADS = 8
HEAD_DIM = 64
ATTN_WIDTH = N_HEADS * HEAD_DIM
N_GROUPS = 4
EXPERTS_PER_GROUP = 4
GROUP_SHIFT = 20
DISPATCH_RING = 3
LANES = 128
ATTN_BLOCK = 256
LOG2E_HI = 1.4426950216293335
LOG2E_LO = 1.925963033500011e-08
VMEM_LIMIT = 56 * 1024 * 1024


def _dot(a, b):
    return jnp.dot(a, b, preferred_element_type=F32)


def _dot_nt(a, b):
    return lax.dot_general(a, b, (((1,), (1,)), ((), ())), preferred_element_type=F32)


def _rms_scale(x):
    return lax.rsqrt(jnp.mean(x * x, axis=-1, keepdims=True) + EPS)


def _const_spec(shape):
    nd = len(shape)
    return pl.BlockSpec(shape, lambda *_: (0,) * nd, pipeline_mode=pl.Buffered(1))


def _head_norm(z, gain, hsum_ref):
    ss = _dot((z * z).astype(BF16), hsum_ref[...])
    return z * lax.rsqrt(ss * (1.0 / HEAD_DIM) + EPS) * gain


def _input_stage_body(x, past1, past2, lnmix_ref, win_ref, qn_ref, hsum_ref, convw_ref, wcb_ref,
                      q_ref, sa_ref, mb_ref):
    aw = ATTN_WIDTH
    cw = convw_ref.shape[1]
    d = x.shape[1]
    h = (x * _rms_scale(x) * lnmix_ref[...]).astype(BF16)

    def proj(lo, width):
        return _dot(h, win_ref[:, lo:lo + width])

    q = _head_norm(proj(0, aw), qn_ref[...], hsum_ref)
    q = q * (HEAD_DIM ** -0.5)
    q_ref[...] = (q * LOG2E_HI + q * LOG2E_LO).astype(BF16)

    cb = proj(3 * aw, cw)
    u = proj(3 * aw + cw, cw) * proj(3 * aw + 2 * cw, cw)
    u1, u2 = past1(u), past2(u)
    conv_y = convw_ref[0:1, :] * u2 + convw_ref[1:2, :] * u1 + convw_ref[2:3, :] * u
    yb = _dot((cb * conv_y).astype(BF16), wcb_ref[...])
    ga = proj(3 * aw + 3 * cw, d)
    gb = proj(3 * aw + 3 * cw + d, d)
    sa_ref[...] = jax.nn.sigmoid(ga).astype(BF16)
    mb_ref[...] = (jax.nn.sigmoid(gb) * yb).astype(BF16)
    return h, proj, u


def _input_stage_prompt_kernel(x_ref, lnmix_ref, win_ref, wkvt_ref, qn_ref, knt_ref, hsum_ref,
                               convw_ref, wcb_ref, q_ref, kt_ref, vt_ref, ktb_ref, vtb_ref,
                               sa_ref, mb_ref, cnew_ref, tail_ref, *, tiles_per_seq):
    t = x_ref.shape[0]
    aw = ATTN_WIDTH

    @pl.when(pl.program_id(0) % tiles_per_seq == 0)
    def _():
        tail_ref[...] = jnp.zeros_like(tail_ref)

    def shifted(u, n):
        ext = jnp.concatenate([tail_ref[...], u], axis=0)
        return pltpu.roll(ext, n, axis=0)[8:, :]

    h, _, u = _input_stage_body(x_ref[...], lambda u: shifted(u, 1), lambda u: shifted(u, 2),
                                lnmix_ref, win_ref, qn_ref, hsum_ref, convw_ref, wcb_ref,
                                q_ref, sa_ref, mb_ref)
    tail_ref[...] = u[t - 8:, :]
    cnew_ref[0] = u[t - 8:, :]

    kt = _dot_nt(wkvt_ref[0:aw, :], h)
    k3 = kt.reshape(N_HEADS, HEAD_DIM, t)
    scale = lax.rsqrt(jnp.mean(k3 * k3, axis=1, keepdims=True) + EPS)
    kt = (k3 * scale).reshape(aw, t) * jnp.tile(knt_ref[...], (1, t // LANES))
    kt_ref[0] = kt
    ktb_ref[0] = kt.astype(BF16)
    vt = _dot_nt(wkvt_ref[aw:2 * aw, :], h)
    vt_ref[0] = vt
    vtb_ref[0] = vt.astype(BF16)


def _input_stage_sample_kernel(x_ref, pe0_ref, pe1_ref, lnmix_ref, win_ref, qn_ref, kn_ref,
                               hsum_ref, convw_ref, wcb_ref, q_ref, k_ref, v_ref, kb_ref,
                               vb_ref, sa_ref, mb_ref, u_ref, *, seq):
    t = x_ref.shape[0]
    aw = ATTN_WIDTH
    pos = lax.broadcasted_iota(jnp.int32, (t, 1), 0) % seq

    def past1(u):
        return jnp.where(pos == 0, pe1_ref[...], pltpu.roll(u, 1, axis=0))

    def past2(u):
        return jnp.where(pos == 0, pe0_ref[...],
                         jnp.where(pos == 1, pe1_ref[...], pltpu.roll(u, 2, axis=0)))

    _, proj, u = _input_stage_body(x_ref[...], past1, past2, lnmix_ref, win_ref, qn_ref,
                                   hsum_ref, convw_ref, wcb_ref, q_ref, sa_ref, mb_ref)
    u_ref[...] = u
    k = _head_norm(proj(aw, aw), kn_ref[...], hsum_ref)
    k_ref[...] = k
    kb_ref[...] = k.astype(BF16)
    v = proj(2 * aw, aw)
    v_ref[...] = v
    vb_ref[...] = v.astype(BF16)


def _input_stage_prompt(x, wts, *, seq, tile):
    n, d = x.shape
    nb = n // seq
    aw = ATTN_WIDTH
    cw = wts['conv_w'].shape[1]
    tiles_per_seq = seq // tile
    row = lambda w: pl.BlockSpec((tile, w), lambda i: (i, 0))
    feat = pl.BlockSpec((1, aw, tile), lambda i: (i // tiles_per_seq, 0, i % tiles_per_seq))
    consts = [wts['ln_mix'], wts['w_in'], wts['w_kv_t'], wts['q_norm'], wts['k_norm_t'],
              wts['head_sum'], wts['conv_w'], wts['w_conv_branch']]
    sds = jax.ShapeDtypeStruct
    outs = pl.pallas_call(
        functools.partial(_input_stage_prompt_kernel, tiles_per_seq=tiles_per_seq),
        grid=(n // tile,),
        in_specs=[row(d)] + [_const_spec(c.shape) for c in consts],
        out_specs=[row(aw), feat, feat, feat, feat, row(d), row(d),
                   pl.BlockSpec((1, 8, cw), lambda i: (i // tiles_per_seq, 0, 0))],
        out_shape=[sds((n, aw), BF16), sds((nb, aw, seq), F32), sds((nb, aw, seq), F32),
                   sds((nb, aw, seq), BF16), sds((nb, aw, seq), BF16), sds((n, d), BF16),
                   sds((n, d), BF16), sds((nb, 8, cw), F32)],
        scratch_shapes=[pltpu.VMEM((8, cw), F32)],
        compiler_params=pltpu.CompilerParams(dimension_semantics=("arbitrary",),
                                             vmem_limit_bytes=VMEM_LIMIT),
        name="input_stage_prompt",
    )(x, *consts)
    return outs[:7], outs[7][:, 6:, :]


def _input_stage_sample(x, state, wts, *, seq):
    n, d = x.shape
    aw = ATTN_WIDTH
    cw = wts['conv_w'].shape[1]
    pe0 = jnp.repeat(state[:, 0, :], seq, axis=0)
    pe1 = jnp.repeat(state[:, 1, :], seq, axis=0)
    consts = [wts['ln_mix'], wts['w_in'], wts['q_norm'], wts['k_norm'], wts['head_sum'],
              wts['conv_w'], wts['w_conv_branch']]
    full = lambda a: pl.BlockSpec(a.shape, lambda i: (0,) * a.ndim)
    sds = jax.ShapeDtypeStruct
    out_shape = [sds((n, aw), BF16), sds((n, aw), F32), sds((n, aw), F32), sds((n, aw), BF16),
                 sds((n, aw), BF16), sds((n, d), BF16), sds((n, d), BF16), sds((n, cw), F32)]
    outs = pl.pallas_call(
        functools.partial(_input_stage_sample_kernel, seq=seq),
        grid=(1,),
        in_specs=[full(x), full(pe0), full(pe1)] + [full(c) for c in consts],
        out_specs=[full(o) for o in out_shape],
        out_shape=out_shape,
        compiler_params=pltpu.CompilerParams(dimension_semantics=("arbitrary",),
                                             vmem_limit_bytes=VMEM_LIMIT),
        name="input_stage_sample",
    )(x, pe0, pe1, *consts)
    conv_new = outs[7].reshape(n // seq, seq, cw)[:, seq - 2:, :]
    return outs[:7], conv_new


def _stick_block(z, carry, tri_ref, keep):
    sp = jnp.maximum(z, 0.0) + jnp.log2(1.0 + jnp.exp2(-jnp.abs(z)))
    if keep is not None:
        sp = jnp.where(keep, sp, 0.0)
    hi = sp.astype(BF16)
    lo = (sp - hi.astype(F32)).astype(BF16)
    incl = _dot(jnp.concatenate([hi, lo], axis=1), tri_ref[...])
    a = jnp.exp2(z - incl - carry)
    if keep is not None:
        a = jnp.where(keep, a, 0.0)
    return a.astype(BF16), carry + jnp.sum(sp, axis=-1, keepdims=True)


def _attn_prompt_kernel(q_ref, kt_ref, vt_ref, tri_ref, o_ref, acc_ref):
    tq = q_ref.shape[0]
    qi = pl.program_id(2)
    heads = range(LANES // HEAD_DIM)
    lane_head = lax.broadcasted_iota(jnp.int32, (1, LANES), 1) // HEAD_DIM
    row_head = lax.broadcasted_iota(jnp.int32, (LANES, tq), 0) // HEAD_DIM
    row = lax.broadcasted_iota(jnp.int32, (tq, tq), 0)
    col = lax.broadcasted_iota(jnp.int32, (tq, tq), 1)
    qm = [q_ref[...] * (lane_head == h).astype(BF16) for h in heads]

    def step(blocks, carries, keep):
        carries = list(carries)
        a_parts, vt_parts = [], []
        for j in blocks:
            start = pl.multiple_of(j * tq, tq)
            kt = kt_ref[0, :, pl.ds(start, tq)]
            vt = vt_ref[0, :, pl.ds(start, tq)]
            for h in heads:
                a, carries[h] = _stick_block(_dot(qm[h], kt), carries[h], tri_ref, keep)
                a_parts.append(a)
                vt_parts.append(jnp.where(row_head == h, vt, jnp.zeros_like(vt)))
        acc_ref[...] += _dot_nt(jnp.concatenate(a_parts, axis=1),
                                jnp.concatenate(vt_parts, axis=1))
        return tuple(carries)

    acc_ref[...] = jnp.zeros_like(acc_ref)
    carries = step([qi], [jnp.zeros((tq, 1), F32) for _ in heads], col < row)
    carries = lax.fori_loop(
        0, qi // 2, lambda p, c: step([qi - 1 - 2 * p, qi - 2 - 2 * p], c, None), carries)

    @pl.when(qi % 2 == 1)
    def _():
        step([0], carries, None)

    o_ref[...] = acc_ref[...].astype(o_ref.dtype)


def _attn_prompt(q, ktb, vtb, tri):
    n, aw = q.shape
    nb, _, seq = ktb.shape
    tq = ATTN_BLOCK
    nq = seq // tq
    qspec = pl.BlockSpec((tq, LANES), lambda b, hp, i: (b * nq + i, hp))
    kvspec = pl.BlockSpec((1, LANES, seq), lambda b, hp, i: (b, hp, 0))
    return pl.pallas_call(
        _attn_prompt_kernel,
        grid=(nb, aw // LANES, nq),
        in_specs=[qspec, kvspec, kvspec, pl.BlockSpec(tri.shape, lambda b, hp, i: (0, 0))],
        out_specs=qspec,
        out_shape=jax.ShapeDtypeStruct((n, aw), BF16),
        scratch_shapes=[pltpu.VMEM((tq, LANES), F32)],
        compiler_params=pltpu.CompilerParams(
            dimension_semantics=("arbitrary", "arbitrary", "arbitrary"),
            vmem_limit_bytes=VMEM_LIMIT),
        name="attn_prompt",
    )(q, ktb, vtb, tri)


def _attn_sample_kernel(q_ref, kn_ref, vn_ref, ck_ref, cv_ref, tri_ref, o_ref,
                        qrow_ref, carry_ref, acc_ref):
    tq, aw = q_ref.shape[1], q_ref.shape[2]
    rows = N_HEADS * tq
    tk = ATTN_BLOCK
    c = pl.program_id(1)

    @pl.when(c == 0)
    def _():
        row_head = lax.broadcasted_iota(jnp.int32, (rows, aw), 0) // tq
        lane_head = lax.broadcasted_iota(jnp.int32, (rows, aw), 1) // HEAD_DIM
        qt = jnp.concatenate([q_ref[0]] * N_HEADS, axis=0)
        qrow = jnp.where(row_head == lane_head, qt, jnp.zeros_like(qt))
        qrow_ref[...] = qrow
        pad = jnp.zeros((tk - tq, aw), BF16)
        kn = jnp.concatenate([kn_ref[0], pad], axis=0)
        vn = jnp.concatenate([vn_ref[0], pad], axis=0)
        qpos = lax.broadcasted_iota(jnp.int32, (rows, tk), 0) % tq
        kpos = lax.broadcasted_iota(jnp.int32, (rows, tk), 1)
        a, carry = _stick_block(_dot_nt(qrow, kn), jnp.zeros((rows, 1), F32), tri_ref,
                                kpos < qpos)
        carry_ref[...] = carry
        acc_ref[...] = _dot(a, vn)

    qrow = qrow_ref[...]
    nsub = ck_ref.shape[2] // tk
    for s in reversed(range(nsub)):
        kt = ck_ref[0, :, s * tk:(s + 1) * tk].astype(BF16)
        vt = cv_ref[0, :, s * tk:(s + 1) * tk].astype(BF16)
        a, carry = _stick_block(_dot(qrow, kt), carry_ref[...], tri_ref, None)
        carry_ref[...] = carry
        acc_ref[...] += _dot_nt(a, vt)

    @pl.when(c == pl.num_programs(1) - 1)
    def _():
        lane_head = lax.broadcasted_iota(jnp.int32, (tq, aw), 1) // HEAD_DIM
        out = jnp.zeros((tq, aw), F32)
        for head in range(N_HEADS):
            out = out + jnp.where(lane_head == head, acc_ref[head * tq:(head + 1) * tq, :], 0.0)
        o_ref[0] = out.astype(o_ref.dtype)


def _attn_sample(q, kn, vn, cache_kt, cache_vt, tri, *, chunk):
    nb, tq, aw = q.shape
    past = cache_kt.shape[2]
    nchunk = past // chunk
    new = pl.BlockSpec((1, tq, aw), lambda b, c: (b, 0, 0))
    cache = pl.BlockSpec((1, aw, chunk), lambda b, c: (b, 0, nchunk - 1 - c))
    rows = N_HEADS * tq
    return pl.pallas_call(
        _attn_sample_kernel,
        grid=(nb, nchunk),
        in_specs=[new, new, new, cache, cache, pl.BlockSpec(tri.shape, lambda b, c: (0, 0))],
        out_specs=new,
        out_shape=jax.ShapeDtypeStruct((nb, tq, aw), BF16),
        scratch_shapes=[pltpu.VMEM((rows, aw), BF16), pltpu.VMEM((rows, 1), F32),
                        pltpu.VMEM((rows, aw), F32)],
        compiler_params=pltpu.CompilerParams(dimension_semantics=("arbitrary", "arbitrary"),
                                             vmem_limit_bytes=VMEM_LIMIT),
        name="attn_sample",
    )(q, kn, vn, cache_kt, cache_vt, tri)


def _first_max(vals, lane):
    m = jnp.max(vals, axis=-1, keepdims=True)
    idx = jnp.min(jnp.where(vals == m, lane, float(LANES)), axis=-1, keepdims=True)
    return m, idx


def _router_gates(h, wrg_ref, brg_ref, wre_ref, bre_ref):
    lane = lax.broadcasted_iota(jnp.int32, (1, LANES), 1).astype(F32)
    neg = -jnp.inf
    lg = _dot(h, wrg_ref[...]) + brg_ref[...]
    lg = jnp.where(lane < N_GROUPS, lg, neg)
    mg, gidx = _first_max(lg, lane)
    p_sel = 1.0 / jnp.sum(jnp.exp(lg - mg), axis=-1, keepdims=True)
    le = _dot(h, wre_ref[...]) + bre_ref[...]
    in_group = jnp.floor(lane * (1.0 / EXPERTS_PER_GROUP)) == gidx
    le = jnp.where(in_group, le, neg)
    t1, i1 = _first_max(le, lane)
    t2, i2 = _first_max(jnp.where(lane == i1, neg, le), lane)
    e2 = jnp.exp(t2 - t1)
    w1 = 1.0 / (1.0 + e2)
    gate = p_sel * (jnp.where(lane == i1, w1, 0.0) + jnp.where(lane == i2, e2 * w1, 0.0))
    return gate, gidx


def _merge_route_kernel(x_ref, attn_ref, sa_ref, mb_ref, wab_ref, wout_ref, lnffn_ref,
                        wrg_ref, brg_ref, wre_ref, bre_ref, ltri_ref,
                        x1_ref, pay_ref, key_ref, cnt_out_ref, cnt_ref):
    t, d = x_ref.shape

    @pl.when(pl.program_id(0) == 0)
    def _():
        cnt_ref[...] = jnp.zeros_like(cnt_ref)

    ya = _dot(attn_ref[...], wab_ref[...])
    m = sa_ref[...].astype(F32) * ya + mb_ref[...].astype(F32)
    x1 = x_ref[...] + _dot(m.astype(BF16), wout_ref[...])
    x1_ref[...] = x1
    h = x1 * _rms_scale(x1) * lnffn_ref[...]
    gate, gidx = _router_gates(h.astype(BF16), wrg_ref, brg_ref, wre_ref, bre_ref)
    pay_ref[:, 0:d] = h
    pay_ref[:, d:d + LANES] = gate

    lane = lax.broadcasted_iota(jnp.int32, (1, LANES), 1).astype(F32)
    onehot = lane == gidx
    before = _dot(ltri_ref[...], onehot.astype(BF16)) + cnt_ref[...]
    rank = jnp.sum(jnp.where(onehot, before, 0.0), axis=-1, keepdims=True)
    cnt = cnt_ref[...] + jnp.sum(onehot.astype(F32), axis=0, keepdims=True)
    cnt_ref[...] = cnt
    cnt_out_ref[...] = cnt.astype(jnp.int32)
    key = gidx * float(1 << GROUP_SHIFT) + rank
    key_ref[0] = jnp.broadcast_to(key, (t, LANES)).T[0:1, :].astype(jnp.int32)


def _merge_route(x, attn, sa, mb, wts, *, tile):
    n, d = x.shape
    assert n < (1 << GROUP_SHIFT) and N_GROUPS << GROUP_SHIFT <= 1 << 24
    row = lambda c: pl.BlockSpec((tile, c), lambda i: (i, 0))
    ltri = (jnp.arange(tile)[:, None] > jnp.arange(tile)[None, :]).astype(BF16)
    consts = [wts['w_attn_branch'], wts['w_out'], wts['ln_ffn'], wts['w_router_group'],
              wts['b_router_group'], wts['w_router_expert'], wts['b_router_expert'], ltri]
    sds = jax.ShapeDtypeStruct
    x1, pay, keys, cnt = pl.pallas_call(
        _merge_route_kernel,
        grid=(n // tile,),
        in_specs=[row(d), row(attn.shape[1]), row(d), row(d)]
        + [_const_spec(c.shape) for c in consts],
        out_specs=[row(d), row(d + LANES), pl.BlockSpec((1, 1, tile), lambda i: (i, 0, 0)),
                   pl.BlockSpec((1, LANES), lambda i: (0, 0))],
        out_shape=[sds((n, d), F32), sds((n, d + LANES), F32),
                   sds((n // tile, 1, tile), jnp.int32), sds((1, LANES), jnp.int32)],
        scratch_shapes=[pltpu.VMEM((1, LANES), F32)],
        compiler_params=pltpu.CompilerParams(dimension_semantics=("arbitrary",),
                                             vmem_limit_bytes=VMEM_LIMIT),
        name="merge_route",
    )(x, attn, sa, mb, *consts)
    return x1, pay, keys, cnt[0, :N_GROUPS]


def _row_positions(key_row, off_ref):
    grp = key_row >> GROUP_SHIFT
    pos = key_row & ((1 << GROUP_SHIFT) - 1)
    for g in range(N_GROUPS):
        pos = pos + jnp.where(grp == g, off_ref[g], 0)
    return pos


def _to_smem(src_ref, dst_ref, sem):
    cp = pltpu.make_async_copy(src_ref, dst_ref, sem)
    cp.start()
    cp.wait()


def _dispatch_kernel(off_ref, cnt_ref, key_ref, pay_ref, xs_ref, stage_ref, posv_ref, pos_smem,
                     zero_ref, sem_s, sem_in, sem_row, sem_pad, *, tile_m):
    i = pl.program_id(0)
    n = pl.num_programs(0)
    t = key_ref.shape[2]
    ring = stage_ref.shape[0]
    slot = lax.rem(i, ring)
    nxt = lax.rem(i + 1, ring)

    def load(tile, s):
        return pltpu.make_async_copy(pay_ref.at[pl.ds(tile * t, t), :], stage_ref.at[s],
                                     sem_in.at[s])

    def wait_rows(s):
        pltpu.make_async_copy(stage_ref.at[s], xs_ref.at[pl.ds(0, t), :], sem_row.at[s]).wait()

    @pl.when(i == 0)
    def _():
        load(0, 0).start()

    @pl.when(i >= ring - 1)
    def _():
        wait_rows(nxt)

    @pl.when(i + 1 < n)
    def _():
        load(i + 1, nxt).start()

    posv_ref[...] = _row_positions(key_ref[0], off_ref)
    _to_smem(posv_ref, pos_smem, sem_s)
    load(i, slot).wait()

    def copy_row(r, _):
        pltpu.make_async_copy(stage_ref.at[slot, pl.ds(r, 1), :],
                              xs_ref.at[pl.ds(pos_smem[0, r], 1), :], sem_row.at[slot]).start()
        return 0

    lax.fori_loop(0, t, copy_row, 0, unroll=True)

    @pl.when(i == n - 1)
    def _():
        for back in range(ring - 1):
            @pl.when(i >= back)
            def _():
                wait_rows(lax.rem(i - back + ring, ring))
        zero_ref[...] = jnp.zeros_like(zero_ref)
        pads = []

        def pad(cond, start, size):
            pads.append((cond, pltpu.make_async_copy(
                zero_ref.at[pl.ds(0, size), :], xs_ref.at[pl.ds(start, size), :], sem_pad)))

        used = 0
        for g in range(N_GROUPS):
            c = cnt_ref[g]
            c8 = (c + 7) >> 3 << 3
            end = (c + tile_m - 1) // tile_m * tile_m
            for k in range(7):
                pad(c + k < c8, off_ref[g] + c + k, 1)
            start, size = off_ref[g] + c8, 8
            while size < tile_m:
                cond = ((end - c8) & size) != 0
                pad(cond, pl.multiple_of(start, 8), size)
                start = start + jnp.where(cond, size, 0)
                size *= 2
            used = off_ref[g] + end
        for k in range(N_GROUPS):
            start = used + k * tile_m
            pad(start < xs_ref.shape[0], pl.multiple_of(start, tile_m), tile_m)
        for cond, cp in pads:
            pl.when(cond)(cp.start)
        for cond, cp in pads:
            pl.when(cond)(cp.wait)


def _dispatch(pay, keys, off, counts, *, tile_m):
    n, w = pay.shape
    nt, _, tile = keys.shape
    rows = n + N_GROUPS * tile_m
    return pl.pallas_call(
        functools.partial(_dispatch_kernel, tile_m=tile_m),
        grid_spec=pltpu.PrefetchScalarGridSpec(
            num_scalar_prefetch=2,
            grid=(nt,),
            in_specs=[pl.BlockSpec((1, 1, tile), lambda i, off, cnt: (i, 0, 0)),
                      pl.BlockSpec(memory_space=pl.ANY)],
            out_specs=pl.BlockSpec(memory_space=pl.ANY),
            scratch_shapes=[pltpu.VMEM((DISPATCH_RING, tile, w), F32),
                            pltpu.VMEM((1, tile), jnp.int32), pltpu.SMEM((1, tile), jnp.int32),
                            pltpu.VMEM((tile_m, w), F32), pltpu.SemaphoreType.DMA,
                            pltpu.SemaphoreType.DMA((DISPATCH_RING,)),
                            pltpu.SemaphoreType.DMA((DISPATCH_RING,)),
                            pltpu.SemaphoreType.DMA]),
        out_shape=jax.ShapeDtypeStruct((rows, w), F32),
        compiler_params=pltpu.CompilerParams(dimension_semantics=("arbitrary",),
                                             vmem_limit_bytes=VMEM_LIMIT),
        name="dispatch",
    )(off, counts, keys, pay)


def _group_experts_kernel(grp_ref, xs_ref, weg_ref, weu_ref, wed_ref, y_ref):
    d = y_ref.shape[1]
    grp = grp_ref[pl.program_id(0)]
    h = xs_ref[:, 0:d].astype(BF16)
    gate = xs_ref[:, d:d + LANES]
    lane = lax.broadcasted_iota(jnp.int32, (1, LANES), 1)
    acts = []
    for j in range(EXPERTS_PER_GROUP):
        g = jnp.sum(jnp.where(lane == grp * EXPERTS_PER_GROUP + j, gate, 0.0),
                    axis=-1, keepdims=True)
        act = jax.nn.silu(_dot(h, weg_ref[0, j])) * _dot(h, weu_ref[0, j])
        acts.append((g * act).astype(BF16))
    y_ref[...] = _dot(jnp.concatenate(acts, axis=1), wed_ref[0])


def _group_experts(xs, tile_grp, wts, *, tile_m):
    d = xs.shape[1] - LANES
    n_exp, _, d_exp = wts['w_exp_gate'].shape
    epg = EXPERTS_PER_GROUP
    weg = wts['w_exp_gate'].reshape(N_GROUPS, epg, d, d_exp)
    weu = wts['w_exp_up'].reshape(N_GROUPS, epg, d, d_exp)
    wed = wts['w_exp_down'].reshape(N_GROUPS, epg * d_exp, d)
    return pl.pallas_call(
        _group_experts_kernel,
        grid_spec=pltpu.PrefetchScalarGridSpec(
            num_scalar_prefetch=1,
            grid=(xs.shape[0] // tile_m,),
            in_specs=[pl.BlockSpec((tile_m, d + LANES), lambda i, gr: (i, 0)),
                      pl.BlockSpec((1, epg, d, d_exp), lambda i, gr: (gr[i], 0, 0, 0)),
                      pl.BlockSpec((1, epg, d, d_exp), lambda i, gr: (gr[i], 0, 0, 0)),
                      pl.BlockSpec((1, epg * d_exp, d), lambda i, gr: (gr[i], 0, 0))],
            out_specs=pl.BlockSpec((tile_m, d), lambda i, gr: (i, 0))),
        out_shape=jax.ShapeDtypeStruct((xs.shape[0], d), F32),
        compiler_params=pltpu.CompilerParams(dimension_semantics=("arbitrary",),
                                             vmem_limit_bytes=VMEM_LIMIT),
        name="group_experts",
    )(tile_grp, xs, weg, weu, wed)


def _combine_ple_kernel(off_ref, x1_ref, p_ref, keyc_ref, keyn_ref, lnple_ref, wpg_ref, wpp_ref,
                        y_ref, o_ref, ybuf_ref, posv_ref, pos_smem, sem_y, sem_s):
    i = pl.program_id(0)
    n = pl.num_programs(0)
    t = x1_ref.shape[0]
    slot = i % 2

    def fetch(key_ref, s):
        posv_ref[...] = _row_positions(key_ref[0], off_ref)
        _to_smem(posv_ref, pos_smem.at[pl.ds(s, 1), :], sem_s)

        def gather_row(r, _):
            pltpu.make_async_copy(y_ref.at[pl.ds(pos_smem[s, r], 1), :],
                                  ybuf_ref.at[s, pl.ds(r, 1), :], sem_y.at[s]).start()
            return 0

        lax.fori_loop(0, t, gather_row, 0, unroll=True)

    @pl.when(i == 0)
    def _():
        fetch(keyc_ref, 0)

    @pl.when(i + 1 < n)
    def _():
        fetch(keyn_ref, 1 - slot)

    pltpu.make_async_copy(y_ref.at[pl.ds(0, t), :], ybuf_ref.at[slot], sem_y.at[slot]).wait()
    x2 = x1_ref[...] + ybuf_ref[slot]
    hp = (x2 * _rms_scale(x2) * lnple_ref[...]).astype(BF16)
    gate = jax.nn.sigmoid(_dot(hp, wpg_ref[...]))
    o_ref[...] = x2 + gate * _dot(p_ref[...].astype(BF16), wpp_ref[...])


def _combine_ple(x1, p, keys, off, y, wts):
    n, d = x1.shape
    nt, _, tile = keys.shape
    row = lambda c: pl.BlockSpec((tile, c), lambda i, off: (i, 0))
    consts = [wts['ln_ple'], wts['w_ple_gate'], wts['w_ple_proj']]
    return pl.pallas_call(
        _combine_ple_kernel,
        grid_spec=pltpu.PrefetchScalarGridSpec(
            num_scalar_prefetch=1,
            grid=(nt,),
            in_specs=[row(d), row(p.shape[1]),
                      pl.BlockSpec((1, 1, tile), lambda i, off: (i, 0, 0)),
                      pl.BlockSpec((1, 1, tile),
                                   lambda i, off: (jnp.minimum(i + 1, nt - 1), 0, 0))]
            + [_const_spec(c.shape) for c in consts] + [pl.BlockSpec(memory_space=pl.ANY)],
            out_specs=row(d),
            scratch_shapes=[pltpu.VMEM((2, tile, d), F32), pltpu.VMEM((1, tile), jnp.int32),
                            pltpu.SMEM((2, tile), jnp.int32), pltpu.SemaphoreType.DMA((2,)),
                            pltpu.SemaphoreType.DMA]),
        out_shape=jax.ShapeDtypeStruct((n, d), F32),
        compiler_params=pltpu.CompilerParams(dimension_semantics=("arbitrary",),
                                             vmem_limit_bytes=VMEM_LIMIT),
        name="combine_ple",
    )(off, x1, p, keys, keys, *consts, y)


def _channel_stage(x, attn, sa, mb, p, wts, *, tile, tile_m):
    n = x.shape[0]
    assert n % tile == 0 and n % tile_m == 0
    x1, pay, keys, counts = _merge_route(x, attn, sa, mb, wts, tile=tile)
    ends = jnp.cumsum((counts + tile_m - 1) // tile_m * tile_m)
    off = jnp.concatenate([jnp.zeros((1,), jnp.int32), ends[:-1]])
    xs = _dispatch(pay, keys, off, counts, tile_m=tile_m)
    tile_start = jnp.arange(xs.shape[0] // tile_m, dtype=jnp.int32) * tile_m
    tile_grp = jnp.minimum(jnp.sum(tile_start[:, None] >= ends[None, :], axis=1), N_GROUPS - 1)
    y = _group_experts(xs, tile_grp.astype(jnp.int32), wts, tile_m=tile_m)
    return _combine_ple(x1, p, keys, off, y, wts)


def _layer_weights(i, ln_mix, w_in, q_norm, k_norm, conv_w, w_attn_branch, w_conv_branch, w_out,
                   ln_ffn, w_router_group, b_router_group, w_router_expert, b_router_expert,
                   w_exp_gate, w_exp_up, w_exp_down, ln_ple, w_ple_gate, w_ple_proj):
    aw = ATTN_WIDTH
    pad_cols = lambda a: jnp.pad(a, ((0, 0), (0, LANES - a.shape[1])))
    head_id = jnp.arange(aw) // HEAD_DIM
    k_gain = jnp.tile(k_norm[i], N_HEADS)
    return dict(
        ln_mix=ln_mix[i][None, :], w_in=w_in[i].astype(BF16),
        w_kv_t=w_in[i][:, aw:3 * aw].T.astype(BF16),
        q_norm=jnp.tile(q_norm[i], N_HEADS)[None, :], k_norm=k_gain[None, :],
        k_norm_t=jnp.broadcast_to(k_gain[:, None], (aw, LANES)),
        head_sum=(head_id[:, None] == head_id[None, :]).astype(BF16),
        conv_w=conv_w[i], w_conv_branch=w_conv_branch[i].astype(BF16),
        w_attn_branch=w_attn_branch[i].astype(BF16), w_out=w_out[i].astype(BF16),
        ln_ffn=ln_ffn[i][None, :],
        w_router_group=pad_cols(w_router_group[i]).astype(BF16),
        b_router_group=pad_cols(b_router_group[i][None, :]),
        w_router_expert=pad_cols(w_router_expert[i]).astype(BF16),
        b_router_expert=pad_cols(b_router_expert[i][None, :]),
        w_exp_gate=w_exp_gate[i].astype(BF16), w_exp_up=w_exp_up[i].astype(BF16),
        w_exp_down=w_exp_down[i].astype(BF16),
        ln_ple=ln_ple[i][None, :], w_ple_gate=w_ple_gate[i].astype(BF16),
        w_ple_proj=w_ple_proj[i].astype(BF16))


def _feature_major(a):
    b, s, h, dh = a.shape
    return jnp.transpose(a, (0, 2, 3, 1)).reshape(b, h * dh, s)


def _position_major(a):
    b, _, s = a.shape
    return jnp.transpose(a.reshape(b, N_HEADS, HEAD_DIM, s), (0, 3, 1, 2))


def kernel(x_prompt, x_sample, cache_k, cache_v, state_conv, p_prompt, p_sample, ln_mix, w_in, q_norm, k_norm, conv_w, w_attn_branch, w_conv_branch, w_out, ln_ffn, w_router_group, b_router_group, w_router_expert, b_router_expert, w_exp_gate, w_exp_up, w_exp_down, ln_ple, w_ple_gate, w_ple_proj):
    depth = ln_mix.shape[0]
    nb, seq, d = x_prompt.shape
    nbs, seqs, _ = x_sample.shape
    aw = ATTN_WIDTH
    assert cache_k.shape[3:] == (N_HEADS, HEAD_DIM)
    assert w_router_group.shape[2] == N_GROUPS
    assert w_router_expert.shape[2] == N_GROUPS * EXPERTS_PER_GROUP
    tile = 512
    tri = (jnp.arange(ATTN_BLOCK)[:, None] >= jnp.arange(ATTN_BLOCK)[None, :]).astype(BF16)
    tri = jnp.concatenate([tri, tri], axis=0)

    xp = x_prompt.reshape(nb * seq, d)
    xs = x_sample.reshape(nbs * seqs, d)
    outs = [[] for _ in range(6)]
    for i in range(depth):
        wts = _layer_weights(i, ln_mix, w_in, q_norm, k_norm, conv_w, w_attn_branch,
                             w_conv_branch, w_out, ln_ffn, w_router_group, b_router_group,
                             w_router_expert, b_router_expert, w_exp_gate, w_exp_up, w_exp_down,
                             ln_ple, w_ple_gate, w_ple_proj)

        (q, kt, vt, ktb, vtb, sa, mb), conv_new = _input_stage_prompt(xp, wts, seq=seq, tile=tile)
        attn = _attn_prompt(q, ktb, vtb, tri)
        xp = _channel_stage(xp, attn, sa, mb, p_prompt[i].reshape(nb * seq, -1), wts, tile=tile,
                            tile_m=512)
        outs[0].append(_position_major(kt))
        outs[1].append(_position_major(vt))
        outs[2].append(conv_new)

        (q, k, v, kb, vb, sa, mb), conv_new = _input_stage_sample(xs, state_conv[i], wts, seq=seqs)
        attn = _attn_sample(q.reshape(nbs, seqs, aw), kb.reshape(nbs, seqs, aw),
                            vb.reshape(nbs, seqs, aw), _feature_major(cache_k[i]),
                            _feature_major(cache_v[i]), tri, chunk=1024)
        xs = _channel_stage(xs, attn.reshape(nbs * seqs, aw), sa, mb,
                            p_sample[i].reshape(nbs * seqs, -1), wts, tile=nbs * seqs,
                            tile_m=min(256, nbs * seqs))
        outs[3].append(k.reshape(nbs, seqs, N_HEADS, HEAD_DIM))
        outs[4].append(v.reshape(nbs, seqs, N_HEADS, HEAD_DIM))
        outs[5].append(conv_new)

    kp, vp, cp, ks, vs, cs = [jnp.stack(o) for o in outs]
    return (xp.reshape(nb, seq, d), xs.reshape(nbs, seqs, d), kp, vp, cp, ks, vs, cs)
```

```python
import functools

import jax
import jax.numpy as jnp
from jax import lax
from jax.experimental import pallas as pl
from jax.experimental.pallas import tpu as pltpu

F32 = jnp.float32
BF16 = jnp.bfloat16

EPS = 1e-6
N_HEADS = 8
HEAD_DIM = 64
ATTN_WIDTH = N_HEADS * HEAD_DIM
N_GROUPS = 4
EXPERTS_PER_GROUP = 4
GROUP_SHIFT = 20
DISPATCH_RING = 3
LANES = 128
ATTN_BLOCK = 256
MASKED = -1e30
SP_CLAMP = 64.0
ATTN_PIPE = 4
LOG2E_HI = 1.4426950216293335
LOG2E_LO = 1.925963033500011e-08
VMEM_LIMIT = 56 * 1024 * 1024


def _dot(a, b):
    return jnp.dot(a, b, preferred_element_type=F32)


def _dot_nt(a, b):
    return lax.dot_general(a, b, (((1,), (1,)), ((), ())), preferred_element_type=F32)


def _rms_scale(x):
    return lax.rsqrt(jnp.mean(x * x, axis=-1, keepdims=True) + EPS)


def _const_spec(shape):
    nd = len(shape)
    return pl.BlockSpec(shape, lambda *_: (0,) * nd, pipeline_mode=pl.Buffered(1))


def _head_norm(z, gain, hsum_ref):
    ss = _dot((z * z).astype(BF16), hsum_ref[...])
    return z * lax.rsqrt(ss * (1.0 / HEAD_DIM) + EPS) * gain


def _input_stage_body(x, past1, past2, lnmix_ref, win_ref, qn_ref, hsum_ref, convw_ref, wcb_ref,
                      q_ref, sa_ref, mb_ref):
    aw = ATTN_WIDTH
    cw = convw_ref.shape[1]
    d = x.shape[1]
    h = (x * _rms_scale(x) * lnmix_ref[...]).astype(BF16)

    def proj(lo, width):
        return _dot(h, win_ref[:, lo:lo + width])

    q = _head_norm(proj(0, aw), qn_ref[...], hsum_ref)
    q = q * (HEAD_DIM ** -0.5)
    q_ref[...] = (q * LOG2E_HI + q * LOG2E_LO).astype(BF16)

    cb = proj(3 * aw, cw)
    u = proj(3 * aw + cw, cw) * proj(3 * aw + 2 * cw, cw)
    u1, u2 = past1(u), past2(u)
    conv_y = convw_ref[0:1, :] * u2 + convw_ref[1:2, :] * u1 + convw_ref[2:3, :] * u
    yb = _dot((cb * conv_y).astype(BF16), wcb_ref[...])
    ga = proj(3 * aw + 3 * cw, d)
    gb = proj(3 * aw + 3 * cw + d, d)
    sa_ref[...] = jax.nn.sigmoid(ga).astype(BF16)
    mb_ref[...] = (jax.nn.sigmoid(gb) * yb).astype(BF16)
    return h, proj, u


def _input_stage_prompt_kernel(x_ref, lnmix_ref, win_ref, wkvt_ref, qn_ref, knt_ref, hsum_ref,
                               convw_ref, wcb_ref, q_ref, kt_ref, vt_ref, ktb_ref, vtb_ref,
                               sa_ref, mb_ref, cnew_ref, tail_ref, *, tiles_per_seq):
    t = x_ref.shape[0]
    aw = ATTN_WIDTH

    @pl.when(pl.program_id(0) % tiles_per_seq == 0)
    def _():
        tail_ref[...] = jnp.zeros_like(tail_ref)

    def shifted(u, n):
        ext = jnp.concatenate([tail_ref[...], u], axis=0)
        return pltpu.roll(ext, n, axis=0)[8:, :]

    h, _, u = _input_stage_body(x_ref[...], lambda u: shifted(u, 1), lambda u: shifted(u, 2),
                                lnmix_ref, win_ref, qn_ref, hsum_ref, convw_ref, wcb_ref,
                                q_ref, sa_ref, mb_ref)
    tail_ref[...] = u[t - 8:, :]
    cnew_ref[0] = u[t - 8:, :]

    kt = _dot_nt(wkvt_ref[0:aw, :], h)
    k3 = kt.reshape(N_HEADS, HEAD_DIM, t)
    scale = lax.rsqrt(jnp.mean(k3 * k3, axis=1, keepdims=True) + EPS)
    kt = (k3 * scale).reshape(aw, t) * jnp.tile(knt_ref[...], (1, t // LANES))
    kt_ref[0] = kt
    ktb_ref[0] = kt.astype(BF16)
    vt = _dot_nt(wkvt_ref[aw:2 * aw, :], h)
    vt_ref[0] = vt
    vtb_ref[0] = vt.astype(BF16)


def _input_stage_sample_kernel(x_ref, pe0_ref, pe1_ref, lnmix_ref, win_ref, qn_ref, kn_ref,
                               hsum_ref, convw_ref, wcb_ref, q_ref, k_ref, v_ref, kb_ref,
                               vb_ref, sa_ref, mb_ref, u_ref, *, seq):
    t = x_ref.shape[0]
    aw = ATTN_WIDTH
    pos = lax.broadcasted_iota(jnp.int32, (t, 1), 0) % seq

    def past1(u):
        return jnp.where(pos == 0, pe1_ref[...], pltpu.roll(u, 1, axis=0))

    def past2(u):
        return jnp.where(pos == 0, pe0_ref[...],
                         jnp.where(pos == 1, pe1_ref[...], pltpu.roll(u, 2, axis=0)))

    _, proj, u = _input_stage_body(x_ref[...], past1, past2, lnmix_ref, win_ref, qn_ref,
                                   hsum_ref, convw_ref, wcb_ref, q_ref, sa_ref, mb_ref)
    u_ref[...] = u
    k = _head_norm(proj(aw, aw), kn_ref[...], hsum_ref)
    k_ref[...] = k
    kb_ref[...] = k.astype(BF16)
    v = proj(2 * aw, aw)
    v_ref[...] = v
    vb_ref[...] = v.astype(BF16)


def _input_stage_prompt(x, wts, *, seq, tile):
    n, d = x.shape
    nb = n // seq
    aw = ATTN_WIDTH
    cw = wts['conv_w'].shape[1]
    tiles_per_seq = seq // tile
    row = lambda w: pl.BlockSpec((tile, w), lambda i: (i, 0))
    feat = pl.BlockSpec((1, aw, tile), lambda i: (i // tiles_per_seq, 0, i % tiles_per_seq))
    consts = [wts['ln_mix'], wts['w_in'], wts['w_kv_t'], wts['q_norm'], wts['k_norm_t'],
              wts['head_sum'], wts['conv_w'], wts['w_conv_branch']]
    sds = jax.ShapeDtypeStruct
    outs = pl.pallas_call(
        functools.partial(_input_stage_prompt_kernel, tiles_per_seq=tiles_per_seq),
        grid=(n // tile,),
        in_specs=[row(d)] + [_const_spec(c.shape) for c in consts],
        out_specs=[row(aw), feat, feat, feat, feat, row(d), row(d),
                   pl.BlockSpec((1, 8, cw), lambda i: (i // tiles_per_seq, 0, 0))],
        out_shape=[sds((n, aw), BF16), sds((nb, aw, seq), F32), sds((nb, aw, seq), F32),
                   sds((nb, aw, seq), BF16), sds((nb, aw, seq), BF16), sds((n, d), BF16),
                   sds((n, d), BF16), sds((nb, 8, cw), F32)],
        scratch_shapes=[pltpu.VMEM((8, cw), F32)],
        compiler_params=pltpu.CompilerParams(dimension_semantics=("arbitrary",),
                                             vmem_limit_bytes=VMEM_LIMIT),
        name="input_stage_prompt",
    )(x, *consts)
    return outs[:7], outs[7][:, 6:, :]


def _input_stage_sample(x, state, wts, *, seq):
    n, d = x.shape
    aw = ATTN_WIDTH
    cw = wts['conv_w'].shape[1]
    pe0 = jnp.repeat(state[:, 0, :], seq, axis=0)
    pe1 = jnp.repeat(state[:, 1, :], seq, axis=0)
    consts = [wts['ln_mix'], wts['w_in'], wts['q_norm'], wts['k_norm'], wts['head_sum'],
              wts['conv_w'], wts['w_conv_branch']]
    full = lambda a: pl.BlockSpec(a.shape, lambda i: (0,) * a.ndim)
    sds = jax.ShapeDtypeStruct
    out_shape = [sds((n, aw), BF16), sds((n, aw), F32), sds((n, aw), F32), sds((n, aw), BF16),
                 sds((n, aw), BF16), sds((n, d), BF16), sds((n, d), BF16), sds((n, cw), F32)]
    outs = pl.pallas_call(
        functools.partial(_input_stage_sample_kernel, seq=seq),
        grid=(1,),
        in_specs=[full(x), full(pe0), full(pe1)] + [full(c) for c in consts],
        out_specs=[full(o) for o in out_shape],
        out_shape=out_shape,
        compiler_params=pltpu.CompilerParams(dimension_semantics=("arbitrary",),
                                             vmem_limit_bytes=VMEM_LIMIT),
        name="input_stage_sample",
    )(x, pe0, pe1, *consts)
    conv_new = outs[7].reshape(n // seq, seq, cw)[:, seq - 2:, :]
    return outs[:7], conv_new


def _softplus2(z, keep=None):
    sp = jnp.maximum(z, jnp.log2(1.0 + jnp.exp2(jnp.minimum(z, SP_CLAMP))))
    if keep is not None:
        sp = jnp.where(keep, sp, 0.0)
    return sp.astype(BF16), jnp.sum(sp, axis=-1, keepdims=True)


def _stick_weights(z, sp, carry, tri_ref, keep=None):
    a = jnp.exp2(z - _dot(sp, tri_ref[...]) - carry)
    if keep is not None:
        a = jnp.where(keep, a, 0.0)
    return a.astype(BF16)


def _stick_block(z, carry, tri_ref, keep):
    sp, row_sum = _softplus2(z, keep)
    return _stick_weights(z, sp, carry, tri_ref, keep), carry + row_sum


def _attn_items(nq):
    items = [(qi, j) for qi in range(nq) for j in range(qi, -1, -1)]
    items = [(0, 0)] * (-len(items) % ATTN_PIPE) + items
    return (jnp.array([i[0] for i in items], jnp.int32),
            jnp.array([i[1] for i in items], jnp.int32))


def _attn_prompt_kernel(iq_ref, ik_ref, q_ref, kt_ref, vt_ref, tri_ref, mask_ref, o_ref,
                        z_ref, sp_ref, rs_ref, carry_ref, acc_ref):
    tq = tri_ref.shape[1]
    n_items = iq_ref.shape[0]
    heads = range(LANES // HEAD_DIM)
    lane_head = lax.broadcasted_iota(jnp.int32, (1, LANES), 1) // HEAD_DIM
    row_head = lax.broadcasted_iota(jnp.int32, (LANES, tq), 0) // HEAD_DIM

    def blocks(item):
        qi, j = iq_ref[item], ik_ref[item]
        return pl.multiple_of(qi * tq, tq), pl.multiple_of(j * tq, tq), qi == j

    def front(item, slot):
        qs, ks, own = blocks(item)
        q = q_ref[pl.ds(qs, tq), :]
        kt = kt_ref[0, :, pl.ds(ks, tq)]
        bias = mask_ref[own.astype(jnp.int32)]
        for h in heads:
            z = _dot(q * (lane_head == h).astype(BF16), kt) + bias
            z_ref[slot, h] = z
            sp_ref[slot, h], rs_ref[slot, h] = _softplus2(z)

    def back(item, slot):
        qs, ks, own = blocks(item)
        vt = vt_ref[0, :, pl.ds(ks, tq)]
        a_parts, vt_parts = [], []
        for h in heads:
            carry = jnp.where(own, 0.0, carry_ref[h])
            a_parts.append(_stick_weights(z_ref[slot, h], sp_ref[slot, h], carry, tri_ref))
            carry_ref[h] = carry + rs_ref[slot, h]
            vt_parts.append(jnp.where(row_head == h, vt, jnp.zeros_like(vt)))
        acc = jnp.where(own, 0.0, acc_ref[...]) + _dot_nt(
            jnp.concatenate(a_parts, axis=1), jnp.concatenate(vt_parts, axis=1))
        acc_ref[...] = acc
        o_ref[pl.ds(qs, tq), :] = acc.astype(o_ref.dtype)

    depth = ATTN_PIPE // 2
    for s in range(depth):
        front(s, s)

    def trip(k, _):
        for s in range(ATTN_PIPE):
            item = ATTN_PIPE * k + s
            front(jnp.minimum(item + depth, n_items - 1), (s + depth) % ATTN_PIPE)
            back(item, s)
        return 0

    lax.fori_loop(0, n_items // ATTN_PIPE, trip, 0)


def _attn_prompt(q, ktb, vtb, tri):
    n, aw = q.shape
    nb, _, seq = ktb.shape
    tq = ATTN_BLOCK
    row = jnp.arange(tq)[:, None]
    col = jnp.arange(tq)[None, :]
    mask = jnp.stack([jnp.zeros((tq, tq), F32), jnp.where(col < row, 0.0, MASKED).astype(F32)])
    iq, ik = _attn_items(seq // tq)
    qspec = pl.BlockSpec((seq, LANES), lambda b, hp, iq, ik: (b, hp))
    kvspec = pl.BlockSpec((1, LANES, seq), lambda b, hp, iq, ik: (b, hp, 0))
    const = lambda a: pl.BlockSpec(a.shape, lambda b, hp, iq, ik: (0,) * a.ndim)
    return pl.pallas_call(
        _attn_prompt_kernel,
        grid_spec=pltpu.PrefetchScalarGridSpec(
            num_scalar_prefetch=2,
            grid=(nb, aw // LANES),
            in_specs=[qspec, kvspec, kvspec, const(tri), const(mask)],
            out_specs=qspec,
            scratch_shapes=[pltpu.VMEM((ATTN_PIPE, 2, tq, tq), F32),
                            pltpu.VMEM((ATTN_PIPE, 2, tq, tq), BF16),
                            pltpu.VMEM((ATTN_PIPE, 2, tq, 1), F32), pltpu.VMEM((2, tq, 1), F32),
                            pltpu.VMEM((tq, LANES), F32)]),
        out_shape=jax.ShapeDtypeStruct((n, aw), BF16),
        compiler_params=pltpu.CompilerParams(dimension_semantics=("arbitrary", "arbitrary"),
                                             vmem_limit_bytes=VMEM_LIMIT),
        name="attn_prompt",
    )(iq, ik, q, ktb, vtb, tri, mask)


def _attn_sample_kernel(q_ref, kn_ref, vn_ref, ck_ref, cv_ref, tri_ref, o_ref,
                        qrow_ref, carry_ref, acc_ref):
    tq, aw = q_ref.shape[1], q_ref.shape[2]
    rows = N_HEADS * tq
    tk = ATTN_BLOCK
    c = pl.program_id(1)

    @pl.when(c == 0)
    def _():
        row_head = lax.broadcasted_iota(jnp.int32, (rows, aw), 0) // tq
        lane_head = lax.broadcasted_iota(jnp.int32, (rows, aw), 1) // HEAD_DIM
        qt = jnp.concatenate([q_ref[0]] * N_HEADS, axis=0)
        qrow = jnp.where(row_head == lane_head, qt, jnp.zeros_like(qt))
        qrow_ref[...] = qrow
        pad = jnp.zeros((tk - tq, aw), BF16)
        kn = jnp.concatenate([kn_ref[0], pad], axis=0)
        vn = jnp.concatenate([vn_ref[0], pad], axis=0)
        qpos = lax.broadcasted_iota(jnp.int32, (rows, tk), 0) % tq
        kpos = lax.broadcasted_iota(jnp.int32, (rows, tk), 1)
        a, carry = _stick_block(_dot_nt(qrow, kn), jnp.zeros((rows, 1), F32), tri_ref,
                                kpos < qpos)
        carry_ref[...] = carry
        acc_ref[...] = _dot(a, vn)

    qrow = qrow_ref[...]
    nsub = ck_ref.shape[2] // tk
    for s in reversed(range(nsub)):
        kt = ck_ref[0, :, s * tk:(s + 1) * tk].astype(BF16)
        vt = cv_ref[0, :, s * tk:(s + 1) * tk].astype(BF16)
        a, carry = _stick_block(_dot(qrow, kt), carry_ref[...], tri_ref, None)
        carry_ref[...] = carry
        acc_ref[...] += _dot_nt(a, vt)

    @pl.when(c == pl.num_programs(1) - 1)
    def _():
        lane_head = lax.broadcasted_iota(jnp.int32, (tq, aw), 1) // HEAD_DIM
        out = jnp.zeros((tq, aw), F32)
        for head in range(N_HEADS):
            out = out + jnp.where(lane_head == head, acc_ref[head * tq:(head + 1) * tq, :], 0.0)
        o_ref[0] = out.astype(o_ref.dtype)


def _attn_sample(q, kn, vn, cache_kt, cache_vt, tri, *, chunk):
    nb, tq, aw = q.shape
    past = cache_kt.shape[2]
    nchunk = past // chunk
    new = pl.BlockSpec((1, tq, aw), lambda b, c: (b, 0, 0))
    cache = pl.BlockSpec((1, aw, chunk), lambda b, c: (b, 0, nchunk - 1 - c))
    rows = N_HEADS * tq
    return pl.pallas_call(
        _attn_sample_kernel,
        grid=(nb, nchunk),
        in_specs=[new, new, new, cache, cache, pl.BlockSpec(tri.shape, lambda b, c: (0, 0))],
        out_specs=new,
        out_shape=jax.ShapeDtypeStruct((nb, tq, aw), BF16),
        scratch_shapes=[pltpu.VMEM((rows, aw), BF16), pltpu.VMEM((rows, 1), F32),
                        pltpu.VMEM((rows, aw), F32)],
        compiler_params=pltpu.CompilerParams(dimension_semantics=("arbitrary", "arbitrary"),
                                             vmem_limit_bytes=VMEM_LIMIT),
        name="attn_sample",
    )(q, kn, vn, cache_kt, cache_vt, tri)


def _first_max(vals, lane):
    m = jnp.max(vals, axis=-1, keepdims=True)
    idx = jnp.min(jnp.where(vals == m, lane, float(LANES)), axis=-1, keepdims=True)
    return m, idx


def _router_gates(h, wrg_ref, brg_ref, wre_ref, bre_ref):
    lane = lax.broadcasted_iota(jnp.int32, (1, LANES), 1).astype(F32)
    neg = -jnp.inf
    lg = _dot(h, wrg_ref[...]) + brg_ref[...]
    lg = jnp.where(lane < N_GROUPS, lg, neg)
    mg, gidx = _first_max(lg, lane)
    p_sel = 1.0 / jnp.sum(jnp.exp(lg - mg), axis=-1, keepdims=True)
    le = _dot(h, wre_ref[...]) + bre_ref[...]
    in_group = jnp.floor(lane * (1.0 / EXPERTS_PER_GROUP)) == gidx
    le = jnp.where(in_group, le, neg)
    t1, i1 = _first_max(le, lane)
    t2, i2 = _first_max(jnp.where(lane == i1, neg, le), lane)
    e2 = jnp.exp(t2 - t1)
    w1 = 1.0 / (1.0 + e2)
    gate = p_sel * (jnp.where(lane == i1, w1, 0.0) + jnp.where(lane == i2, e2 * w1, 0.0))
    return gate, gidx


def _merge_route_kernel(x_ref, attn_ref, sa_ref, mb_ref, wab_ref, wout_ref, lnffn_ref,
                        wrg_ref, brg_ref, wre_ref, bre_ref, ltri_ref,
                        x1_ref, pay_ref, key_ref, cnt_out_ref, cnt_ref):
    t, d = x_ref.shape

    @pl.when(pl.program_id(0) == 0)
    def _():
        cnt_ref[...] = jnp.zeros_like(cnt_ref)

    ya = _dot(attn_ref[...], wab_ref[...])
    m = sa_ref[...].astype(F32) * ya + mb_ref[...].astype(F32)
    x1 = x_ref[...] + _dot(m.astype(BF16), wout_ref[...])
    x1_ref[...] = x1
    h = x1 * _rms_scale(x1) * lnffn_ref[...]
    gate, gidx = _router_gates(h.astype(BF16), wrg_ref, brg_ref, wre_ref, bre_ref)
    pay_ref[:, 0:d] = h
    pay_ref[:, d:d + LANES] = gate

    lane = lax.broadcasted_iota(jnp.int32, (1, LANES), 1).astype(F32)
    onehot = lane == gidx
    before = _dot(ltri_ref[...], onehot.astype(BF16)) + cnt_ref[...]
    rank = jnp.sum(jnp.where(onehot, before, 0.0), axis=-1, keepdims=True)
    cnt = cnt_ref[...] + jnp.sum(onehot.astype(F32), axis=0, keepdims=True)
    cnt_ref[...] = cnt
    cnt_out_ref[...] = cnt.astype(jnp.int32)
    key = gidx * float(1 << GROUP_SHIFT) + rank
    key_ref[0] = jnp.broadcast_to(key, (t, LANES)).T[0:1, :].astype(jnp.int32)


def _merge_route(x, attn, sa, mb, wts, *, tile):
    n, d = x.shape
    assert n < (1 << GROUP_SHIFT) and N_GROUPS << GROUP_SHIFT <= 1 << 24
    row = lambda c: pl.BlockSpec((tile, c), lambda i: (i, 0))
    ltri = (jnp.arange(tile)[:, None] > jnp.arange(tile)[None, :]).astype(BF16)
    consts = [wts['w_attn_branch'], wts['w_out'], wts['ln_ffn'], wts['w_router_group'],
              wts['b_router_group'], wts['w_router_expert'], wts['b_router_expert'], ltri]
    sds = jax.ShapeDtypeStruct
    x1, pay, keys, cnt = pl.pallas_call(
        _merge_route_kernel,
        grid=(n // tile,),
        in_specs=[row(d), row(attn.shape[1]), row(d), row(d)]
        + [_const_spec(c.shape) for c in consts],
        out_specs=[row(d), row(d + LANES), pl.BlockSpec((1, 1, tile), lambda i: (i, 0, 0)),
                   pl.BlockSpec((1, LANES), lambda i: (0, 0))],
        out_shape=[sds((n, d), F32), sds((n, d + LANES), F32),
                   sds((n // tile, 1, tile), jnp.int32), sds((1, LANES), jnp.int32)],
        scratch_shapes=[pltpu.VMEM((1, LANES), F32)],
        compiler_params=pltpu.CompilerParams(dimension_semantics=("arbitrary",),
                                             vmem_limit_bytes=VMEM_LIMIT),
        name="merge_route",
    )(x, attn, sa, mb, *consts)
    return x1, pay, keys, cnt[0, :N_GROUPS]


def _row_positions(key_row, off_ref):
    grp = key_row >> GROUP_SHIFT
    pos = key_row & ((1 << GROUP_SHIFT) - 1)
    for g in range(N_GROUPS):
        pos = pos + jnp.where(grp == g, off_ref[g], 0)
    return pos


def _to_smem(src_ref, dst_ref, sem):
    cp = pltpu.make_async_copy(src_ref, dst_ref, sem)
    cp.start()
    cp.wait()


def _dispatch_kernel(off_ref, cnt_ref, key_ref, pay_ref, xs_ref, stage_ref, posv_ref, pos_smem,
                     zero_ref, sem_s, sem_in, sem_row, sem_pad, *, tile_m):
    i = pl.program_id(0)
    n = pl.num_programs(0)
    t = key_ref.shape[2]
    ring = stage_ref.shape[0]
    slot = lax.rem(i, ring)
    nxt = lax.rem(i + 1, ring)

    def load(tile, s):
        return pltpu.make_async_copy(pay_ref.at[pl.ds(tile * t, t), :], stage_ref.at[s],
                                     sem_in.at[s])

    def wait_rows(s):
        pltpu.make_async_copy(stage_ref.at[s], xs_ref.at[pl.ds(0, t), :], sem_row.at[s]).wait()

    @pl.when(i == 0)
    def _():
        load(0, 0).start()

    @pl.when(i >= ring - 1)
    def _():
        wait_rows(nxt)

    @pl.when(i + 1 < n)
    def _():
        load(i + 1, nxt).start()

    posv_ref[...] = _row_positions(key_ref[0], off_ref)
    _to_smem(posv_ref, pos_smem, sem_s)
    load(i, slot).wait()

    def copy_row(r, _):
        pltpu.make_async_copy(stage_ref.at[slot, pl.ds(r, 1), :],
                              xs_ref.at[pl.ds(pos_smem[0, r], 1), :], sem_row.at[slot]).start()
        return 0

    lax.fori_loop(0, t, copy_row, 0, unroll=True)

    @pl.when(i == n - 1)
    def _():
        for back in range(ring - 1):
            @pl.when(i >= back)
            def _():
                wait_rows(lax.rem(i - back + ring, ring))
        zero_ref[...] = jnp.zeros_like(zero_ref)
        pads = []

        def pad(cond, start, size):
            pads.append((cond, pltpu.make_async_copy(
                zero_ref.at[pl.ds(0, size), :], xs_ref.at[pl.ds(start, size), :], sem_pad)))

        used = 0
        for g in range(N_GROUPS):
            c = cnt_ref[g]
            c8 = (c + 7) >> 3 << 3
            end = (c + tile_m - 1) // tile_m * tile_m
            for k in range(7):
                pad(c + k < c8, off_ref[g] + c + k, 1)
            start, size = off_ref[g] + c8, 8
            while size < tile_m:
                cond = ((end - c8) & size) != 0
                pad(cond, pl.multiple_of(start, 8), size)
                start = start + jnp.where(cond, size, 0)
                size *= 2
            used = off_ref[g] + end
        for k in range(N_GROUPS):
            start = used + k * tile_m
            pad(start < xs_ref.shape[0], pl.multiple_of(start, tile_m), tile_m)
        for cond, cp in pads:
            pl.when(cond)(cp.start)
        for cond, cp in pads:
            pl.when(cond)(cp.wait)


def _dispatch(pay, keys, off, counts, *, tile_m):
    n, w = pay.shape
    nt, _, tile = keys.shape
    rows = n + N_GROUPS * tile_m
    return pl.pallas_call(
        functools.partial(_dispatch_kernel, tile_m=tile_m),
        grid_spec=pltpu.PrefetchScalarGridSpec(
            num_scalar_prefetch=2,
            grid=(nt,),
            in_specs=[pl.BlockSpec((1, 1, tile), lambda i, off, cnt: (i, 0, 0)),
                      pl.BlockSpec(memory_space=pl.ANY)],
            out_specs=pl.BlockSpec(memory_space=pl.ANY),
            scratch_shapes=[pltpu.VMEM((DISPATCH_RING, tile, w), F32),
                            pltpu.VMEM((1, tile), jnp.int32), pltpu.SMEM((1, tile), jnp.int32),
                            pltpu.VMEM((tile_m, w), F32), pltpu.SemaphoreType.DMA,
                            pltpu.SemaphoreType.DMA((DISPATCH_RING,)),
                            pltpu.SemaphoreType.DMA((DISPATCH_RING,)),
                            pltpu.SemaphoreType.DMA]),
        out_shape=jax.ShapeDtypeStruct((rows, w), F32),
        compiler_params=pltpu.CompilerParams(dimension_semantics=("arbitrary",),
                                             vmem_limit_bytes=VMEM_LIMIT),
        name="dispatch",
    )(off, counts, keys, pay)


def _group_experts_kernel(grp_ref, xs_ref, weg_ref, weu_ref, wed_ref, y_ref):
    d = y_ref.shape[1]
    grp = grp_ref[pl.program_id(0)]
    h = xs_ref[:, 0:d].astype(BF16)
    gate = xs_ref[:, d:d + LANES]
    lane = lax.broadcasted_iota(jnp.int32, (1, LANES), 1)
    acts = []
    for j in range(EXPERTS_PER_GROUP):
        g = jnp.sum(jnp.where(lane == grp * EXPERTS_PER_GROUP + j, gate, 0.0),
                    axis=-1, keepdims=True)
        act = jax.nn.silu(_dot(h, weg_ref[0, j])) * _dot(h, weu_ref[0, j])
        acts.append((g * act).astype(BF16))
    y_ref[...] = _dot(jnp.concatenate(acts, axis=1), wed_ref[0])


def _group_experts(xs, tile_grp, wts, *, tile_m):
    d = xs.shape[1] - LANES
    n_exp, _, d_exp = wts['w_exp_gate'].shape
    epg = EXPERTS_PER_GROUP
    weg = wts['w_exp_gate'].reshape(N_GROUPS, epg, d, d_exp)
    weu = wts['w_exp_up'].reshape(N_GROUPS, epg, d, d_exp)
    wed = wts['w_exp_down'].reshape(N_GROUPS, epg * d_exp, d)
    return pl.pallas_call(
        _group_experts_kernel,
        grid_spec=pltpu.PrefetchScalarGridSpec(
            num_scalar_prefetch=1,
            grid=(xs.shape[0] // tile_m,),
            in_specs=[pl.BlockSpec((tile_m, d + LANES), lambda i, gr: (i, 0)),
                      pl.BlockSpec((1, epg, d, d_exp), lambda i, gr: (gr[i], 0, 0, 0)),
                      pl.BlockSpec((1, epg, d, d_exp), lambda i, gr: (gr[i], 0, 0, 0)),
                      pl.BlockSpec((1, epg * d_exp, d), lambda i, gr: (gr[i], 0, 0))],
            out_specs=pl.BlockSpec((tile_m, d), lambda i, gr: (i, 0))),
        out_shape=jax.ShapeDtypeStruct((xs.shape[0], d), F32),
        compiler_params=pltpu.CompilerParams(dimension_semantics=("arbitrary",),
                                             vmem_limit_bytes=VMEM_LIMIT),
        name="group_experts",
    )(tile_grp, xs, weg, weu, wed)


def _combine_ple_kernel(off_ref, x1_ref, p_ref, keyc_ref, keyn_ref, lnple_ref, wpg_ref, wpp_ref,
                        y_ref, o_ref, ybuf_ref, posv_ref, pos_smem, sem_y, sem_s):
    i = pl.program_id(0)
    n = pl.num_programs(0)
    t = x1_ref.shape[0]
    slot = i % 2

    def fetch(key_ref, s):
        posv_ref[...] = _row_positions(key_ref[0], off_ref)
        _to_smem(posv_ref, pos_smem.at[pl.ds(s, 1), :], sem_s)

        def gather_row(r, _):
            pltpu.make_async_copy(y_ref.at[pl.ds(pos_smem[s, r], 1), :],
                                  ybuf_ref.at[s, pl.ds(r, 1), :], sem_y.at[s]).start()
            return 0

        lax.fori_loop(0, t, gather_row, 0, unroll=True)

    @pl.when(i == 0)
    def _():
        fetch(keyc_ref, 0)

    @pl.when(i + 1 < n)
    def _():
        fetch(keyn_ref, 1 - slot)

    pltpu.make_async_copy(y_ref.at[pl.ds(0, t), :], ybuf_ref.at[slot], sem_y.at[slot]).wait()
    x2 = x1_ref[...] + ybuf_ref[slot]
    hp = (x2 * _rms_scale(x2) * lnple_ref[...]).astype(BF16)
    gate = jax.nn.sigmoid(_dot(hp, wpg_ref[...]))
    o_ref[...] = x2 + gate * _dot(p_ref[...].astype(BF16), wpp_ref[...])


def _combine_ple(x1, p, keys, off, y, wts):
    n, d = x1.shape
    nt, _, tile = keys.shape
    row = lambda c: pl.BlockSpec((tile, c), lambda i, off: (i, 0))
    consts = [wts['ln_ple'], wts['w_ple_gate'], wts['w_ple_proj']]
    return pl.pallas_call(
        _combine_ple_kernel,
        grid_spec=pltpu.PrefetchScalarGridSpec(
            num_scalar_prefetch=1,
            grid=(nt,),
            in_specs=[row(d), row(p.shape[1]),
                      pl.BlockSpec((1, 1, tile), lambda i, off: (i, 0, 0)),
                      pl.BlockSpec((1, 1, tile),
                                   lambda i, off: (jnp.minimum(i + 1, nt - 1), 0, 0))]
            + [_const_spec(c.shape) for c in consts] + [pl.BlockSpec(memory_space=pl.ANY)],
            out_specs=row(d),
            scratch_shapes=[pltpu.VMEM((2, tile, d), F32), pltpu.VMEM((1, tile), jnp.int32),
                            pltpu.SMEM((2, tile), jnp.int32), pltpu.SemaphoreType.DMA((2,)),
                            pltpu.SemaphoreType.DMA]),
        out_shape=jax.ShapeDtypeStruct((n, d), F32),
        compiler_params=pltpu.CompilerParams(dimension_semantics=("arbitrary",),
                                             vmem_limit_bytes=VMEM_LIMIT),
        name="combine_ple",
    )(off, x1, p, keys, keys, *consts, y)


def _channel_stage(x, attn, sa, mb, p, wts, *, tile, tile_m):
    n = x.shape[0]
    assert n % tile == 0 and n % tile_m == 0
    x1, pay, keys, counts = _merge_route(x, attn, sa, mb, wts, tile=tile)
    ends = jnp.cumsum((counts + tile_m - 1) // tile_m * tile_m)
    off = jnp.concatenate([jnp.zeros((1,), jnp.int32), ends[:-1]])
    xs = _dispatch(pay, keys, off, counts, tile_m=tile_m)
    tile_start = jnp.arange(xs.shape[0] // tile_m, dtype=jnp.int32) * tile_m
    tile_grp = jnp.minimum(jnp.sum(tile_start[:, None] >= ends[None, :], axis=1), N_GROUPS - 1)
    y = _group_experts(xs, tile_grp.astype(jnp.int32), wts, tile_m=tile_m)
    return _combine_ple(x1, p, keys, off, y, wts)


def _layer_weights(i, ln_mix, w_in, q_norm, k_norm, conv_w, w_attn_branch, w_conv_branch, w_out,
                   ln_ffn, w_router_group, b_router_group, w_router_expert, b_router_expert,
                   w_exp_gate, w_exp_up, w_exp_down, ln_ple, w_ple_gate, w_ple_proj):
    aw = ATTN_WIDTH
    pad_cols = lambda a: jnp.pad(a, ((0, 0), (0, LANES - a.shape[1])))
    head_id = jnp.arange(aw) // HEAD_DIM
    k_gain = jnp.tile(k_norm[i], N_HEADS)
    return dict(
        ln_mix=ln_mix[i][None, :], w_in=w_in[i].astype(BF16),
        w_kv_t=w_in[i][:, aw:3 * aw].T.astype(BF16),
        q_norm=jnp.tile(q_norm[i], N_HEADS)[None, :], k_norm=k_gain[None, :],
        k_norm_t=jnp.broadcast_to(k_gain[:, None], (aw, LANES)),
        head_sum=(head_id[:, None] == head_id[None, :]).astype(BF16),
        conv_w=conv_w[i], w_conv_branch=w_conv_branch[i].astype(BF16),
        w_attn_branch=w_attn_branch[i].astype(BF16), w_out=w_out[i].astype(BF16),
        ln_ffn=ln_ffn[i][None, :],
        w_router_group=pad_cols(w_router_group[i]).astype(BF16),
        b_router_group=pad_cols(b_router_group[i][None, :]),
        w_router_expert=pad_cols(w_router_expert[i]).astype(BF16),
        b_router_expert=pad_cols(b_router_expert[i][None, :]),
        w_exp_gate=w_exp_gate[i].astype(BF16), w_exp_up=w_exp_up[i].astype(BF16),
        w_exp_down=w_exp_down[i].astype(BF16),
        ln_ple=ln_ple[i][None, :], w_ple_gate=w_ple_gate[i].astype(BF16),
        w_ple_proj=w_ple_proj[i].astype(BF16))


def _feature_major(a):
    b, s, h, dh = a.shape
    return jnp.transpose(a, (0, 2, 3, 1)).reshape(b, h * dh, s)


def _position_major(a):
    b, _, s = a.shape
    return jnp.transpose(a.reshape(b, N_HEADS, HEAD_DIM, s), (0, 3, 1, 2))


def kernel(x_prompt, x_sample, cache_k, cache_v, state_conv, p_prompt, p_sample, ln_mix, w_in, q_norm, k_norm, conv_w, w_attn_branch, w_conv_branch, w_out, ln_ffn, w_router_group, b_router_group, w_router_expert, b_router_expert, w_exp_gate, w_exp_up, w_exp_down, ln_ple, w_ple_gate, w_ple_proj):
    depth = ln_mix.shape[0]
    nb, seq, d = x_prompt.shape
    nbs, seqs, _ = x_sample.shape
    aw = ATTN_WIDTH
    assert cache_k.shape[3:] == (N_HEADS, HEAD_DIM)
    assert w_router_group.shape[2] == N_GROUPS
    assert w_router_expert.shape[2] == N_GROUPS * EXPERTS_PER_GROUP
    tile = 512
    tri = (jnp.arange(ATTN_BLOCK)[:, None] >= jnp.arange(ATTN_BLOCK)[None, :]).astype(BF16)

    xp = x_prompt.reshape(nb * seq, d)
    xs = x_sample.reshape(nbs * seqs, d)
    outs = [[] for _ in range(6)]
    for i in range(depth):
        wts = _layer_weights(i, ln_mix, w_in, q_norm, k_norm, conv_w, w_attn_branch,
                             w_conv_branch, w_out, ln_ffn, w_router_group, b_router_group,
                             w_router_expert, b_router_expert, w_exp_gate, w_exp_up, w_exp_down,
                             ln_ple, w_ple_gate, w_ple_proj)

        (q, kt, vt, ktb, vtb, sa, mb), conv_new = _input_stage_prompt(xp, wts, seq=seq, tile=tile)
        attn = _attn_prompt(q, ktb, vtb, tri)
        xp = _channel_stage(xp, attn, sa, mb, p_prompt[i].reshape(nb * seq, -1), wts, tile=tile,
                            tile_m=512)
        outs[0].append(_position_major(kt))
        outs[1].append(_position_major(vt))
        outs[2].append(conv_new)

        (q, k, v, kb, vb, sa, mb), conv_new = _input_stage_sample(xs, state_conv[i], wts, seq=seqs)
        attn = _attn_sample(q.reshape(nbs, seqs, aw), kb.reshape(nbs, seqs, aw),
                            vb.reshape(nbs, seqs, aw), _feature_major(cache_k[i]),
                            _feature_major(cache_v[i]), tri, chunk=1024)
        xs = _channel_stage(xs, attn.reshape(nbs * seqs, aw), sa, mb,
                            p_sample[i].reshape(nbs * seqs, -1), wts, tile=nbs * seqs,
                            tile_m=min(256, nbs * seqs))
        outs[3].append(k.reshape(nbs, seqs, N_HEADS, HEAD_DIM))
        outs[4].append(v.reshape(nbs, seqs, N_HEADS, HEAD_DIM))
        outs[5].append(conv_new)

    kp, vp, cp, ks, vs, cs = [jnp.stack(o) for o in outs]
    return (xp.reshape(nb, seq, d), xs.reshape(nbs, seqs, d), kp, vp, cp, ks, vs, cs)
```

```python
import functools

import jax
import jax.numpy as jnp
from jax import lax
from jax.experimental import pallas as pl
from jax.experimental.pallas import tpu as pltpu

F32 = jnp.float32
BF16 = jnp.bfloat16

EPS = 1e-6
N_HEADS = 8
HEAD_DIM = 64
ATTN_WIDTH = N_HEADS * HEAD_DIM
N_GROUPS = 4
EXPERTS_PER_GROUP = 4
GROUP_SHIFT = 20
DISPATCH_RING = 3
LANES = 128
ATTN_BLOCK = 256
MASKED = -1e30
SP_CLAMP = 64.0
ATTN_PIPE = 8
ATTN_EAGER = 2
WEIGHT_GONE = 150.0
LOG2E_HI = 1.4426950216293335
LOG2E_LO = 1.925963033500011e-08
VMEM_LIMIT = 56 * 1024 * 1024


def _dot(a, b):
    return jnp.dot(a, b, preferred_element_type=F32)


def _dot_nt(a, b):
    return lax.dot_general(a, b, (((1,), (1,)), ((), ())), preferred_element_type=F32)


def _rms_scale(x):
    return lax.rsqrt(jnp.mean(x * x, axis=-1, keepdims=True) + EPS)


def _const_spec(shape):
    nd = len(shape)
    return pl.BlockSpec(shape, lambda *_: (0,) * nd, pipeline_mode=pl.Buffered(1))


def _head_norm(z, gain, hsum_ref):
    ss = _dot((z * z).astype(BF16), hsum_ref[...])
    return z * lax.rsqrt(ss * (1.0 / HEAD_DIM) + EPS) * gain


def _input_stage_body(x, past1, past2, lnmix_ref, win_ref, qn_ref, hsum_ref, convw_ref, wcb_ref,
                      q_ref, sa_ref, mb_ref):
    aw = ATTN_WIDTH
    cw = convw_ref.shape[1]
    d = x.shape[1]
    h = (x * _rms_scale(x) * lnmix_ref[...]).astype(BF16)

    def proj(lo, width):
        return _dot(h, win_ref[:, lo:lo + width])

    q = _head_norm(proj(0, aw), qn_ref[...], hsum_ref)
    q = q * (HEAD_DIM ** -0.5)
    q_ref[...] = (q * LOG2E_HI + q * LOG2E_LO).astype(BF16)

    cb = proj(3 * aw, cw)
    u = proj(3 * aw + cw, cw) * proj(3 * aw + 2 * cw, cw)
    u1, u2 = past1(u), past2(u)
    conv_y = convw_ref[0:1, :] * u2 + convw_ref[1:2, :] * u1 + convw_ref[2:3, :] * u
    yb = _dot((cb * conv_y).astype(BF16), wcb_ref[...])
    ga = proj(3 * aw + 3 * cw, d)
    gb = proj(3 * aw + 3 * cw + d, d)
    sa_ref[...] = jax.nn.sigmoid(ga).astype(BF16)
    mb_ref[...] = (jax.nn.sigmoid(gb) * yb).astype(BF16)
    return h, proj, u


def _input_stage_prompt_kernel(x_ref, lnmix_ref, win_ref, wkvt_ref, qn_ref, knt_ref, hsum_ref,
                               convw_ref, wcb_ref, q_ref, kt_ref, vt_ref, ktb_ref, vtb_ref,
                               sa_ref, mb_ref, cnew_ref, tail_ref, *, tiles_per_seq):
    t = x_ref.shape[0]
    aw = ATTN_WIDTH

    @pl.when(pl.program_id(0) % tiles_per_seq == 0)
    def _():
        tail_ref[...] = jnp.zeros_like(tail_ref)

    def shifted(u, n):
        ext = jnp.concatenate([tail_ref[...], u], axis=0)
        return pltpu.roll(ext, n, axis=0)[8:, :]

    h, _, u = _input_stage_body(x_ref[...], lambda u: shifted(u, 1), lambda u: shifted(u, 2),
                                lnmix_ref, win_ref, qn_ref, hsum_ref, convw_ref, wcb_ref,
                                q_ref, sa_ref, mb_ref)
    tail_ref[...] = u[t - 8:, :]
    cnew_ref[0] = u[t - 8:, :]

    kt = _dot_nt(wkvt_ref[0:aw, :], h)
    k3 = kt.reshape(N_HEADS, HEAD_DIM, t)
    scale = lax.rsqrt(jnp.mean(k3 * k3, axis=1, keepdims=True) + EPS)
    kt = (k3 * scale).reshape(aw, t) * jnp.tile(knt_ref[...], (1, t // LANES))
    kt_ref[0] = kt
    ktb_ref[0] = kt.astype(BF16)
    vt = _dot_nt(wkvt_ref[aw:2 * aw, :], h)
    vt_ref[0] = vt
    vtb_ref[0] = vt.astype(BF16)


def _input_stage_sample_kernel(x_ref, pe0_ref, pe1_ref, lnmix_ref, win_ref, qn_ref, kn_ref,
                               hsum_ref, convw_ref, wcb_ref, q_ref, k_ref, v_ref, kb_ref,
                               vb_ref, sa_ref, mb_ref, u_ref, *, seq):
    t = x_ref.shape[0]
    aw = ATTN_WIDTH
    pos = lax.broadcasted_iota(jnp.int32, (t, 1), 0) % seq

    def past1(u):
        return jnp.where(pos == 0, pe1_ref[...], pltpu.roll(u, 1, axis=0))

    def past2(u):
        return jnp.where(pos == 0, pe0_ref[...],
                         jnp.where(pos == 1, pe1_ref[...], pltpu.roll(u, 2, axis=0)))

    _, proj, u = _input_stage_body(x_ref[...], past1, past2, lnmix_ref, win_ref, qn_ref,
                                   hsum_ref, convw_ref, wcb_ref, q_ref, sa_ref, mb_ref)
    u_ref[...] = u
    k = _head_norm(proj(aw, aw), kn_ref[...], hsum_ref)
    k_ref[...] = k
    kb_ref[...] = k.astype(BF16)
    v = proj(2 * aw, aw)
    v_ref[...] = v
    vb_ref[...] = v.astype(BF16)


def _input_stage_prompt(x, wts, *, seq, tile):
    n, d = x.shape
    nb = n // seq
    aw = ATTN_WIDTH
    cw = wts['conv_w'].shape[1]
    tiles_per_seq = seq // tile
    row = lambda w: pl.BlockSpec((tile, w), lambda i: (i, 0))
    feat = pl.BlockSpec((1, aw, tile), lambda i: (i // tiles_per_seq, 0, i % tiles_per_seq))
    consts = [wts['ln_mix'], wts['w_in'], wts['w_kv_t'], wts['q_norm'], wts['k_norm_t'],
              wts['head_sum'], wts['conv_w'], wts['w_conv_branch']]
    sds = jax.ShapeDtypeStruct
    outs = pl.pallas_call(
        functools.partial(_input_stage_prompt_kernel, tiles_per_seq=tiles_per_seq),
        grid=(n // tile,),
        in_specs=[row(d)] + [_const_spec(c.shape) for c in consts],
        out_specs=[row(aw), feat, feat, feat, feat, row(d), row(d),
                   pl.BlockSpec((1, 8, cw), lambda i: (i // tiles_per_seq, 0, 0))],
        out_shape=[sds((n, aw), BF16), sds((nb, aw, seq), F32), sds((nb, aw, seq), F32),
                   sds((nb, aw, seq), BF16), sds((nb, aw, seq), BF16), sds((n, d), BF16),
                   sds((n, d), BF16), sds((nb, 8, cw), F32)],
        scratch_shapes=[pltpu.VMEM((8, cw), F32)],
        compiler_params=pltpu.CompilerParams(dimension_semantics=("arbitrary",),
                                             vmem_limit_bytes=VMEM_LIMIT),
        name="input_stage_prompt",
    )(x, *consts)
    return outs[:7], outs[7][:, 6:, :]


def _input_stage_sample(x, state, wts, *, seq):
    n, d = x.shape
    aw = ATTN_WIDTH
    cw = wts['conv_w'].shape[1]
    pe0 = jnp.repeat(state[:, 0, :], seq, axis=0)
    pe1 = jnp.repeat(state[:, 1, :], seq, axis=0)
    consts = [wts['ln_mix'], wts['w_in'], wts['q_norm'], wts['k_norm'], wts['head_sum'],
              wts['conv_w'], wts['w_conv_branch']]
    full = lambda a: pl.BlockSpec(a.shape, lambda i: (0,) * a.ndim)
    sds = jax.ShapeDtypeStruct
    out_shape = [sds((n, aw), BF16), sds((n, aw), F32), sds((n, aw), F32), sds((n, aw), BF16),
                 sds((n, aw), BF16), sds((n, d), BF16), sds((n, d), BF16), sds((n, cw), F32)]
    outs = pl.pallas_call(
        functools.partial(_input_stage_sample_kernel, seq=seq),
        grid=(1,),
        in_specs=[full(x), full(pe0), full(pe1)] + [full(c) for c in consts],
        out_specs=[full(o) for o in out_shape],
        out_shape=out_shape,
        compiler_params=pltpu.CompilerParams(dimension_semantics=("arbitrary",),
                                             vmem_limit_bytes=VMEM_LIMIT),
        name="input_stage_sample",
    )(x, pe0, pe1, *consts)
    conv_new = outs[7].reshape(n // seq, seq, cw)[:, seq - 2:, :]
    return outs[:7], conv_new


def _softplus2(z, keep=None):
    sp = jnp.maximum(z, jnp.log2(1.0 + jnp.exp2(jnp.minimum(z, SP_CLAMP))))
    if keep is not None:
        sp = jnp.where(keep, sp, 0.0)
    return sp.astype(BF16), jnp.sum(sp, axis=-1, keepdims=True)


def _stick_weights(z, sp, carry, tri_ref, keep=None):
    a = jnp.exp2(jnp.minimum(z - _dot(sp, tri_ref[...]), 0.0) - carry)
    if keep is not None:
        a = jnp.where(keep, a, 0.0)
    return a.astype(BF16)


def _stick_block(z, carry, tri_ref, keep):
    sp, row_sum = _softplus2(z, keep)
    return _stick_weights(z, sp, carry, tri_ref, keep), carry + row_sum


def _attn_items(nq):
    items = [(qi, j) for qi in range(nq) for j in range(qi, max(qi - ATTN_EAGER, -1), -1)]
    items = [(0, 0)] * (-len(items) % ATTN_PIPE) + items
    return (jnp.array([i[0] for i in items], jnp.int32),
            jnp.array([i[1] for i in items], jnp.int32))


def _attn_prompt_kernel(iq_ref, ik_ref, q_ref, kt_ref, vt_ref, tri_ref, mask_ref, o_ref,
                        z_ref, sp_ref, rs_ref, carry_ref, acc_ref):
    tq = tri_ref.shape[1]
    nq = q_ref.shape[0] // tq
    n_items = iq_ref.shape[0]
    heads = range(LANES // HEAD_DIM)
    lane_head = lax.broadcasted_iota(jnp.int32, (1, LANES), 1) // HEAD_DIM
    row_head = lax.broadcasted_iota(jnp.int32, (LANES, tq), 0) // HEAD_DIM

    def head_q(q, h):
        return q * (lane_head == h).astype(BF16)

    def weighted_v(a_parts, ks):
        vt = vt_ref[0, :, pl.ds(ks, tq)]
        vts = [jnp.where(row_head == h, vt, jnp.zeros_like(vt)) for h in heads]
        return _dot_nt(jnp.concatenate(a_parts, axis=1), jnp.concatenate(vts, axis=1))

    def blocks(item):
        qi, j = iq_ref[item], ik_ref[item]
        return qi, pl.multiple_of(qi * tq, tq), pl.multiple_of(j * tq, tq), qi == j

    def front(item, slot):
        _, qs, ks, own = blocks(item)
        q = q_ref[pl.ds(qs, tq), :]
        kt = kt_ref[0, :, pl.ds(ks, tq)]
        bias = mask_ref[own.astype(jnp.int32)]
        for h in heads:
            z = _dot(head_q(q, h), kt) + bias
            z_ref[slot, h] = z
            sp_ref[slot, h], rs_ref[slot, h] = _softplus2(z)

    def back(item, slot):
        qi, qs, ks, own = blocks(item)
        a_parts = []
        for h in heads:
            carry = jnp.where(own, 0.0, carry_ref[qi, h])
            a_parts.append(_stick_weights(z_ref[slot, h], sp_ref[slot, h], carry, tri_ref))
            carry_ref[qi, h] = carry + rs_ref[slot, h]
        acc_ref[pl.ds(qs, tq), :] = (jnp.where(own, 0.0, acc_ref[pl.ds(qs, tq), :])
                                     + weighted_v(a_parts, ks))

    depth = ATTN_PIPE // 2
    for s in range(depth):
        front(s, s)

    def trip(k, _):
        for s in range(ATTN_PIPE):
            item = ATTN_PIPE * k + s
            front(jnp.minimum(item + depth, n_items - 1), (s + depth) % ATTN_PIPE)
            back(item, s)
        return 0

    lax.fori_loop(0, n_items // ATTN_PIPE, trip, 0)

    def finish(qi, _):
        qs = pl.multiple_of(qi * tq, tq)

        def more(state):
            j, c0, c1 = state
            return jnp.logical_and(j >= 0, jnp.min(jnp.minimum(c0, c1)) < WEIGHT_GONE)

        def step(state):
            j, *carries = state
            ks = pl.multiple_of(j * tq, tq)
            q = q_ref[pl.ds(qs, tq), :]
            kt = kt_ref[0, :, pl.ds(ks, tq)]
            a_parts = []
            for h in heads:
                a, carries[h] = _stick_block(_dot(head_q(q, h), kt), carries[h], tri_ref, None)
                a_parts.append(a)
            acc_ref[pl.ds(qs, tq), :] += weighted_v(a_parts, ks)
            return (j - 1, *carries)

        lax.while_loop(more, step, (qi - ATTN_EAGER, carry_ref[qi, 0], carry_ref[qi, 1]))
        return 0

    lax.fori_loop(ATTN_EAGER, nq, finish, 0)
    o_ref[...] = acc_ref[...].astype(o_ref.dtype)


def _attn_prompt(q, ktb, vtb, tri):
    n, aw = q.shape
    nb, _, seq = ktb.shape
    tq = ATTN_BLOCK
    row = jnp.arange(tq)[:, None]
    col = jnp.arange(tq)[None, :]
    mask = jnp.stack([jnp.zeros((tq, tq), F32), jnp.where(col < row, 0.0, MASKED).astype(F32)])
    iq, ik = _attn_items(seq // tq)
    qspec = pl.BlockSpec((seq, LANES), lambda b, hp, iq, ik: (b, hp))
    kvspec = pl.BlockSpec((1, LANES, seq), lambda b, hp, iq, ik: (b, hp, 0))
    const = lambda a: pl.BlockSpec(a.shape, lambda b, hp, iq, ik: (0,) * a.ndim)
    return pl.pallas_call(
        _attn_prompt_kernel,
        grid_spec=pltpu.PrefetchScalarGridSpec(
            num_scalar_prefetch=2,
            grid=(nb, aw // LANES),
            in_specs=[qspec, kvspec, kvspec, const(tri), const(mask)],
            out_specs=qspec,
            scratch_shapes=[pltpu.VMEM((ATTN_PIPE, 2, tq, tq), F32),
                            pltpu.VMEM((ATTN_PIPE, 2, tq, tq), BF16),
                            pltpu.VMEM((ATTN_PIPE, 2, tq, 1), F32),
                            pltpu.VMEM((seq // tq, 2, tq, 1), F32),
                            pltpu.VMEM((seq, LANES), F32)]),
        out_shape=jax.ShapeDtypeStruct((n, aw), BF16),
        compiler_params=pltpu.CompilerParams(dimension_semantics=("arbitrary", "arbitrary"),
                                             vmem_limit_bytes=VMEM_LIMIT),
        name="attn_prompt",
    )(iq, ik, q, ktb, vtb, tri, mask)


def _attn_sample_kernel(q_ref, kn_ref, vn_ref, tri_ref, ck_ref, cv_ref, o_ref,
                        kbuf_ref, vbuf_ref, acc_ref, sem):
    b = pl.program_id(0)
    tq, aw = q_ref.shape[1], q_ref.shape[2]
    rows = N_HEADS * tq
    tk = ATTN_BLOCK
    last = ck_ref.shape[2] // tk - 1

    def fetch(j):
        ks = pl.multiple_of(j * tk, tk)
        return (pltpu.make_async_copy(ck_ref.at[b, :, pl.ds(ks, tk)], kbuf_ref, sem.at[0]),
                pltpu.make_async_copy(cv_ref.at[b, :, pl.ds(ks, tk)], vbuf_ref, sem.at[1]))

    for cp in fetch(last):
        cp.start()
    row_head = lax.broadcasted_iota(jnp.int32, (rows, aw), 0) // tq
    lane_head = lax.broadcasted_iota(jnp.int32, (rows, aw), 1) // HEAD_DIM
    qt = jnp.concatenate([q_ref[0]] * N_HEADS, axis=0)
    qrow = jnp.where(row_head == lane_head, qt, jnp.zeros_like(qt))
    pad = jnp.zeros((tk - tq, aw), BF16)
    kn = jnp.concatenate([kn_ref[0], pad], axis=0)
    vn = jnp.concatenate([vn_ref[0], pad], axis=0)
    qpos = lax.broadcasted_iota(jnp.int32, (rows, tk), 0) % tq
    kpos = lax.broadcasted_iota(jnp.int32, (rows, tk), 1)
    a, carry = _stick_block(_dot_nt(qrow, kn), jnp.zeros((rows, 1), F32), tri_ref, kpos < qpos)
    acc_ref[...] = _dot(a, vn)
    for cp in fetch(last):
        cp.wait()

    def step(state):
        j, _, carry = state
        a, carry = _stick_block(_dot(qrow, kbuf_ref[...].astype(BF16)), carry, tri_ref, None)
        acc_ref[...] += _dot_nt(a, vbuf_ref[...].astype(BF16))
        go_on = jnp.logical_and(j > 0, jnp.min(carry) < WEIGHT_GONE)

        @pl.when(go_on)
        def _():
            for cp in fetch(j - 1):
                cp.start()
            for cp in fetch(j - 1):
                cp.wait()

        return j - 1, go_on, carry

    lax.while_loop(lambda state: state[1], step, (jnp.int32(last), True, carry))

    lane_head = lax.broadcasted_iota(jnp.int32, (tq, aw), 1) // HEAD_DIM
    out = jnp.zeros((tq, aw), F32)
    for head in range(N_HEADS):
        out = out + jnp.where(lane_head == head, acc_ref[head * tq:(head + 1) * tq, :], 0.0)
    o_ref[0] = out.astype(o_ref.dtype)


def _attn_sample(q, kn, vn, cache_kt, cache_vt, tri):
    nb, tq, aw = q.shape
    new = pl.BlockSpec((1, tq, aw), lambda b: (b, 0, 0))
    hbm = pl.BlockSpec(memory_space=pl.ANY)
    return pl.pallas_call(
        _attn_sample_kernel,
        grid=(nb,),
        in_specs=[new, new, new, pl.BlockSpec(tri.shape, lambda b: (0, 0)), hbm, hbm],
        out_specs=new,
        out_shape=jax.ShapeDtypeStruct((nb, tq, aw), BF16),
        scratch_shapes=[pltpu.VMEM((aw, ATTN_BLOCK), F32), pltpu.VMEM((aw, ATTN_BLOCK), F32),
                        pltpu.VMEM((N_HEADS * tq, aw), F32), pltpu.SemaphoreType.DMA((2,))],
        compiler_params=pltpu.CompilerParams(dimension_semantics=("arbitrary",),
                                             vmem_limit_bytes=VMEM_LIMIT),
        name="attn_sample",
    )(q, kn, vn, tri, cache_kt, cache_vt)


def _first_max(vals, lane):
    m = jnp.max(vals, axis=-1, keepdims=True)
    idx = jnp.min(jnp.where(vals == m, lane, float(LANES)), axis=-1, keepdims=True)
    return m, idx


def _router_gates(h, wrg_ref, brg_ref, wre_ref, bre_ref):
    lane = lax.broadcasted_iota(jnp.int32, (1, LANES), 1).astype(F32)
    neg = -jnp.inf
    lg = _dot(h, wrg_ref[...]) + brg_ref[...]
    lg = jnp.where(lane < N_GROUPS, lg, neg)
    mg, gidx = _first_max(lg, lane)
    p_sel = 1.0 / jnp.sum(jnp.exp(lg - mg), axis=-1, keepdims=True)
    le = _dot(h, wre_ref[...]) + bre_ref[...]
    in_group = jnp.floor(lane * (1.0 / EXPERTS_PER_GROUP)) == gidx
    le = jnp.where(in_group, le, neg)
    t1, i1 = _first_max(le, lane)
    t2, i2 = _first_max(jnp.where(lane == i1, neg, le), lane)
    e2 = jnp.exp(t2 - t1)
    w1 = 1.0 / (1.0 + e2)
    gate = p_sel * (jnp.where(lane == i1, w1, 0.0) + jnp.where(lane == i2, e2 * w1, 0.0))
    return gate, gidx


def _merge_route_kernel(x_ref, attn_ref, sa_ref, mb_ref, wab_ref, wout_ref, lnffn_ref,
                        wrg_ref, brg_ref, wre_ref, bre_ref, ltri_ref,
                        x1_ref, pay_ref, key_ref, cnt_out_ref, cnt_ref):
    t, d = x_ref.shape

    @pl.when(pl.program_id(0) == 0)
    def _():
        cnt_ref[...] = jnp.zeros_like(cnt_ref)

    ya = _dot(attn_ref[...], wab_ref[...])
    m = sa_ref[...].astype(F32) * ya + mb_ref[...].astype(F32)
    x1 = x_ref[...] + _dot(m.astype(BF16), wout_ref[...])
    x1_ref[...] = x1
    h = x1 * _rms_scale(x1) * lnffn_ref[...]
    gate, gidx = _router_gates(h.astype(BF16), wrg_ref, brg_ref, wre_ref, bre_ref)
    pay_ref[:, 0:d] = h
    pay_ref[:, d:d + LANES] = gate

    lane = lax.broadcasted_iota(jnp.int32, (1, LANES), 1).astype(F32)
    onehot = lane == gidx
    before = _dot(ltri_ref[...], onehot.astype(BF16)) + cnt_ref[...]
    rank = jnp.sum(jnp.where(onehot, before, 0.0), axis=-1, keepdims=True)
    cnt = cnt_ref[...] + jnp.sum(onehot.astype(F32), axis=0, keepdims=True)
    cnt_ref[...] = cnt
    cnt_out_ref[...] = cnt.astype(jnp.int32)
    key = gidx * float(1 << GROUP_SHIFT) + rank
    key_ref[0] = jnp.broadcast_to(key, (t, LANES)).T[0:1, :].astype(jnp.int32)


def _merge_route(x, attn, sa, mb, wts, *, tile):
    n, d = x.shape
    assert n < (1 << GROUP_SHIFT) and N_GROUPS << GROUP_SHIFT <= 1 << 24
    row = lambda c: pl.BlockSpec((tile, c), lambda i: (i, 0))
    ltri = (jnp.arange(tile)[:, None] > jnp.arange(tile)[None, :]).astype(BF16)
    consts = [wts['w_attn_branch'], wts['w_out'], wts['ln_ffn'], wts['w_router_group'],
              wts['b_router_group'], wts['w_router_expert'], wts['b_router_expert'], ltri]
    sds = jax.ShapeDtypeStruct
    x1, pay, keys, cnt = pl.pallas_call(
        _merge_route_kernel,
        grid=(n // tile,),
        in_specs=[row(d), row(attn.shape[1]), row(d), row(d)]
        + [_const_spec(c.shape) for c in consts],
        out_specs=[row(d), row(d + LANES), pl.BlockSpec((1, 1, tile), lambda i: (i, 0, 0)),
                   pl.BlockSpec((1, LANES), lambda i: (0, 0))],
        out_shape=[sds((n, d), F32), sds((n, d + LANES), F32),
                   sds((n // tile, 1, tile), jnp.int32), sds((1, LANES), jnp.int32)],
        scratch_shapes=[pltpu.VMEM((1, LANES), F32)],
        compiler_params=pltpu.CompilerParams(dimension_semantics=("arbitrary",),
                                             vmem_limit_bytes=VMEM_LIMIT),
        name="merge_route",
    )(x, attn, sa, mb, *consts)
    return x1, pay, keys, cnt[0, :N_GROUPS]


def _row_positions(key_row, off_ref):
    grp = key_row >> GROUP_SHIFT
    pos = key_row & ((1 << GROUP_SHIFT) - 1)
    for g in range(N_GROUPS):
        pos = pos + jnp.where(grp == g, off_ref[g], 0)
    return pos


def _to_smem(src_ref, dst_ref, sem):
    cp = pltpu.make_async_copy(src_ref, dst_ref, sem)
    cp.start()
    cp.wait()


def _dispatch_kernel(off_ref, cnt_ref, key_ref, pay_ref, xs_ref, stage_ref, posv_ref, pos_smem,
                     zero_ref, sem_s, sem_in, sem_row, sem_pad, *, tile_m):
    i = pl.program_id(0)
    n = pl.num_programs(0)
    t = key_ref.shape[2]
    ring = stage_ref.shape[0]
    slot = lax.rem(i, ring)
    nxt = lax.rem(i + 1, ring)

    def load(tile, s):
        return pltpu.make_async_copy(pay_ref.at[pl.ds(tile * t, t), :], stage_ref.at[s],
                                     sem_in.at[s])

    def wait_rows(s):
        pltpu.make_async_copy(stage_ref.at[s], xs_ref.at[pl.ds(0, t), :], sem_row.at[s]).wait()

    @pl.when(i == 0)
    def _():
        load(0, 0).start()

    @pl.when(i >= ring - 1)
    def _():
        wait_rows(nxt)

    @pl.when(i + 1 < n)
    def _():
        load(i + 1, nxt).start()

    posv_ref[...] = _row_positions(key_ref[0], off_ref)
    _to_smem(posv_ref, pos_smem, sem_s)
    load(i, slot).wait()

    def copy_row(r, _):
        pltpu.make_async_copy(stage_ref.at[slot, pl.ds(r, 1), :],
                              xs_ref.at[pl.ds(pos_smem[0, r], 1), :], sem_row.at[slot]).start()
        return 0

    lax.fori_loop(0, t, copy_row, 0, unroll=True)

    @pl.when(i == n - 1)
    def _():
        for back in range(ring - 1):
            @pl.when(i >= back)
            def _():
                wait_rows(lax.rem(i - back + ring, ring))
        zero_ref[...] = jnp.zeros_like(zero_ref)
        pads = []

        def pad(cond, start, size):
            pads.append((cond, pltpu.make_async_copy(
                zero_ref.at[pl.ds(0, size), :], xs_ref.at[pl.ds(start, size), :], sem_pad)))

        used = 0
        for g in range(N_GROUPS):
            c = cnt_ref[g]
            c8 = (c + 7) >> 3 << 3
            end = (c + tile_m - 1) // tile_m * tile_m
            for k in range(7):
                pad(c + k < c8, off_ref[g] + c + k, 1)
            start, size = off_ref[g] + c8, 8
            while size < tile_m:
                cond = ((end - c8) & size) != 0
                pad(cond, pl.multiple_of(start, 8), size)
                start = start + jnp.where(cond, size, 0)
                size *= 2
            used = off_ref[g] + end
        for k in range(N_GROUPS):
            start = used + k * tile_m
            pad(start < xs_ref.shape[0], pl.multiple_of(start, tile_m), tile_m)
        for cond, cp in pads:
            pl.when(cond)(cp.start)
        for cond, cp in pads:
            pl.when(cond)(cp.wait)


def _dispatch(pay, keys, off, counts, *, tile_m):
    n, w = pay.shape
    nt, _, tile = keys.shape
    rows = n + N_GROUPS * tile_m
    return pl.pallas_call(
        functools.partial(_dispatch_kernel, tile_m=tile_m),
        grid_spec=pltpu.PrefetchScalarGridSpec(
            num_scalar_prefetch=2,
            grid=(nt,),
            in_specs=[pl.BlockSpec((1, 1, tile), lambda i, off, cnt: (i, 0, 0)),
                      pl.BlockSpec(memory_space=pl.ANY)],
            out_specs=pl.BlockSpec(memory_space=pl.ANY),
            scratch_shapes=[pltpu.VMEM((DISPATCH_RING, tile, w), F32),
                            pltpu.VMEM((1, tile), jnp.int32), pltpu.SMEM((1, tile), jnp.int32),
                            pltpu.VMEM((tile_m, w), F32), pltpu.SemaphoreType.DMA,
                            pltpu.SemaphoreType.DMA((DISPATCH_RING,)),
                            pltpu.SemaphoreType.DMA((DISPATCH_RING,)),
                            pltpu.SemaphoreType.DMA]),
        out_shape=jax.ShapeDtypeStruct((rows, w), F32),
        compiler_params=pltpu.CompilerParams(dimension_semantics=("arbitrary",),
                                             vmem_limit_bytes=VMEM_LIMIT),
        name="dispatch",
    )(off, counts, keys, pay)


def _group_experts_kernel(grp_ref, xs_ref, weg_ref, weu_ref, wed_ref, y_ref):
    d = y_ref.shape[1]
    grp = grp_ref[pl.program_id(0)]
    h = xs_ref[:, 0:d].astype(BF16)
    gate = xs_ref[:, d:d + LANES]
    lane = lax.broadcasted_iota(jnp.int32, (1, LANES), 1)
    acts = []
    for j in range(EXPERTS_PER_GROUP):
        g = jnp.sum(jnp.where(lane == grp * EXPERTS_PER_GROUP + j, gate, 0.0),
                    axis=-1, keepdims=True)
        act = jax.nn.silu(_dot(h, weg_ref[0, j])) * _dot(h, weu_ref[0, j])
        acts.append((g * act).astype(BF16))
    y_ref[...] = _dot(jnp.concatenate(acts, axis=1), wed_ref[0])


def _group_experts(xs, tile_grp, wts, *, tile_m):
    d = xs.shape[1] - LANES
    n_exp, _, d_exp = wts['w_exp_gate'].shape
    epg = EXPERTS_PER_GROUP
    weg = wts['w_exp_gate'].reshape(N_GROUPS, epg, d, d_exp)
    weu = wts['w_exp_up'].reshape(N_GROUPS, epg, d, d_exp)
    wed = wts['w_exp_down'].reshape(N_GROUPS, epg * d_exp, d)
    return pl.pallas_call(
        _group_experts_kernel,
        grid_spec=pltpu.PrefetchScalarGridSpec(
            num_scalar_prefetch=1,
            grid=(xs.shape[0] // tile_m,),
            in_specs=[pl.BlockSpec((tile_m, d + LANES), lambda i, gr: (i, 0)),
                      pl.BlockSpec((1, epg, d, d_exp), lambda i, gr: (gr[i], 0, 0, 0)),
                      pl.BlockSpec((1, epg, d, d_exp), lambda i, gr: (gr[i], 0, 0, 0)),
                      pl.BlockSpec((1, epg * d_exp, d), lambda i, gr: (gr[i], 0, 0))],
            out_specs=pl.BlockSpec((tile_m, d), lambda i, gr: (i, 0))),
        out_shape=jax.ShapeDtypeStruct((xs.shape[0], d), F32),
        compiler_params=pltpu.CompilerParams(dimension_semantics=("arbitrary",),
                                             vmem_limit_bytes=VMEM_LIMIT),
        name="group_experts",
    )(tile_grp, xs, weg, weu, wed)


def _combine_ple_kernel(off_ref, x1_ref, p_ref, keyc_ref, keyn_ref, lnple_ref, wpg_ref, wpp_ref,
                        y_ref, o_ref, ybuf_ref, posv_ref, pos_smem, sem_y, sem_s):
    i = pl.program_id(0)
    n = pl.num_programs(0)
    t = x1_ref.shape[0]
    slot = i % 2

    def fetch(key_ref, s):
        posv_ref[...] = _row_positions(key_ref[0], off_ref)
        _to_smem(posv_ref, pos_smem.at[pl.ds(s, 1), :], sem_s)

        def gather_row(r, _):
            pltpu.make_async_copy(y_ref.at[pl.ds(pos_smem[s, r], 1), :],
                                  ybuf_ref.at[s, pl.ds(r, 1), :], sem_y.at[s]).start()
            return 0

        lax.fori_loop(0, t, gather_row, 0, unroll=True)

    @pl.when(i == 0)
    def _():
        fetch(keyc_ref, 0)

    @pl.when(i + 1 < n)
    def _():
        fetch(keyn_ref, 1 - slot)

    pltpu.make_async_copy(y_ref.at[pl.ds(0, t), :], ybuf_ref.at[slot], sem_y.at[slot]).wait()
    x2 = x1_ref[...] + ybuf_ref[slot]
    hp = (x2 * _rms_scale(x2) * lnple_ref[...]).astype(BF16)
    gate = jax.nn.sigmoid(_dot(hp, wpg_ref[...]))
    o_ref[...] = x2 + gate * _dot(p_ref[...].astype(BF16), wpp_ref[...])


def _combine_ple(x1, p, keys, off, y, wts):
    n, d = x1.shape
    nt, _, tile = keys.shape
    row = lambda c: pl.BlockSpec((tile, c), lambda i, off: (i, 0))
    consts = [wts['ln_ple'], wts['w_ple_gate'], wts['w_ple_proj']]
    return pl.pallas_call(
        _combine_ple_kernel,
        grid_spec=pltpu.PrefetchScalarGridSpec(
            num_scalar_prefetch=1,
            grid=(nt,),
            in_specs=[row(d), row(p.shape[1]),
                      pl.BlockSpec((1, 1, tile), lambda i, off: (i, 0, 0)),
                      pl.BlockSpec((1, 1, tile),
                                   lambda i, off: (jnp.minimum(i + 1, nt - 1), 0, 0))]
            + [_const_spec(c.shape) for c in consts] + [pl.BlockSpec(memory_space=pl.ANY)],
            out_specs=row(d),
            scratch_shapes=[pltpu.VMEM((2, tile, d), F32), pltpu.VMEM((1, tile), jnp.int32),
                            pltpu.SMEM((2, tile), jnp.int32), pltpu.SemaphoreType.DMA((2,)),
                            pltpu.SemaphoreType.DMA]),
        out_shape=jax.ShapeDtypeStruct((n, d), F32),
        compiler_params=pltpu.CompilerParams(dimension_semantics=("arbitrary",),
                                             vmem_limit_bytes=VMEM_LIMIT),
        name="combine_ple",
    )(off, x1, p, keys, keys, *consts, y)


def _channel_stage(x, attn, sa, mb, p, wts, *, tile, tile_m):
    n = x.shape[0]
    assert n % tile == 0 and n % tile_m == 0
    x1, pay, keys, counts = _merge_route(x, attn, sa, mb, wts, tile=tile)
    ends = jnp.cumsum((counts + tile_m - 1) // tile_m * tile_m)
    off = jnp.concatenate([jnp.zeros((1,), jnp.int32), ends[:-1]])
    xs = _dispatch(pay, keys, off, counts, tile_m=tile_m)
    tile_start = jnp.arange(xs.shape[0] // tile_m, dtype=jnp.int32) * tile_m
    tile_grp = jnp.minimum(jnp.sum(tile_start[:, None] >= ends[None, :], axis=1), N_GROUPS - 1)
    y = _group_experts(xs, tile_grp.astype(jnp.int32), wts, tile_m=tile_m)
    return _combine_ple(x1, p, keys, off, y, wts)


def _layer_weights(i, ln_mix, w_in, q_norm, k_norm, conv_w, w_attn_branch, w_conv_branch, w_out,
                   ln_ffn, w_router_group, b_router_group, w_router_expert, b_router_expert,
                   w_exp_gate, w_exp_up, w_exp_down, ln_ple, w_ple_gate, w_ple_proj):
    aw = ATTN_WIDTH
    pad_cols = lambda a: jnp.pad(a, ((0, 0), (0, LANES - a.shape[1])))
    head_id = jnp.arange(aw) // HEAD_DIM
    k_gain = jnp.tile(k_norm[i], N_HEADS)
    return dict(
        ln_mix=ln_mix[i][None, :], w_in=w_in[i].astype(BF16),
        w_kv_t=w_in[i][:, aw:3 * aw].T.astype(BF16),
        q_norm=jnp.tile(q_norm[i], N_HEADS)[None, :], k_norm=k_gain[None, :],
        k_norm_t=jnp.broadcast_to(k_gain[:, None], (aw, LANES)),
        head_sum=(head_id[:, None] == head_id[None, :]).astype(BF16),
        conv_w=conv_w[i], w_conv_branch=w_conv_branch[i].astype(BF16),
        w_attn_branch=w_attn_branch[i].astype(BF16), w_out=w_out[i].astype(BF16),
        ln_ffn=ln_ffn[i][None, :],
        w_router_group=pad_cols(w_router_group[i]).astype(BF16),
        b_router_group=pad_cols(b_router_group[i][None, :]),
        w_router_expert=pad_cols(w_router_expert[i]).astype(BF16),
        b_router_expert=pad_cols(b_router_expert[i][None, :]),
        w_exp_gate=w_exp_gate[i].astype(BF16), w_exp_up=w_exp_up[i].astype(BF16),
        w_exp_down=w_exp_down[i].astype(BF16),
        ln_ple=ln_ple[i][None, :], w_ple_gate=w_ple_gate[i].astype(BF16),
        w_ple_proj=w_ple_proj[i].astype(BF16))


def _feature_major(a):
    b, s, h, dh = a.shape
    return jnp.transpose(a, (0, 2, 3, 1)).reshape(b, h * dh, s)


def _position_major(a):
    b, _, s = a.shape
    return jnp.transpose(a.reshape(b, N_HEADS, HEAD_DIM, s), (0, 3, 1, 2))


def kernel(x_prompt, x_sample, cache_k, cache_v, state_conv, p_prompt, p_sample, ln_mix, w_in, q_norm, k_norm, conv_w, w_attn_branch, w_conv_branch, w_out, ln_ffn, w_router_group, b_router_group, w_router_expert, b_router_expert, w_exp_gate, w_exp_up, w_exp_down, ln_ple, w_ple_gate, w_ple_proj):
    depth = ln_mix.shape[0]
    nb, seq, d = x_prompt.shape
    nbs, seqs, _ = x_sample.shape
    aw = ATTN_WIDTH
    assert cache_k.shape[3:] == (N_HEADS, HEAD_DIM)
    assert w_router_group.shape[2] == N_GROUPS
    assert w_router_expert.shape[2] == N_GROUPS * EXPERTS_PER_GROUP
    tile = 512
    tri = (jnp.arange(ATTN_BLOCK)[:, None] >= jnp.arange(ATTN_BLOCK)[None, :]).astype(BF16)

    xp = x_prompt.reshape(nb * seq, d)
    xs = x_sample.reshape(nbs * seqs, d)
    outs = [[] for _ in range(6)]
    for i in range(depth):
        wts = _layer_weights(i, ln_mix, w_in, q_norm, k_norm, conv_w, w_attn_branch,
                             w_conv_branch, w_out, ln_ffn, w_router_group, b_router_group,
                             w_router_expert, b_router_expert, w_exp_gate, w_exp_up, w_exp_down,
                             ln_ple, w_ple_gate, w_ple_proj)

        (q, kt, vt, ktb, vtb, sa, mb), conv_new = _input_stage_prompt(xp, wts, seq=seq, tile=tile)
        attn = _attn_prompt(q, ktb, vtb, tri)
        xp = _channel_stage(xp, attn, sa, mb, p_prompt[i].reshape(nb * seq, -1), wts, tile=tile,
                            tile_m=512)
        outs[0].append(_position_major(kt))
        outs[1].append(_position_major(vt))
        outs[2].append(conv_new)

        (q, k, v, kb, vb, sa, mb), conv_new = _input_stage_sample(xs, state_conv[i], wts, seq=seqs)
        attn = _attn_sample(q.reshape(nbs, seqs, aw), kb.reshape(nbs, seqs, aw),
                            vb.reshape(nbs, seqs, aw), _feature_major(cache_k[i]),
                            _feature_major(cache_v[i]), tri)
        xs = _channel_stage(xs, attn.reshape(nbs * seqs, aw), sa, mb,
                            p_sample[i].reshape(nbs * seqs, -1), wts, tile=nbs * seqs,
                            tile_m=min(256, nbs * seqs))
        outs[3].append(k.reshape(nbs, seqs, N_HEADS, HEAD_DIM))
        outs[4].append(v.reshape(nbs, seqs, N_HEADS, HEAD_DIM))
        outs[5].append(conv_new)

    kp, vp, cp, ks, vs, cs = [jnp.stack(o) for o in outs]
    return (xp.reshape(nb, seq, d), xs.reshape(nbs, seqs, d), kp, vp, cp, ks, vs, cs)
```

```python
import functools

import jax
import jax.numpy as jnp
from jax import lax
from jax.experimental import pallas as pl
from jax.experimental.pallas import tpu as pltpu

F32 = jnp.float32
BF16 = jnp.bfloat16

EPS = 1e-6
N_HEADS = 8
HEAD_DIM = 64
ATTN_WIDTH = N_HEADS * HEAD_DIM
N_GROUPS = 4
EXPERTS_PER_GROUP = 4
GROUP_SHIFT = 16
DISPATCH_RING = 3
LANES = 128
ATTN_BLOCK = 256
MASKED = -1e30
SP_CLAMP = 64.0
ATTN_PIPE = 8
ATTN_EAGER = 2
WEIGHT_GONE = 150.0
LOG2E_HI = 1.4426950216293335
LOG2E_LO = 1.925963033500011e-08
VMEM_LIMIT = 56 * 1024 * 1024


def _dot(a, b):
    return jnp.dot(a, b, preferred_element_type=F32)


def _dot_nt(a, b):
    return lax.dot_general(a, b, (((1,), (1,)), ((), ())), preferred_element_type=F32)


def _rms_scale(x):
    return lax.rsqrt(jnp.mean(x * x, axis=-1, keepdims=True) + EPS)


def _const_spec(shape):
    nd = len(shape)
    return pl.BlockSpec(shape, lambda *_: (0,) * nd, pipeline_mode=pl.Buffered(1))


def _head_norm(z, gain, hsum_ref):
    ss = _dot((z * z).astype(BF16), hsum_ref[...])
    return z * lax.rsqrt(ss * (1.0 / HEAD_DIM) + EPS) * gain


def _input_stage_body(x, past1, past2, lnmix_ref, win_ref, qn_ref, hsum_ref, convw_ref, wcb_ref,
                      q_ref, sa_ref, mb_ref):
    aw = ATTN_WIDTH
    cw = convw_ref.shape[1]
    d = x.shape[1]
    h = (x * _rms_scale(x) * lnmix_ref[...]).astype(BF16)

    def proj(lo, width):
        return _dot(h, win_ref[:, lo:lo + width])

    q = _head_norm(proj(0, aw), qn_ref[...], hsum_ref)
    q = q * (HEAD_DIM ** -0.5)
    q_ref[...] = (q * LOG2E_HI + q * LOG2E_LO).astype(BF16)

    cb = proj(3 * aw, cw)
    u = proj(3 * aw + cw, cw) * proj(3 * aw + 2 * cw, cw)
    u1, u2 = past1(u), past2(u)
    conv_y = convw_ref[0:1, :] * u2 + convw_ref[1:2, :] * u1 + convw_ref[2:3, :] * u
    yb = _dot((cb * conv_y).astype(BF16), wcb_ref[...])
    ga = proj(3 * aw + 3 * cw, d)
    gb = proj(3 * aw + 3 * cw + d, d)
    sa_ref[...] = jax.nn.sigmoid(ga).astype(BF16)
    mb_ref[...] = (jax.nn.sigmoid(gb) * yb).astype(BF16)
    return h, proj, u


def _input_stage_prompt_kernel(x_ref, lnmix_ref, win_ref, wkvt_ref, qn_ref, knt_ref, hsum_ref,
                               convw_ref, wcb_ref, q_ref, kt_ref, vt_ref, ktb_ref, vtb_ref,
                               sa_ref, mb_ref, cnew_ref, tail_ref, *, tiles_per_seq):
    t = x_ref.shape[0]
    aw = ATTN_WIDTH

    @pl.when(pl.program_id(0) % tiles_per_seq == 0)
    def _():
        tail_ref[...] = jnp.zeros_like(tail_ref)

    def shifted(u, n):
        ext = jnp.concatenate([tail_ref[...], u], axis=0)
        return pltpu.roll(ext, n, axis=0)[8:, :]

    h, _, u = _input_stage_body(x_ref[...], lambda u: shifted(u, 1), lambda u: shifted(u, 2),
                                lnmix_ref, win_ref, qn_ref, hsum_ref, convw_ref, wcb_ref,
                                q_ref, sa_ref, mb_ref)
    tail_ref[...] = u[t - 8:, :]
    cnew_ref[0] = u[t - 8:, :]

    kt = _dot_nt(wkvt_ref[0:aw, :], h)
    k3 = kt.reshape(N_HEADS, HEAD_DIM, t)
    scale = lax.rsqrt(jnp.mean(k3 * k3, axis=1, keepdims=True) + EPS)
    kt = (k3 * scale).reshape(aw, t) * jnp.tile(knt_ref[...], (1, t // LANES))
    kt_ref[0] = kt
    ktb_ref[0] = kt.astype(BF16)
    vt = _dot_nt(wkvt_ref[aw:2 * aw, :], h)
    vt_ref[0] = vt
    vtb_ref[0] = vt.astype(BF16)


def _input_stage_sample_kernel(x_ref, pe0_ref, pe1_ref, lnmix_ref, win_ref, qn_ref, kn_ref,
                               hsum_ref, convw_ref, wcb_ref, q_ref, k_ref, v_ref, kb_ref,
                               vb_ref, sa_ref, mb_ref, u_ref, *, seq):
    t = x_ref.shape[0]
    aw = ATTN_WIDTH
    pos = lax.broadcasted_iota(jnp.int32, (t, 1), 0) % seq

    def past1(u):
        return jnp.where(pos == 0, pe1_ref[...], pltpu.roll(u, 1, axis=0))

    def past2(u):
        return jnp.where(pos == 0, pe0_ref[...],
                         jnp.where(pos == 1, pe1_ref[...], pltpu.roll(u, 2, axis=0)))

    _, proj, u = _input_stage_body(x_ref[...], past1, past2, lnmix_ref, win_ref, qn_ref,
                                   hsum_ref, convw_ref, wcb_ref, q_ref, sa_ref, mb_ref)
    u_ref[...] = u
    k = _head_norm(proj(aw, aw), kn_ref[...], hsum_ref)
    k_ref[...] = k
    kb_ref[...] = k.astype(BF16)
    v = proj(2 * aw, aw)
    v_ref[...] = v
    vb_ref[...] = v.astype(BF16)


def _input_stage_prompt(x, wts, *, seq, tile):
    n, d = x.shape
    nb = n // seq
    aw = ATTN_WIDTH
    cw = wts['conv_w'].shape[1]
    tiles_per_seq = seq // tile
    row = lambda w: pl.BlockSpec((tile, w), lambda i: (i, 0))
    feat = pl.BlockSpec((1, aw, tile), lambda i: (i // tiles_per_seq, 0, i % tiles_per_seq))
    consts = [wts['ln_mix'], wts['w_in'], wts['w_kv_t'], wts['q_norm'], wts['k_norm_t'],
              wts['head_sum'], wts['conv_w'], wts['w_conv_branch']]
    sds = jax.ShapeDtypeStruct
    outs = pl.pallas_call(
        functools.partial(_input_stage_prompt_kernel, tiles_per_seq=tiles_per_seq),
        grid=(n // tile,),
        in_specs=[row(d)] + [_const_spec(c.shape) for c in consts],
        out_specs=[row(aw), feat, feat, feat, feat, row(d), row(d),
                   pl.BlockSpec((1, 8, cw), lambda i: (i // tiles_per_seq, 0, 0))],
        out_shape=[sds((n, aw), BF16), sds((nb, aw, seq), F32), sds((nb, aw, seq), F32),
                   sds((nb, aw, seq), BF16), sds((nb, aw, seq), BF16), sds((n, d), BF16),
                   sds((n, d), BF16), sds((nb, 8, cw), F32)],
        scratch_shapes=[pltpu.VMEM((8, cw), F32)],
        compiler_params=pltpu.CompilerParams(dimension_semantics=("arbitrary",),
                                             vmem_limit_bytes=VMEM_LIMIT),
        name="input_stage_prompt",
    )(x, *consts)
    return outs[:7], outs[7][:, 6:, :]


def _input_stage_sample(x, state, wts, *, seq):
    n, d = x.shape
    aw = ATTN_WIDTH
    cw = wts['conv_w'].shape[1]
    pe0 = jnp.repeat(state[:, 0, :], seq, axis=0)
    pe1 = jnp.repeat(state[:, 1, :], seq, axis=0)
    consts = [wts['ln_mix'], wts['w_in'], wts['q_norm'], wts['k_norm'], wts['head_sum'],
              wts['conv_w'], wts['w_conv_branch']]
    full = lambda a: pl.BlockSpec(a.shape, lambda i: (0,) * a.ndim)
    sds = jax.ShapeDtypeStruct
    out_shape = [sds((n, aw), BF16), sds((n, aw), F32), sds((n, aw), F32), sds((n, aw), BF16),
                 sds((n, aw), BF16), sds((n, d), BF16), sds((n, d), BF16), sds((n, cw), F32)]
    outs = pl.pallas_call(
        functools.partial(_input_stage_sample_kernel, seq=seq),
        grid=(1,),
        in_specs=[full(x), full(pe0), full(pe1)] + [full(c) for c in consts],
        out_specs=[full(o) for o in out_shape],
        out_shape=out_shape,
        compiler_params=pltpu.CompilerParams(dimension_semantics=("arbitrary",),
                                             vmem_limit_bytes=VMEM_LIMIT),
        name="input_stage_sample",
    )(x, pe0, pe1, *consts)
    conv_new = outs[7].reshape(n // seq, seq, cw)[:, seq - 2:, :]
    return outs[:7], conv_new


def _softplus2(z, keep=None):
    sp = jnp.maximum(z, jnp.log2(1.0 + jnp.exp2(jnp.minimum(z, SP_CLAMP))))
    if keep is not None:
        sp = jnp.where(keep, sp, 0.0)
    return sp.astype(BF16), jnp.sum(sp, axis=-1, keepdims=True)


def _stick_weights(z, sp, carry, tri_ref, keep=None):
    a = jnp.exp2(jnp.minimum(z - _dot(sp, tri_ref[...]), 0.0) - carry)
    if keep is not None:
        a = jnp.where(keep, a, 0.0)
    return a.astype(BF16)


def _stick_block(z, carry, tri_ref, keep):
    sp, row_sum = _softplus2(z, keep)
    return _stick_weights(z, sp, carry, tri_ref, keep), carry + row_sum


def _attn_items(nq):
    items = [(qi, j) for qi in range(nq) for j in range(qi, max(qi - ATTN_EAGER, -1), -1)]
    items = [(0, 0)] * (-len(items) % ATTN_PIPE) + items
    return (jnp.array([i[0] for i in items], jnp.int32),
            jnp.array([i[1] for i in items], jnp.int32))


def _attn_prompt_kernel(iq_ref, ik_ref, q_ref, kt_ref, vt_ref, tri_ref, mask_ref, o_ref,
                        z_ref, sp_ref, rs_ref, carry_ref, acc_ref):
    tq = tri_ref.shape[1]
    nq = q_ref.shape[0] // tq
    n_items = iq_ref.shape[0]
    heads = range(LANES // HEAD_DIM)
    lane_head = lax.broadcasted_iota(jnp.int32, (1, LANES), 1) // HEAD_DIM
    row_head = lax.broadcasted_iota(jnp.int32, (LANES, tq), 0) // HEAD_DIM

    def head_q(q, h):
        return q * (lane_head == h).astype(BF16)

    def weighted_v(a_parts, ks):
        vt = vt_ref[0, :, pl.ds(ks, tq)]
        vts = [jnp.where(row_head == h, vt, jnp.zeros_like(vt)) for h in heads]
        return _dot_nt(jnp.concatenate(a_parts, axis=1), jnp.concatenate(vts, axis=1))

    def blocks(item):
        qi, j = iq_ref[item], ik_ref[item]
        return qi, pl.multiple_of(qi * tq, tq), pl.multiple_of(j * tq, tq), qi == j

    def front(item, slot):
        _, qs, ks, own = blocks(item)
        q = q_ref[pl.ds(qs, tq), :]
        kt = kt_ref[0, :, pl.ds(ks, tq)]
        bias = mask_ref[own.astype(jnp.int32)]
        for h in heads:
            z = _dot(head_q(q, h), kt) + bias
            z_ref[slot, h] = z
            sp_ref[slot, h], rs_ref[slot, h] = _softplus2(z)

    def back(item, slot):
        qi, qs, ks, own = blocks(item)
        a_parts = []
        for h in heads:
            carry = jnp.where(own, 0.0, carry_ref[qi, h])
            a_parts.append(_stick_weights(z_ref[slot, h], sp_ref[slot, h], carry, tri_ref))
            carry_ref[qi, h] = carry + rs_ref[slot, h]
        acc_ref[pl.ds(qs, tq), :] = (jnp.where(own, 0.0, acc_ref[pl.ds(qs, tq), :])
                                     + weighted_v(a_parts, ks))

    depth = ATTN_PIPE // 2
    for s in range(depth):
        front(s, s)

    def trip(k, _):
        for s in range(ATTN_PIPE):
            item = ATTN_PIPE * k + s
            front(jnp.minimum(item + depth, n_items - 1), (s + depth) % ATTN_PIPE)
            back(item, s)
        return 0

    lax.fori_loop(0, n_items // ATTN_PIPE, trip, 0)

    def finish(qi, _):
        qs = pl.multiple_of(qi * tq, tq)

        def more(state):
            j, c0, c1 = state
            return jnp.logical_and(j >= 0, jnp.min(jnp.minimum(c0, c1)) < WEIGHT_GONE)

        def step(state):
            j, *carries = state
            ks = pl.multiple_of(j * tq, tq)
            q = q_ref[pl.ds(qs, tq), :]
            kt = kt_ref[0, :, pl.ds(ks, tq)]
            a_parts = []
            for h in heads:
                a, carries[h] = _stick_block(_dot(head_q(q, h), kt), carries[h], tri_ref, None)
                a_parts.append(a)
            acc_ref[pl.ds(qs, tq), :] += weighted_v(a_parts, ks)
            return (j - 1, *carries)

        lax.while_loop(more, step, (qi - ATTN_EAGER, carry_ref[qi, 0], carry_ref[qi, 1]))
        return 0

    lax.fori_loop(ATTN_EAGER, nq, finish, 0)
    o_ref[...] = acc_ref[...].astype(o_ref.dtype)


def _attn_prompt(q, ktb, vtb, tri):
    n, aw = q.shape
    nb, _, seq = ktb.shape
    tq = ATTN_BLOCK
    row = jnp.arange(tq)[:, None]
    col = jnp.arange(tq)[None, :]
    mask = jnp.stack([jnp.zeros((tq, tq), F32), jnp.where(col < row, 0.0, MASKED).astype(F32)])
    iq, ik = _attn_items(seq // tq)
    qspec = pl.BlockSpec((seq, LANES), lambda b, hp, iq, ik: (b, hp))
    kvspec = pl.BlockSpec((1, LANES, seq), lambda b, hp, iq, ik: (b, hp, 0))
    const = lambda a: pl.BlockSpec(a.shape, lambda b, hp, iq, ik: (0,) * a.ndim)
    return pl.pallas_call(
        _attn_prompt_kernel,
        grid_spec=pltpu.PrefetchScalarGridSpec(
            num_scalar_prefetch=2,
            grid=(nb, aw // LANES),
            in_specs=[qspec, kvspec, kvspec, const(tri), const(mask)],
            out_specs=qspec,
            scratch_shapes=[pltpu.VMEM((ATTN_PIPE, 2, tq, tq), F32),
                            pltpu.VMEM((ATTN_PIPE, 2, tq, tq), BF16),
                            pltpu.VMEM((ATTN_PIPE, 2, tq, 1), F32),
                            pltpu.VMEM((seq // tq, 2, tq, 1), F32),
                            pltpu.VMEM((seq, LANES), F32)]),
        out_shape=jax.ShapeDtypeStruct((n, aw), BF16),
        compiler_params=pltpu.CompilerParams(dimension_semantics=("arbitrary", "arbitrary"),
                                             vmem_limit_bytes=VMEM_LIMIT),
        name="attn_prompt",
    )(iq, ik, q, ktb, vtb, tri, mask)


def _attn_sample_kernel(q_ref, kn_ref, vn_ref, tri_ref, ck_ref, cv_ref, o_ref,
                        kbuf_ref, vbuf_ref, acc_ref, sem):
    b = pl.program_id(0)
    tq, aw = q_ref.shape[1], q_ref.shape[2]
    rows = N_HEADS * tq
    tk = ATTN_BLOCK
    last = ck_ref.shape[2] // tk - 1

    def fetch(j):
        ks = pl.multiple_of(j * tk, tk)
        return (pltpu.make_async_copy(ck_ref.at[b, :, pl.ds(ks, tk)], kbuf_ref, sem.at[0]),
                pltpu.make_async_copy(cv_ref.at[b, :, pl.ds(ks, tk)], vbuf_ref, sem.at[1]))

    for cp in fetch(last):
        cp.start()
    row_head = lax.broadcasted_iota(jnp.int32, (rows, aw), 0) // tq
    lane_head = lax.broadcasted_iota(jnp.int32, (rows, aw), 1) // HEAD_DIM
    qt = jnp.concatenate([q_ref[0]] * N_HEADS, axis=0)
    qrow = jnp.where(row_head == lane_head, qt, jnp.zeros_like(qt))
    pad = jnp.zeros((tk - tq, aw), BF16)
    kn = jnp.concatenate([kn_ref[0], pad], axis=0)
    vn = jnp.concatenate([vn_ref[0], pad], axis=0)
    qpos = lax.broadcasted_iota(jnp.int32, (rows, tk), 0) % tq
    kpos = lax.broadcasted_iota(jnp.int32, (rows, tk), 1)
    a, carry = _stick_block(_dot_nt(qrow, kn), jnp.zeros((rows, 1), F32), tri_ref, kpos < qpos)
    acc_ref[...] = _dot(a, vn)
    for cp in fetch(last):
        cp.wait()

    def step(state):
        j, _, carry = state
        a, carry = _stick_block(_dot(qrow, kbuf_ref[...].astype(BF16)), carry, tri_ref, None)
        acc_ref[...] += _dot_nt(a, vbuf_ref[...].astype(BF16))
        go_on = jnp.logical_and(j > 0, jnp.min(carry) < WEIGHT_GONE)

        @pl.when(go_on)
        def _():
            for cp in fetch(j - 1):
                cp.start()
            for cp in fetch(j - 1):
                cp.wait()

        return j - 1, go_on, carry

    lax.while_loop(lambda state: state[1], step, (jnp.int32(last), True, carry))

    lane_head = lax.broadcasted_iota(jnp.int32, (tq, aw), 1) // HEAD_DIM
    out = jnp.zeros((tq, aw), F32)
    for head in range(N_HEADS):
        out = out + jnp.where(lane_head == head, acc_ref[head * tq:(head + 1) * tq, :], 0.0)
    o_ref[0] = out.astype(o_ref.dtype)


def _attn_sample(q, kn, vn, cache_kt, cache_vt, tri):
    nb, tq, aw = q.shape
    new = pl.BlockSpec((1, tq, aw), lambda b: (b, 0, 0))
    hbm = pl.BlockSpec(memory_space=pl.ANY)
    return pl.pallas_call(
        _attn_sample_kernel,
        grid=(nb,),
        in_specs=[new, new, new, pl.BlockSpec(tri.shape, lambda b: (0, 0)), hbm, hbm],
        out_specs=new,
        out_shape=jax.ShapeDtypeStruct((nb, tq, aw), BF16),
        scratch_shapes=[pltpu.VMEM((aw, ATTN_BLOCK), F32), pltpu.VMEM((aw, ATTN_BLOCK), F32),
                        pltpu.VMEM((N_HEADS * tq, aw), F32), pltpu.SemaphoreType.DMA((2,))],
        compiler_params=pltpu.CompilerParams(dimension_semantics=("arbitrary",),
                                             vmem_limit_bytes=VMEM_LIMIT),
        name="attn_sample",
    )(q, kn, vn, tri, cache_kt, cache_vt)


def _first_max(vals, lane):
    m = jnp.max(vals, axis=-1, keepdims=True)
    idx = jnp.min(jnp.where(vals == m, lane, float(LANES)), axis=-1, keepdims=True)
    return m, idx


def _router_gates(h, wrg_ref, brg_ref, wre_ref, bre_ref):
    lane = lax.broadcasted_iota(jnp.int32, (1, LANES), 1).astype(F32)
    neg = -jnp.inf
    lg = _dot(h, wrg_ref[...]) + brg_ref[...]
    lg = jnp.where(lane < N_GROUPS, lg, neg)
    mg, gidx = _first_max(lg, lane)
    p_sel = 1.0 / jnp.sum(jnp.exp(lg - mg), axis=-1, keepdims=True)
    le = _dot(h, wre_ref[...]) + bre_ref[...]
    in_group = jnp.floor(lane * (1.0 / EXPERTS_PER_GROUP)) == gidx
    le = jnp.where(in_group, le, neg)
    t1, i1 = _first_max(le, lane)
    t2, i2 = _first_max(jnp.where(lane == i1, neg, le), lane)
    e2 = jnp.exp(t2 - t1)
    w1 = 1.0 / (1.0 + e2)
    gate = p_sel * (jnp.where(lane == i1, w1, 0.0) + jnp.where(lane == i2, e2 * w1, 0.0))
    return gate, gidx, i1, i2


def _expert_buckets(by_pair):
    epg = EXPERTS_PER_GROUP
    if not by_pair:
        return [[g * epg + j for j in range(epg)] for g in range(N_GROUPS)]
    return [[g * epg + lo, g * epg + hi] for g in range(N_GROUPS)
            for lo in range(epg) for hi in range(lo + 1, epg)]


def _merge_route_kernel(x_ref, attn_ref, sa_ref, mb_ref, wab_ref, wout_ref, lnffn_ref,
                        wrg_ref, brg_ref, wre_ref, bre_ref, ltri_ref,
                        x1_ref, pay_ref, key_ref, cnt_out_ref, cnt_ref, *, by_pair):
    t, d = x_ref.shape

    @pl.when(pl.program_id(0) == 0)
    def _():
        cnt_ref[...] = jnp.zeros_like(cnt_ref)

    ya = _dot(attn_ref[...], wab_ref[...])
    m = sa_ref[...].astype(F32) * ya + mb_ref[...].astype(F32)
    x1 = x_ref[...] + _dot(m.astype(BF16), wout_ref[...])
    x1_ref[...] = x1
    h = x1 * _rms_scale(x1) * lnffn_ref[...]
    gate, gidx, i1, i2 = _router_gates(h.astype(BF16), wrg_ref, brg_ref, wre_ref, bre_ref)
    pay_ref[:, 0:d] = h
    pay_ref[:, d:d + LANES] = gate
    bucket = gidx
    if by_pair:
        a, b = i1 - gidx * EXPERTS_PER_GROUP, i2 - gidx * EXPERTS_PER_GROUP
        lo, hi = jnp.minimum(a, b), jnp.maximum(a, b)
        pair = jnp.where(lo == 0.0, hi - 1.0, jnp.where(lo == 1.0, hi + 1.0, 5.0))
        bucket = gidx * 6.0 + pair

    lane = lax.broadcasted_iota(jnp.int32, (1, LANES), 1).astype(F32)
    onehot = lane == bucket
    before = _dot(ltri_ref[...], onehot.astype(BF16)) + cnt_ref[...]
    rank = jnp.sum(jnp.where(onehot, before, 0.0), axis=-1, keepdims=True)
    cnt = cnt_ref[...] + jnp.sum(onehot.astype(F32), axis=0, keepdims=True)
    cnt_ref[...] = cnt
    cnt_out_ref[...] = cnt.astype(jnp.int32)
    key = bucket * float(1 << GROUP_SHIFT) + rank
    key_ref[0] = jnp.broadcast_to(key, (t, LANES)).T[0:1, :].astype(jnp.int32)


def _merge_route(x, attn, sa, mb, wts, *, tile, by_pair):
    n, d = x.shape
    n_buckets = len(_expert_buckets(by_pair))
    assert EXPERTS_PER_GROUP == 4 and n_buckets <= LANES
    assert n < (1 << GROUP_SHIFT) and n_buckets << GROUP_SHIFT <= 1 << 24
    row = lambda c: pl.BlockSpec((tile, c), lambda i: (i, 0))
    ltri = (jnp.arange(tile)[:, None] > jnp.arange(tile)[None, :]).astype(BF16)
    consts = [wts['w_attn_branch'], wts['w_out'], wts['ln_ffn'], wts['w_router_group'],
              wts['b_router_group'], wts['w_router_expert'], wts['b_router_expert'], ltri]
    sds = jax.ShapeDtypeStruct
    x1, pay, keys, cnt = pl.pallas_call(
        functools.partial(_merge_route_kernel, by_pair=by_pair),
        grid=(n // tile,),
        in_specs=[row(d), row(attn.shape[1]), row(d), row(d)]
        + [_const_spec(c.shape) for c in consts],
        out_specs=[row(d), row(d + LANES), pl.BlockSpec((1, 1, tile), lambda i: (i, 0, 0)),
                   pl.BlockSpec((1, LANES), lambda i: (0, 0))],
        out_shape=[sds((n, d), F32), sds((n, d + LANES), F32),
                   sds((n // tile, 1, tile), jnp.int32), sds((1, LANES), jnp.int32)],
        scratch_shapes=[pltpu.VMEM((1, LANES), F32)],
        compiler_params=pltpu.CompilerParams(dimension_semantics=("arbitrary",),
                                             vmem_limit_bytes=VMEM_LIMIT),
        name="merge_route",
    )(x, attn, sa, mb, *consts)
    return x1, pay, keys, cnt[0, :n_buckets]


def _row_positions(key_row, off_ref):
    bucket = key_row >> GROUP_SHIFT
    pos = key_row & ((1 << GROUP_SHIFT) - 1)
    for g in range(off_ref.shape[0]):
        pos = pos + jnp.where(bucket == g, off_ref[g], 0)
    return pos


def _to_smem(src_ref, dst_ref, sem):
    cp = pltpu.make_async_copy(src_ref, dst_ref, sem)
    cp.start()
    cp.wait()


def _dispatch_kernel(off_ref, cnt_ref, key_ref, pay_ref, xs_ref, stage_ref, posv_ref, pos_smem,
                     zero_ref, sem_s, sem_in, sem_row, sem_pad, *, tile_m):
    i = pl.program_id(0)
    n = pl.num_programs(0)
    t = key_ref.shape[2]
    ring = stage_ref.shape[0]
    slot = lax.rem(i, ring)
    nxt = lax.rem(i + 1, ring)

    def load(tile, s):
        return pltpu.make_async_copy(pay_ref.at[pl.ds(tile * t, t), :], stage_ref.at[s],
                                     sem_in.at[s])

    def wait_rows(s):
        pltpu.make_async_copy(stage_ref.at[s], xs_ref.at[pl.ds(0, t), :], sem_row.at[s]).wait()

    @pl.when(i == 0)
    def _():
        load(0, 0).start()

    @pl.when(i >= ring - 1)
    def _():
        wait_rows(nxt)

    @pl.when(i + 1 < n)
    def _():
        load(i + 1, nxt).start()

    posv_ref[...] = _row_positions(key_ref[0], off_ref)
    _to_smem(posv_ref, pos_smem, sem_s)
    load(i, slot).wait()

    def copy_row(r, _):
        pltpu.make_async_copy(stage_ref.at[slot, pl.ds(r, 1), :],
                              xs_ref.at[pl.ds(pos_smem[0, r], 1), :], sem_row.at[slot]).start()
        return 0

    lax.fori_loop(0, t, copy_row, 0, unroll=True)

    @pl.when(i == n - 1)
    def _():
        for back in range(ring - 1):
            @pl.when(i >= back)
            def _():
                wait_rows(lax.rem(i - back + ring, ring))
        zero_ref[...] = jnp.zeros_like(zero_ref)
        pads = []

        def pad(cond, start, size):
            pads.append((cond, pltpu.make_async_copy(
                zero_ref.at[pl.ds(0, size), :], xs_ref.at[pl.ds(start, size), :], sem_pad)))

        used = 0
        n_buckets = cnt_ref.shape[0]
        for g in range(n_buckets):
            c = cnt_ref[g]
            c8 = (c + 7) >> 3 << 3
            end = (c + tile_m - 1) // tile_m * tile_m
            for k in range(7):
                pad(c + k < c8, off_ref[g] + c + k, 1)
            start, size = off_ref[g] + c8, 8
            while size < tile_m:
                cond = ((end - c8) & size) != 0
                pad(cond, pl.multiple_of(start, 8), size)
                start = start + jnp.where(cond, size, 0)
                size *= 2
            used = off_ref[g] + end
        for k in range(n_buckets):
            start = used + k * tile_m
            pad(start < xs_ref.shape[0], pl.multiple_of(start, tile_m), tile_m)
        for cond, cp in pads:
            pl.when(cond)(cp.start)
        for cond, cp in pads:
            pl.when(cond)(cp.wait)


def _dispatch(pay, keys, off, counts, *, tile_m):
    n, w = pay.shape
    nt, _, tile = keys.shape
    rows = n + counts.shape[0] * tile_m
    return pl.pallas_call(
        functools.partial(_dispatch_kernel, tile_m=tile_m),
        grid_spec=pltpu.PrefetchScalarGridSpec(
            num_scalar_prefetch=2,
            grid=(nt,),
            in_specs=[pl.BlockSpec((1, 1, tile), lambda i, off, cnt: (i, 0, 0)),
                      pl.BlockSpec(memory_space=pl.ANY)],
            out_specs=pl.BlockSpec(memory_space=pl.ANY),
            scratch_shapes=[pltpu.VMEM((DISPATCH_RING, tile, w), F32),
                            pltpu.VMEM((1, tile), jnp.int32), pltpu.SMEM((1, tile), jnp.int32),
                            pltpu.VMEM((tile_m, w), F32), pltpu.SemaphoreType.DMA,
                            pltpu.SemaphoreType.DMA((DISPATCH_RING,)),
                            pltpu.SemaphoreType.DMA((DISPATCH_RING,)),
                            pltpu.SemaphoreType.DMA]),
        out_shape=jax.ShapeDtypeStruct((rows, w), F32),
        compiler_params=pltpu.CompilerParams(dimension_semantics=("arbitrary",),
                                             vmem_limit_bytes=VMEM_LIMIT),
        name="dispatch",
    )(off, counts, keys, pay)


def _bucket_experts_kernel(exp_ref, xs_ref, *refs):
    *w_refs, y_ref = refs
    k = len(w_refs) // 3
    d = y_ref.shape[1]
    i = pl.program_id(0)
    h = xs_ref[:, 0:d].astype(BF16)
    gate = xs_ref[:, d:d + LANES]
    lane = lax.broadcasted_iota(jnp.int32, (1, LANES), 1)
    y = None
    for j in range(k):
        weg_ref, weu_ref, wed_ref = w_refs[3 * j:3 * j + 3]
        g = jnp.sum(jnp.where(lane == exp_ref[j, i], gate, 0.0), axis=-1, keepdims=True)
        act = jax.nn.silu(_dot(h, weg_ref[0])) * _dot(h, weu_ref[0])
        yj = _dot((g * act).astype(BF16), wed_ref[0])
        y = yj if y is None else y + yj
    y_ref[...] = y


def _bucket_experts(xs, tile_exp, wts, *, tile_m):
    d = xs.shape[1] - LANES
    _, _, d_exp = wts['w_exp_gate'].shape
    k = tile_exp.shape[0]
    w_specs, w_args = [], []
    for j in range(k):
        pick = lambda i, ex, j=j: (ex[j, i], 0, 0)
        w_specs += [pl.BlockSpec((1, d, d_exp), pick), pl.BlockSpec((1, d, d_exp), pick),
                    pl.BlockSpec((1, d_exp, d), pick)]
        w_args += [wts['w_exp_gate'], wts['w_exp_up'], wts['w_exp_down']]
    return pl.pallas_call(
        _bucket_experts_kernel,
        grid_spec=pltpu.PrefetchScalarGridSpec(
            num_scalar_prefetch=1,
            grid=(xs.shape[0] // tile_m,),
            in_specs=[pl.BlockSpec((tile_m, d + LANES), lambda i, ex: (i, 0))] + w_specs,
            out_specs=pl.BlockSpec((tile_m, d), lambda i, ex: (i, 0))),
        out_shape=jax.ShapeDtypeStruct((xs.shape[0], d), F32),
        compiler_params=pltpu.CompilerParams(dimension_semantics=("arbitrary",),
                                             vmem_limit_bytes=VMEM_LIMIT),
        name="bucket_experts",
    )(tile_exp, xs, *w_args)


def _combine_ple_kernel(off_ref, x1_ref, p_ref, keyc_ref, keyn_ref, lnple_ref, wpg_ref, wpp_ref,
                        y_ref, o_ref, ybuf_ref, posv_ref, pos_smem, sem_y, sem_s):
    i = pl.program_id(0)
    n = pl.num_programs(0)
    t = x1_ref.shape[0]
    slot = i % 2

    def fetch(key_ref, s):
        posv_ref[...] = _row_positions(key_ref[0], off_ref)
        _to_smem(posv_ref, pos_smem.at[pl.ds(s, 1), :], sem_s)

        def gather_row(r, _):
            pltpu.make_async_copy(y_ref.at[pl.ds(pos_smem[s, r], 1), :],
                                  ybuf_ref.at[s, pl.ds(r, 1), :], sem_y.at[s]).start()
            return 0

        lax.fori_loop(0, t, gather_row, 0, unroll=True)

    @pl.when(i == 0)
    def _():
        fetch(keyc_ref, 0)

    @pl.when(i + 1 < n)
    def _():
        fetch(keyn_ref, 1 - slot)

    pltpu.make_async_copy(y_ref.at[pl.ds(0, t), :], ybuf_ref.at[slot], sem_y.at[slot]).wait()
    x2 = x1_ref[...] + ybuf_ref[slot]
    hp = (x2 * _rms_scale(x2) * lnple_ref[...]).astype(BF16)
    gate = jax.nn.sigmoid(_dot(hp, wpg_ref[...]))
    o_ref[...] = x2 + gate * _dot(p_ref[...].astype(BF16), wpp_ref[...])


def _combine_ple(x1, p, keys, off, y, wts):
    n, d = x1.shape
    nt, _, tile = keys.shape
    row = lambda c: pl.BlockSpec((tile, c), lambda i, off: (i, 0))
    consts = [wts['ln_ple'], wts['w_ple_gate'], wts['w_ple_proj']]
    return pl.pallas_call(
        _combine_ple_kernel,
        grid_spec=pltpu.PrefetchScalarGridSpec(
            num_scalar_prefetch=1,
            grid=(nt,),
            in_specs=[row(d), row(p.shape[1]),
                      pl.BlockSpec((1, 1, tile), lambda i, off: (i, 0, 0)),
                      pl.BlockSpec((1, 1, tile),
                                   lambda i, off: (jnp.minimum(i + 1, nt - 1), 0, 0))]
            + [_const_spec(c.shape) for c in consts] + [pl.BlockSpec(memory_space=pl.ANY)],
            out_specs=row(d),
            scratch_shapes=[pltpu.VMEM((2, tile, d), F32), pltpu.VMEM((1, tile), jnp.int32),
                            pltpu.SMEM((2, tile), jnp.int32), pltpu.SemaphoreType.DMA((2,)),
                            pltpu.SemaphoreType.DMA]),
        out_shape=jax.ShapeDtypeStruct((n, d), F32),
        compiler_params=pltpu.CompilerParams(dimension_semantics=("arbitrary",),
                                             vmem_limit_bytes=VMEM_LIMIT),
        name="combine_ple",
    )(off, x1, p, keys, keys, *consts, y)


def _channel_stage(x, attn, sa, mb, p, wts, *, tile, tile_m, by_pair):
    n = x.shape[0]
    assert n % tile == 0 and n % tile_m == 0
    x1, pay, keys, counts = _merge_route(x, attn, sa, mb, wts, tile=tile, by_pair=by_pair)
    ends = jnp.cumsum((counts + tile_m - 1) // tile_m * tile_m)
    off = jnp.concatenate([jnp.zeros((1,), jnp.int32), ends[:-1]])
    xs = _dispatch(pay, keys, off, counts, tile_m=tile_m)
    tile_start = jnp.arange(xs.shape[0] // tile_m, dtype=jnp.int32) * tile_m
    tile_bucket = jnp.minimum(jnp.sum(tile_start[:, None] >= ends[None, :], axis=1),
                              counts.shape[0] - 1)
    tile_exp = jnp.array(_expert_buckets(by_pair), jnp.int32)[tile_bucket].T
    y = _bucket_experts(xs, tile_exp, wts, tile_m=tile_m)
    return _combine_ple(x1, p, keys, off, y, wts)


def _layer_weights(i, ln_mix, w_in, q_norm, k_norm, conv_w, w_attn_branch, w_conv_branch, w_out,
                   ln_ffn, w_router_group, b_router_group, w_router_expert, b_router_expert,
                   w_exp_gate, w_exp_up, w_exp_down, ln_ple, w_ple_gate, w_ple_proj):
    aw = ATTN_WIDTH
    pad_cols = lambda a: jnp.pad(a, ((0, 0), (0, LANES - a.shape[1])))
    head_id = jnp.arange(aw) // HEAD_DIM
    k_gain = jnp.tile(k_norm[i], N_HEADS)
    return dict(
        ln_mix=ln_mix[i][None, :], w_in=w_in[i].astype(BF16),
        w_kv_t=w_in[i][:, aw:3 * aw].T.astype(BF16),
        q_norm=jnp.tile(q_norm[i], N_HEADS)[None, :], k_norm=k_gain[None, :],
        k_norm_t=jnp.broadcast_to(k_gain[:, None], (aw, LANES)),
        head_sum=(head_id[:, None] == head_id[None, :]).astype(BF16),
        conv_w=conv_w[i], w_conv_branch=w_conv_branch[i].astype(BF16),
        w_attn_branch=w_attn_branch[i].astype(BF16), w_out=w_out[i].astype(BF16),
        ln_ffn=ln_ffn[i][None, :],
        w_router_group=pad_cols(w_router_group[i]).astype(BF16),
        b_router_group=pad_cols(b_router_group[i][None, :]),
        w_router_expert=pad_cols(w_router_expert[i]).astype(BF16),
        b_router_expert=pad_cols(b_router_expert[i][None, :]),
        w_exp_gate=w_exp_gate[i].astype(BF16), w_exp_up=w_exp_up[i].astype(BF16),
        w_exp_down=w_exp_down[i].astype(BF16),
        ln_ple=ln_ple[i][None, :], w_ple_gate=w_ple_gate[i].astype(BF16),
        w_ple_proj=w_ple_proj[i].astype(BF16))


def _feature_major(a):
    b, s, h, dh = a.shape
    return jnp.transpose(a, (0, 2, 3, 1)).reshape(b, h * dh, s)


def _position_major(a):
    b, _, s = a.shape
    return jnp.transpose(a.reshape(b, N_HEADS, HEAD_DIM, s), (0, 3, 1, 2))


def kernel(x_prompt, x_sample, cache_k, cache_v, state_conv, p_prompt, p_sample, ln_mix, w_in, q_norm, k_norm, conv_w, w_attn_branch, w_conv_branch, w_out, ln_ffn, w_router_group, b_router_group, w_router_expert, b_router_expert, w_exp_gate, w_exp_up, w_exp_down, ln_ple, w_ple_gate, w_ple_proj):
    depth = ln_mix.shape[0]
    nb, seq, d = x_prompt.shape
    nbs, seqs, _ = x_sample.shape
    aw = ATTN_WIDTH
    assert cache_k.shape[3:] == (N_HEADS, HEAD_DIM)
    assert w_router_group.shape[2] == N_GROUPS
    assert w_router_expert.shape[2] == N_GROUPS * EXPERTS_PER_GROUP
    tile = 512
    tri = (jnp.arange(ATTN_BLOCK)[:, None] >= jnp.arange(ATTN_BLOCK)[None, :]).astype(BF16)

    xp = x_prompt.reshape(nb * seq, d)
    xs = x_sample.reshape(nbs * seqs, d)
    outs = [[] for _ in range(6)]
    for i in range(depth):
        wts = _layer_weights(i, ln_mix, w_in, q_norm, k_norm, conv_w, w_attn_branch,
                             w_conv_branch, w_out, ln_ffn, w_router_group, b_router_group,
                             w_router_expert, b_router_expert, w_exp_gate, w_exp_up, w_exp_down,
                             ln_ple, w_ple_gate, w_ple_proj)

        (q, kt, vt, ktb, vtb, sa, mb), conv_new = _input_stage_prompt(xp, wts, seq=seq, tile=tile)
        attn = _attn_prompt(q, ktb, vtb, tri)
        xp = _channel_stage(xp, attn, sa, mb, p_prompt[i].reshape(nb * seq, -1), wts, tile=tile,
                            tile_m=256, by_pair=True)
        outs[0].append(_position_major(kt))
        outs[1].append(_position_major(vt))
        outs[2].append(conv_new)

        (q, k, v, kb, vb, sa, mb), conv_new = _input_stage_sample(xs, state_conv[i], wts, seq=seqs)
        attn = _attn_sample(q.reshape(nbs, seqs, aw), kb.reshape(nbs, seqs, aw),
                            vb.reshape(nbs, seqs, aw), _feature_major(cache_k[i]),
                            _feature_major(cache_v[i]), tri)
        xs = _channel_stage(xs, attn.reshape(nbs * seqs, aw), sa, mb,
                            p_sample[i].reshape(nbs * seqs, -1), wts, tile=nbs * seqs,
                            tile_m=min(256, nbs * seqs), by_pair=False)
        outs[3].append(k.reshape(nbs, seqs, N_HEADS, HEAD_DIM))
        outs[4].append(v.reshape(nbs, seqs, N_HEADS, HEAD_DIM))
        outs[5].append(conv_new)

    kp, vp, cp, ks, vs, cs = [jnp.stack(o) for o in outs]
    return (xp.reshape(nb, seq, d), xs.reshape(nbs, seqs, d), kp, vp, cp, ks, vs, cs)
```

```python
import functools

import jax
import jax.numpy as jnp
from jax import lax
from jax.experimental import pallas as pl
from jax.experimental.pallas import tpu as pltpu

F32 = jnp.float32
BF16 = jnp.bfloat16

EPS = 1e-6
N_HEADS = 8
HEAD_DIM = 64
ATTN_WIDTH = N_HEADS * HEAD_DIM
N_GROUPS = 4
EXPERTS_PER_GROUP = 4
GROUP_SHIFT = 16
DISPATCH_RING = 3
LANES = 128
ATTN_BLOCK = 256
MASKED = -1e30
SP_CLAMP = 64.0
ATTN_PIPE = 8
ATTN_EAGER = 2
WEIGHT_GONE = 150.0
LOG2E_HI = 1.4426950216293335
LOG2E_LO = 1.925963033500011e-08
VMEM_LIMIT = 56 * 1024 * 1024


def _dot(a, b):
    return jnp.dot(a, b, preferred_element_type=F32)


def _dot_nt(a, b):
    return lax.dot_general(a, b, (((1,), (1,)), ((), ())), preferred_element_type=F32)


def _rms_scale(x):
    return lax.rsqrt(jnp.mean(x * x, axis=-1, keepdims=True) + EPS)


def _const_spec(shape):
    nd = len(shape)
    return pl.BlockSpec(shape, lambda *_: (0,) * nd, pipeline_mode=pl.Buffered(1))


def _head_norm(z, gain, hsum_ref):
    ss = _dot((z * z).astype(BF16), hsum_ref[...])
    return z * lax.rsqrt(ss * (1.0 / HEAD_DIM) + EPS) * gain


def _input_stage_body(x, past1, past2, lnmix_ref, win_ref, qn_ref, hsum_ref, convw_ref, wcb_ref,
                      q_ref, sa_ref, mb_ref):
    aw = ATTN_WIDTH
    cw = convw_ref.shape[1]
    d = x.shape[1]
    h = (x * _rms_scale(x) * lnmix_ref[...]).astype(BF16)

    def proj(lo, width):
        return _dot(h, win_ref[:, lo:lo + width])

    q = _head_norm(proj(0, aw), qn_ref[...], hsum_ref)
    q = q * (HEAD_DIM ** -0.5)
    q_ref[...] = (q * LOG2E_HI + q * LOG2E_LO).astype(BF16)

    cb = proj(3 * aw, cw)
    u = proj(3 * aw + cw, cw) * proj(3 * aw + 2 * cw, cw)
    u1, u2 = past1(u), past2(u)
    conv_y = convw_ref[0:1, :] * u2 + convw_ref[1:2, :] * u1 + convw_ref[2:3, :] * u
    yb = _dot((cb * conv_y).astype(BF16), wcb_ref[...])
    ga = proj(3 * aw + 3 * cw, d)
    gb = proj(3 * aw + 3 * cw + d, d)
    sa_ref[...] = jax.nn.sigmoid(ga).astype(BF16)
    mb_ref[...] = (jax.nn.sigmoid(gb) * yb).astype(BF16)
    return h, proj, u


def _input_stage_prompt_kernel(x_ref, lnmix_ref, win_ref, wkvt_ref, qn_ref, knt_ref, hsum_ref,
                               convw_ref, wcb_ref, q_ref, kt_ref, vt_ref, ktb_ref, vtb_ref,
                               sa_ref, mb_ref, cnew_ref, tail_ref, *, tiles_per_seq):
    t = x_ref.shape[0]
    aw = ATTN_WIDTH

    @pl.when(pl.program_id(0) % tiles_per_seq == 0)
    def _():
        tail_ref[...] = jnp.zeros_like(tail_ref)

    def shifted(u, n):
        ext = jnp.concatenate([tail_ref[...], u], axis=0)
        return pltpu.roll(ext, n, axis=0)[8:, :]

    h, _, u = _input_stage_body(x_ref[...], lambda u: shifted(u, 1), lambda u: shifted(u, 2),
                                lnmix_ref, win_ref, qn_ref, hsum_ref, convw_ref, wcb_ref,
                                q_ref, sa_ref, mb_ref)
    tail_ref[...] = u[t - 8:, :]
    cnew_ref[0] = u[t - 8:, :]

    kt = _dot_nt(wkvt_ref[0:aw, :], h)
    k3 = kt.reshape(N_HEADS, HEAD_DIM, t)
    scale = lax.rsqrt(jnp.mean(k3 * k3, axis=1, keepdims=True) + EPS)
    kt = (k3 * scale).reshape(aw, t) * jnp.tile(knt_ref[...], (1, t // LANES))
    kt_ref[0] = kt
    ktb_ref[0] = kt.astype(BF16)
    vt = _dot_nt(wkvt_ref[aw:2 * aw, :], h)
    vt_ref[0] = vt
    vtb_ref[0] = vt.astype(BF16)


def _input_stage_sample_kernel(x_ref, pe0_ref, pe1_ref, lnmix_ref, win_ref, qn_ref, kn_ref,
                               hsum_ref, convw_ref, wcb_ref, q_ref, k_ref, v_ref, kb_ref,
                               vb_ref, sa_ref, mb_ref, u_ref, *, seq):
    t = x_ref.shape[0]
    aw = ATTN_WIDTH
    pos = lax.broadcasted_iota(jnp.int32, (t, 1), 0) % seq

    def past1(u):
        return jnp.where(pos == 0, pe1_ref[...], pltpu.roll(u, 1, axis=0))

    def past2(u):
        return jnp.where(pos == 0, pe0_ref[...],
                         jnp.where(pos == 1, pe1_ref[...], pltpu.roll(u, 2, axis=0)))

    _, proj, u = _input_stage_body(x_ref[...], past1, past2, lnmix_ref, win_ref, qn_ref,
                                   hsum_ref, convw_ref, wcb_ref, q_ref, sa_ref, mb_ref)
    u_ref[...] = u
    k = _head_norm(proj(aw, aw), kn_ref[...], hsum_ref)
    k_ref[...] = k
    kb_ref[...] = k.astype(BF16)
    v = proj(2 * aw, aw)
    v_ref[...] = v
    vb_ref[...] = v.astype(BF16)


def _input_stage_prompt(x, wts, *, seq, tile):
    n, d = x.shape
    nb = n // seq
    aw = ATTN_WIDTH
    cw = wts['conv_w'].shape[1]
    tiles_per_seq = seq // tile
    row = lambda w: pl.BlockSpec((tile, w), lambda i: (i, 0))
    feat = pl.BlockSpec((1, aw, tile), lambda i: (i // tiles_per_seq, 0, i % tiles_per_seq))
    consts = [wts['ln_mix'], wts['w_in'], wts['w_kv_t'], wts['q_norm'], wts['k_norm_t'],
              wts['head_sum'], wts['conv_w'], wts['w_conv_branch']]
    sds = jax.ShapeDtypeStruct
    outs = pl.pallas_call(
        functools.partial(_input_stage_prompt_kernel, tiles_per_seq=tiles_per_seq),
        grid=(n // tile,),
        in_specs=[row(d)] + [_const_spec(c.shape) for c in consts],
        out_specs=[row(aw), feat, feat, feat, feat, row(d), row(d),
                   pl.BlockSpec((1, 8, cw), lambda i: (i // tiles_per_seq, 0, 0))],
        out_shape=[sds((n, aw), BF16), sds((nb, aw, seq), F32), sds((nb, aw, seq), F32),
                   sds((nb, aw, seq), BF16), sds((nb, aw, seq), BF16), sds((n, d), BF16),
                   sds((n, d), BF16), sds((nb, 8, cw), F32)],
        scratch_shapes=[pltpu.VMEM((8, cw), F32)],
        compiler_params=pltpu.CompilerParams(dimension_semantics=("arbitrary",),
                                             vmem_limit_bytes=VMEM_LIMIT),
        name="input_stage_prompt",
    )(x, *consts)
    return outs[:7], outs[7][:, 6:, :]


def _input_stage_sample(x, state, wts, *, seq):
    n, d = x.shape
    aw = ATTN_WIDTH
    cw = wts['conv_w'].shape[1]
    pe0 = jnp.repeat(state[:, 0, :], seq, axis=0)
    pe1 = jnp.repeat(state[:, 1, :], seq, axis=0)
    consts = [wts['ln_mix'], wts['w_in'], wts['q_norm'], wts['k_norm'], wts['head_sum'],
              wts['conv_w'], wts['w_conv_branch']]
    full = lambda a: pl.BlockSpec(a.shape, lambda i: (0,) * a.ndim)
    sds = jax.ShapeDtypeStruct
    out_shape = [sds((n, aw), BF16), sds((n, aw), F32), sds((n, aw), F32), sds((n, aw), BF16),
                 sds((n, aw), BF16), sds((n, d), BF16), sds((n, d), BF16), sds((n, cw), F32)]
    outs = pl.pallas_call(
        functools.partial(_input_stage_sample_kernel, seq=seq),
        grid=(1,),
        in_specs=[full(x), full(pe0), full(pe1)] + [full(c) for c in consts],
        out_specs=[full(o) for o in out_shape],
        out_shape=out_shape,
        compiler_params=pltpu.CompilerParams(dimension_semantics=("arbitrary",),
                                             vmem_limit_bytes=VMEM_LIMIT),
        name="input_stage_sample",
    )(x, pe0, pe1, *consts)
    conv_new = outs[7].reshape(n // seq, seq, cw)[:, seq - 2:, :]
    return outs[:7], conv_new


def _softplus2(z, keep=None):
    sp = jnp.maximum(z, jnp.log2(1.0 + jnp.exp2(jnp.minimum(z, SP_CLAMP))))
    if keep is not None:
        sp = jnp.where(keep, sp, 0.0)
    return sp.astype(BF16), jnp.sum(sp, axis=-1, keepdims=True)


def _stick_weights(z, sp, carry, tri_ref, keep=None):
    a = jnp.exp2(jnp.minimum(z - _dot(sp, tri_ref[...]), 0.0) - carry)
    if keep is not None:
        a = jnp.where(keep, a, 0.0)
    return a.astype(BF16)


def _stick_block(z, carry, tri_ref, keep):
    sp, row_sum = _softplus2(z, keep)
    return _stick_weights(z, sp, carry, tri_ref, keep), carry + row_sum


def _attn_items(nq):
    items = [(qi, j) for qi in range(nq) for j in range(qi, max(qi - ATTN_EAGER, -1), -1)]
    items = [(0, 0)] * (-len(items) % ATTN_PIPE) + items
    return (jnp.array([i[0] for i in items], jnp.int32),
            jnp.array([i[1] for i in items], jnp.int32))


def _attn_prompt_kernel(iq_ref, ik_ref, q_ref, kt_ref, vt_ref, tri_ref, mask_ref, o_ref,
                        z_ref, sp_ref, rs_ref, carry_ref, acc_ref):
    tq = tri_ref.shape[1]
    nq = q_ref.shape[0] // tq
    n_items = iq_ref.shape[0]
    heads = range(LANES // HEAD_DIM)
    lane_head = lax.broadcasted_iota(jnp.int32, (1, LANES), 1) // HEAD_DIM
    row_head = lax.broadcasted_iota(jnp.int32, (LANES, tq), 0) // HEAD_DIM

    def head_q(q, h):
        return q * (lane_head == h).astype(BF16)

    def weighted_v(a_parts, ks):
        vt = vt_ref[0, :, pl.ds(ks, tq)]
        vts = [jnp.where(row_head == h, vt, jnp.zeros_like(vt)) for h in heads]
        return _dot_nt(jnp.concatenate(a_parts, axis=1), jnp.concatenate(vts, axis=1))

    def blocks(item):
        qi, j = iq_ref[item], ik_ref[item]
        return qi, pl.multiple_of(qi * tq, tq), pl.multiple_of(j * tq, tq), qi == j

    def front(item, slot):
        _, qs, ks, own = blocks(item)
        q = q_ref[pl.ds(qs, tq), :]
        kt = kt_ref[0, :, pl.ds(ks, tq)]
        bias = mask_ref[own.astype(jnp.int32)]
        for h in heads:
            z = _dot(head_q(q, h), kt) + bias
            z_ref[slot, h] = z
            sp_ref[slot, h], rs_ref[slot, h] = _softplus2(z)

    def back(item, slot):
        qi, qs, ks, own = blocks(item)
        a_parts = []
        for h in heads:
            carry = jnp.where(own, 0.0, carry_ref[qi, h])
            a_parts.append(_stick_weights(z_ref[slot, h], sp_ref[slot, h], carry, tri_ref))
            carry_ref[qi, h] = carry + rs_ref[slot, h]
        acc_ref[pl.ds(qs, tq), :] = (jnp.where(own, 0.0, acc_ref[pl.ds(qs, tq), :])
                                     + weighted_v(a_parts, ks))

    carry_ref[...] = jnp.zeros_like(carry_ref)
    acc_ref[...] = jnp.zeros_like(acc_ref)
    depth = ATTN_PIPE // 2
    for s in range(depth):
        front(s, s)

    def trip(k, _):
        for s in range(ATTN_PIPE):
            item = ATTN_PIPE * k + s
            front(jnp.minimum(item + depth, n_items - 1), (s + depth) % ATTN_PIPE)
            back(item, s)
        return 0

    lax.fori_loop(0, n_items // ATTN_PIPE, trip, 0)

    def finish(qi, _):
        qs = pl.multiple_of(qi * tq, tq)

        def more(state):
            j, c0, c1 = state
            return jnp.logical_and(j >= 0, jnp.min(jnp.minimum(c0, c1)) < WEIGHT_GONE)

        def step(state):
            j, *carries = state
            ks = pl.multiple_of(j * tq, tq)
            q = q_ref[pl.ds(qs, tq), :]
            kt = kt_ref[0, :, pl.ds(ks, tq)]
            a_parts = []
            for h in heads:
                a, carries[h] = _stick_block(_dot(head_q(q, h), kt), carries[h], tri_ref, None)
                a_parts.append(a)
            acc_ref[pl.ds(qs, tq), :] += weighted_v(a_parts, ks)
            return (j - 1, *carries)

        lax.while_loop(more, step, (qi - ATTN_EAGER, carry_ref[qi, 0], carry_ref[qi, 1]))
        return 0

    if nq > ATTN_EAGER:
        @pl.when(jnp.min(carry_ref[ATTN_EAGER:]) < WEIGHT_GONE)
        def _():
            lax.fori_loop(ATTN_EAGER, nq, finish, 0)

    o_ref[...] = acc_ref[...].astype(o_ref.dtype)


def _attn_prompt(q, ktb, vtb, tri):
    n, aw = q.shape
    nb, _, seq = ktb.shape
    tq = ATTN_BLOCK
    row = jnp.arange(tq)[:, None]
    col = jnp.arange(tq)[None, :]
    mask = jnp.stack([jnp.zeros((tq, tq), F32), jnp.where(col < row, 0.0, MASKED).astype(F32)])
    iq, ik = _attn_items(seq // tq)
    qspec = pl.BlockSpec((seq, LANES), lambda b, hp, iq, ik: (b, hp))
    kvspec = pl.BlockSpec((1, LANES, seq), lambda b, hp, iq, ik: (b, hp, 0))
    const = lambda a: pl.BlockSpec(a.shape, lambda b, hp, iq, ik: (0,) * a.ndim)
    return pl.pallas_call(
        _attn_prompt_kernel,
        grid_spec=pltpu.PrefetchScalarGridSpec(
            num_scalar_prefetch=2,
            grid=(nb, aw // LANES),
            in_specs=[qspec, kvspec, kvspec, const(tri), const(mask)],
            out_specs=qspec,
            scratch_shapes=[pltpu.VMEM((ATTN_PIPE, 2, tq, tq), F32),
                            pltpu.VMEM((ATTN_PIPE, 2, tq, tq), BF16),
                            pltpu.VMEM((ATTN_PIPE, 2, tq, 1), F32),
                            pltpu.VMEM((seq // tq, 2, tq, 1), F32),
                            pltpu.VMEM((seq, LANES), F32)]),
        out_shape=jax.ShapeDtypeStruct((n, aw), BF16),
        compiler_params=pltpu.CompilerParams(dimension_semantics=("arbitrary", "arbitrary"),
                                             vmem_limit_bytes=VMEM_LIMIT),
        name="attn_prompt",
    )(iq, ik, q, ktb, vtb, tri, mask)


def _attn_sample_kernel(q_ref, kn_ref, vn_ref, tri_ref, ck_ref, cv_ref, o_ref,
                        kbuf_ref, vbuf_ref, acc_ref, sem):
    b = pl.program_id(0)
    tq, aw = q_ref.shape[1], q_ref.shape[2]
    rows = N_HEADS * tq
    tk = ATTN_BLOCK
    last = ck_ref.shape[2] // tk - 1

    def fetch(j):
        ks = pl.multiple_of(j * tk, tk)
        return (pltpu.make_async_copy(ck_ref.at[b, :, pl.ds(ks, tk)], kbuf_ref, sem.at[0]),
                pltpu.make_async_copy(cv_ref.at[b, :, pl.ds(ks, tk)], vbuf_ref, sem.at[1]))

    for cp in fetch(last):
        cp.start()
    row_head = lax.broadcasted_iota(jnp.int32, (rows, aw), 0) // tq
    lane_head = lax.broadcasted_iota(jnp.int32, (rows, aw), 1) // HEAD_DIM
    qt = jnp.concatenate([q_ref[0]] * N_HEADS, axis=0)
    qrow = jnp.where(row_head == lane_head, qt, jnp.zeros_like(qt))
    pad = jnp.zeros((tk - tq, aw), BF16)
    kn = jnp.concatenate([kn_ref[0], pad], axis=0)
    vn = jnp.concatenate([vn_ref[0], pad], axis=0)
    qpos = lax.broadcasted_iota(jnp.int32, (rows, tk), 0) % tq
    kpos = lax.broadcasted_iota(jnp.int32, (rows, tk), 1)
    a, carry = _stick_block(_dot_nt(qrow, kn), jnp.zeros((rows, 1), F32), tri_ref, kpos < qpos)
    acc_ref[...] = _dot(a, vn)
    for cp in fetch(last):
        cp.wait()

    def step(state):
        j, _, carry = state
        a, carry = _stick_block(_dot(qrow, kbuf_ref[...].astype(BF16)), carry, tri_ref, None)
        acc_ref[...] += _dot_nt(a, vbuf_ref[...].astype(BF16))
        go_on = jnp.logical_and(j > 0, jnp.min(carry) < WEIGHT_GONE)

        @pl.when(go_on)
        def _():
            for cp in fetch(j - 1):
                cp.start()
            for cp in fetch(j - 1):
                cp.wait()

        return j - 1, go_on, carry

    lax.while_loop(lambda state: state[1], step, (jnp.int32(last), True, carry))

    lane_head = lax.broadcasted_iota(jnp.int32, (tq, aw), 1) // HEAD_DIM
    out = jnp.zeros((tq, aw), F32)
    for head in range(N_HEADS):
        out = out + jnp.where(lane_head == head, acc_ref[head * tq:(head + 1) * tq, :], 0.0)
    o_ref[0] = out.astype(o_ref.dtype)


def _attn_sample(q, kn, vn, cache_kt, cache_vt, tri):
    nb, tq, aw = q.shape
    new = pl.BlockSpec((1, tq, aw), lambda b: (b, 0, 0))
    hbm = pl.BlockSpec(memory_space=pl.ANY)
    return pl.pallas_call(
        _attn_sample_kernel,
        grid=(nb,),
        in_specs=[new, new, new, pl.BlockSpec(tri.shape, lambda b: (0, 0)), hbm, hbm],
        out_specs=new,
        out_shape=jax.ShapeDtypeStruct((nb, tq, aw), BF16),
        scratch_shapes=[pltpu.VMEM((aw, ATTN_BLOCK), F32), pltpu.VMEM((aw, ATTN_BLOCK), F32),
                        pltpu.VMEM((N_HEADS * tq, aw), F32), pltpu.SemaphoreType.DMA((2,))],
        compiler_params=pltpu.CompilerParams(dimension_semantics=("arbitrary",),
                                             vmem_limit_bytes=VMEM_LIMIT),
        name="attn_sample",
    )(q, kn, vn, tri, cache_kt, cache_vt)


def _first_max(vals, lane):
    m = jnp.max(vals, axis=-1, keepdims=True)
    idx = jnp.min(jnp.where(vals == m, lane, float(LANES)), axis=-1, keepdims=True)
    return m, idx


def _router_gates(h, wrg_ref, brg_ref, wre_ref, bre_ref):
    lane = lax.broadcasted_iota(jnp.int32, (1, LANES), 1).astype(F32)
    neg = -jnp.inf
    lg = _dot(h, wrg_ref[...]) + brg_ref[...]
    lg = jnp.where(lane < N_GROUPS, lg, neg)
    mg, gidx = _first_max(lg, lane)
    p_sel = 1.0 / jnp.sum(jnp.exp(lg - mg), axis=-1, keepdims=True)
    le = _dot(h, wre_ref[...]) + bre_ref[...]
    in_group = jnp.floor(lane * (1.0 / EXPERTS_PER_GROUP)) == gidx
    le = jnp.where(in_group, le, neg)
    t1, i1 = _first_max(le, lane)
    t2, i2 = _first_max(jnp.where(lane == i1, neg, le), lane)
    e2 = jnp.exp(t2 - t1)
    w1 = 1.0 / (1.0 + e2)
    gate = p_sel * (jnp.where(lane == i1, w1, 0.0) + jnp.where(lane == i2, e2 * w1, 0.0))
    return gate, gidx, i1, i2


def _expert_buckets(by_pair):
    epg = EXPERTS_PER_GROUP
    if not by_pair:
        return [[g * epg + j for j in range(epg)] for g in range(N_GROUPS)]
    return [[g * epg + lo, g * epg + hi] for g in range(N_GROUPS)
            for lo in range(epg) for hi in range(lo + 1, epg)]


def _merge_route_kernel(x_ref, attn_ref, sa_ref, mb_ref, wab_ref, wout_ref, lnffn_ref,
                        wrg_ref, brg_ref, wre_ref, bre_ref, ltri_ref,
                        x1_ref, pay_ref, key_ref, cnt_out_ref, cnt_ref, *, by_pair):
    t, d = x_ref.shape

    @pl.when(pl.program_id(0) == 0)
    def _():
        cnt_ref[...] = jnp.zeros_like(cnt_ref)

    ya = _dot(attn_ref[...], wab_ref[...])
    m = sa_ref[...].astype(F32) * ya + mb_ref[...].astype(F32)
    x1 = x_ref[...] + _dot(m.astype(BF16), wout_ref[...])
    x1_ref[...] = x1
    h = x1 * _rms_scale(x1) * lnffn_ref[...]
    gate, gidx, i1, i2 = _router_gates(h.astype(BF16), wrg_ref, brg_ref, wre_ref, bre_ref)
    pay_ref[:, 0:d] = h
    pay_ref[:, d:d + LANES] = gate
    bucket = gidx
    if by_pair:
        a, b = i1 - gidx * EXPERTS_PER_GROUP, i2 - gidx * EXPERTS_PER_GROUP
        lo, hi = jnp.minimum(a, b), jnp.maximum(a, b)
        pair = jnp.where(lo == 0.0, hi - 1.0, jnp.where(lo == 1.0, hi + 1.0, 5.0))
        bucket = gidx * 6.0 + pair

    lane = lax.broadcasted_iota(jnp.int32, (1, LANES), 1).astype(F32)
    onehot = lane == bucket
    before = _dot(ltri_ref[...], onehot.astype(BF16)) + cnt_ref[...]
    rank = jnp.sum(jnp.where(onehot, before, 0.0), axis=-1, keepdims=True)
    cnt = cnt_ref[...] + jnp.sum(onehot.astype(F32), axis=0, keepdims=True)
    cnt_ref[...] = cnt
    cnt_out_ref[...] = cnt.astype(jnp.int32)
    key = bucket * float(1 << GROUP_SHIFT) + rank
    key_ref[0] = jnp.broadcast_to(key, (t, LANES)).T[0:1, :].astype(jnp.int32)


def _merge_route(x, attn, sa, mb, wts, *, tile, by_pair):
    n, d = x.shape
    n_buckets = len(_expert_buckets(by_pair))
    assert EXPERTS_PER_GROUP == 4 and n_buckets <= LANES
    assert n < (1 << GROUP_SHIFT) and n_buckets << GROUP_SHIFT <= 1 << 24
    row = lambda c: pl.BlockSpec((tile, c), lambda i: (i, 0))
    ltri = (jnp.arange(tile)[:, None] > jnp.arange(tile)[None, :]).astype(BF16)
    consts = [wts['w_attn_branch'], wts['w_out'], wts['ln_ffn'], wts['w_router_group'],
              wts['b_router_group'], wts['w_router_expert'], wts['b_router_expert'], ltri]
    sds = jax.ShapeDtypeStruct
    x1, pay, keys, cnt = pl.pallas_call(
        functools.partial(_merge_route_kernel, by_pair=by_pair),
        grid=(n // tile,),
        in_specs=[row(d), row(attn.shape[1]), row(d), row(d)]
        + [_const_spec(c.shape) for c in consts],
        out_specs=[row(d), row(d + LANES), pl.BlockSpec((1, 1, tile), lambda i: (i, 0, 0)),
                   pl.BlockSpec((1, LANES), lambda i: (0, 0))],
        out_shape=[sds((n, d), F32), sds((n, d + LANES), F32),
                   sds((n // tile, 1, tile), jnp.int32), sds((1, LANES), jnp.int32)],
        scratch_shapes=[pltpu.VMEM((1, LANES), F32)],
        compiler_params=pltpu.CompilerParams(dimension_semantics=("arbitrary",),
                                             vmem_limit_bytes=VMEM_LIMIT),
        name="merge_route",
    )(x, attn, sa, mb, *consts)
    return x1, pay, keys, cnt[0, :n_buckets]


def _row_positions(key_row, off_ref):
    bucket = key_row >> GROUP_SHIFT
    pos = key_row & ((1 << GROUP_SHIFT) - 1)
    for g in range(off_ref.shape[0]):
        pos = pos + jnp.where(bucket == g, off_ref[g], 0)
    return pos


def _to_smem(src_ref, dst_ref, sem):
    cp = pltpu.make_async_copy(src_ref, dst_ref, sem)
    cp.start()
    cp.wait()


def _dispatch_kernel(off_ref, cnt_ref, key_ref, pay_ref, xs_ref, stage_ref, posv_ref, pos_smem,
                     zero_ref, sem_s, sem_in, sem_row, sem_pad, *, tile_m):
    i = pl.program_id(0)
    n = pl.num_programs(0)
    t = key_ref.shape[2]
    ring = stage_ref.shape[0]
    slot = lax.rem(i, ring)
    nxt = lax.rem(i + 1, ring)

    def load(tile, s):
        return pltpu.make_async_copy(pay_ref.at[pl.ds(tile * t, t), :], stage_ref.at[s],
                                     sem_in.at[s])

    def wait_rows(s):
        pltpu.make_async_copy(stage_ref.at[s], xs_ref.at[pl.ds(0, t), :], sem_row.at[s]).wait()

    @pl.when(i == 0)
    def _():
        load(0, 0).start()

    @pl.when(i >= ring - 1)
    def _():
        wait_rows(nxt)

    @pl.when(i + 1 < n)
    def _():
        load(i + 1, nxt).start()

    posv_ref[...] = _row_positions(key_ref[0], off_ref)
    _to_smem(posv_ref, pos_smem, sem_s)
    load(i, slot).wait()

    def copy_row(r, _):
        pltpu.make_async_copy(stage_ref.at[slot, pl.ds(r, 1), :],
                              xs_ref.at[pl.ds(pos_smem[0, r], 1), :], sem_row.at[slot]).start()
        return 0

    lax.fori_loop(0, t, copy_row, 0, unroll=True)

    @pl.when(i == n - 1)
    def _():
        for back in range(ring - 1):
            @pl.when(i >= back)
            def _():
                wait_rows(lax.rem(i - back + ring, ring))
        zero_ref[...] = jnp.zeros_like(zero_ref)
        pads = []

        def pad(cond, start, size):
            pads.append((cond, pltpu.make_async_copy(
                zero_ref.at[pl.ds(0, size), :], xs_ref.at[pl.ds(start, size), :], sem_pad)))

        used = 0
        n_buckets = cnt_ref.shape[0]
        for g in range(n_buckets):
            c = cnt_ref[g]
            c8 = (c + 7) >> 3 << 3
            end = (c + tile_m - 1) // tile_m * tile_m
            for k in range(7):
                pad(c + k < c8, off_ref[g] + c + k, 1)
            start, size = off_ref[g] + c8, 8
            while size < tile_m:
                cond = ((end - c8) & size) != 0
                pad(cond, pl.multiple_of(start, 8), size)
                start = start + jnp.where(cond, size, 0)
                size *= 2
            used = off_ref[g] + end
        for k in range(n_buckets):
            start = used + k * tile_m
            pad(start < xs_ref.shape[0], pl.multiple_of(start, tile_m), tile_m)
        for cond, cp in pads:
            pl.when(cond)(cp.start)
        for cond, cp in pads:
            pl.when(cond)(cp.wait)


def _dispatch(pay, keys, off, counts, *, tile_m):
    n, w = pay.shape
    nt, _, tile = keys.shape
    rows = n + counts.shape[0] * tile_m
    return pl.pallas_call(
        functools.partial(_dispatch_kernel, tile_m=tile_m),
        grid_spec=pltpu.PrefetchScalarGridSpec(
            num_scalar_prefetch=2,
            grid=(nt,),
            in_specs=[pl.BlockSpec((1, 1, tile), lambda i, off, cnt: (i, 0, 0)),
                      pl.BlockSpec(memory_space=pl.ANY)],
            out_specs=pl.BlockSpec(memory_space=pl.ANY),
            scratch_shapes=[pltpu.VMEM((DISPATCH_RING, tile, w), F32),
                            pltpu.VMEM((1, tile), jnp.int32), pltpu.SMEM((1, tile), jnp.int32),
                            pltpu.VMEM((tile_m, w), F32), pltpu.SemaphoreType.DMA,
                            pltpu.SemaphoreType.DMA((DISPATCH_RING,)),
                            pltpu.SemaphoreType.DMA((DISPATCH_RING,)),
                            pltpu.SemaphoreType.DMA]),
        out_shape=jax.ShapeDtypeStruct((rows, w), F32),
        compiler_params=pltpu.CompilerParams(dimension_semantics=("arbitrary",),
                                             vmem_limit_bytes=VMEM_LIMIT),
        name="dispatch",
    )(off, counts, keys, pay)


def _bucket_experts_kernel(exp_ref, xs_ref, *refs):
    *w_refs, y_ref = refs
    k = len(w_refs) // 3
    d = y_ref.shape[1]
    i = pl.program_id(0)
    h = xs_ref[:, 0:d].astype(BF16)
    gate = xs_ref[:, d:d + LANES]
    lane = lax.broadcasted_iota(jnp.int32, (1, LANES), 1)
    y = None
    for j in range(k):
        weg_ref, weu_ref, wed_ref = w_refs[3 * j:3 * j + 3]
        g = jnp.sum(jnp.where(lane == exp_ref[j, i], gate, 0.0), axis=-1, keepdims=True)
        act = jax.nn.silu(_dot(h, weg_ref[0])) * _dot(h, weu_ref[0])
        yj = _dot((g * act).astype(BF16), wed_ref[0])
        y = yj if y is None else y + yj
    y_ref[...] = y


def _bucket_experts(xs, tile_exp, wts, *, tile_m):
    d = xs.shape[1] - LANES
    _, _, d_exp = wts['w_exp_gate'].shape
    k = tile_exp.shape[0]
    w_specs, w_args = [], []
    for j in range(k):
        pick = lambda i, ex, j=j: (ex[j, i], 0, 0)
        w_specs += [pl.BlockSpec((1, d, d_exp), pick), pl.BlockSpec((1, d, d_exp), pick),
                    pl.BlockSpec((1, d_exp, d), pick)]
        w_args += [wts['w_exp_gate'], wts['w_exp_up'], wts['w_exp_down']]
    return pl.pallas_call(
        _bucket_experts_kernel,
        grid_spec=pltpu.PrefetchScalarGridSpec(
            num_scalar_prefetch=1,
            grid=(xs.shape[0] // tile_m,),
            in_specs=[pl.BlockSpec((tile_m, d + LANES), lambda i, ex: (i, 0))] + w_specs,
            out_specs=pl.BlockSpec((tile_m, d), lambda i, ex: (i, 0))),
        out_shape=jax.ShapeDtypeStruct((xs.shape[0], d), F32),
        compiler_params=pltpu.CompilerParams(dimension_semantics=("arbitrary",),
                                             vmem_limit_bytes=VMEM_LIMIT),
        name="bucket_experts",
    )(tile_exp, xs, *w_args)


def _combine_ple_kernel(off_ref, x1_ref, p_ref, keyc_ref, keyn_ref, lnple_ref, wpg_ref, wpp_ref,
                        y_ref, o_ref, ybuf_ref, posv_ref, pos_smem, sem_y, sem_s):
    i = pl.program_id(0)
    n = pl.num_programs(0)
    t = x1_ref.shape[0]
    slot = i % 2

    def fetch(key_ref, s):
        posv_ref[...] = _row_positions(key_ref[0], off_ref)
        _to_smem(posv_ref, pos_smem.at[pl.ds(s, 1), :], sem_s)

        def gather_row(r, _):
            pltpu.make_async_copy(y_ref.at[pl.ds(pos_smem[s, r], 1), :],
                                  ybuf_ref.at[s, pl.ds(r, 1), :], sem_y.at[s]).start()
            return 0

        lax.fori_loop(0, t, gather_row, 0, unroll=True)

    @pl.when(i == 0)
    def _():
        fetch(keyc_ref, 0)

    @pl.when(i + 1 < n)
    def _():
        fetch(keyn_ref, 1 - slot)

    pltpu.make_async_copy(y_ref.at[pl.ds(0, t), :], ybuf_ref.at[slot], sem_y.at[slot]).wait()
    x2 = x1_ref[...] + ybuf_ref[slot]
    hp = (x2 * _rms_scale(x2) * lnple_ref[...]).astype(BF16)
    gate = jax.nn.sigmoid(_dot(hp, wpg_ref[...]))
    o_ref[...] = x2 + gate * _dot(p_ref[...].astype(BF16), wpp_ref[...])


def _combine_ple(x1, p, keys, off, y, wts):
    n, d = x1.shape
    nt, _, tile = keys.shape
    row = lambda c: pl.BlockSpec((tile, c), lambda i, off: (i, 0))
    consts = [wts['ln_ple'], wts['w_ple_gate'], wts['w_ple_proj']]
    return pl.pallas_call(
        _combine_ple_kernel,
        grid_spec=pltpu.PrefetchScalarGridSpec(
            num_scalar_prefetch=1,
            grid=(nt,),
            in_specs=[row(d), row(p.shape[1]),
                      pl.BlockSpec((1, 1, tile), lambda i, off: (i, 0, 0)),
                      pl.BlockSpec((1, 1, tile),
                                   lambda i, off: (jnp.minimum(i + 1, nt - 1), 0, 0))]
            + [_const_spec(c.shape) for c in consts] + [pl.BlockSpec(memory_space=pl.ANY)],
            out_specs=row(d),
            scratch_shapes=[pltpu.VMEM((2, tile, d), F32), pltpu.VMEM((1, tile), jnp.int32),
                            pltpu.SMEM((2, tile), jnp.int32), pltpu.SemaphoreType.DMA((2,)),
                            pltpu.SemaphoreType.DMA]),
        out_shape=jax.ShapeDtypeStruct((n, d), F32),
        compiler_params=pltpu.CompilerParams(dimension_semantics=("arbitrary",),
                                             vmem_limit_bytes=VMEM_LIMIT),
        name="combine_ple",
    )(off, x1, p, keys, keys, *consts, y)


def _channel_stage(x, attn, sa, mb, p, wts, *, tile, tile_m, by_pair):
    n = x.shape[0]
    assert n % tile == 0 and n % tile_m == 0
    x1, pay, keys, counts = _merge_route(x, attn, sa, mb, wts, tile=tile, by_pair=by_pair)
    ends = jnp.cumsum((counts + tile_m - 1) // tile_m * tile_m)
    off = jnp.concatenate([jnp.zeros((1,), jnp.int32), ends[:-1]])
    xs = _dispatch(pay, keys, off, counts, tile_m=tile_m)
    tile_start = jnp.arange(xs.shape[0] // tile_m, dtype=jnp.int32) * tile_m
    tile_bucket = jnp.minimum(jnp.sum(tile_start[:, None] >= ends[None, :], axis=1),
                              counts.shape[0] - 1)
    tile_exp = jnp.array(_expert_buckets(by_pair), jnp.int32)[tile_bucket].T
    y = _bucket_experts(xs, tile_exp, wts, tile_m=tile_m)
    return _combine_ple(x1, p, keys, off, y, wts)


def _layer_weights(i, ln_mix, w_in, q_norm, k_norm, conv_w, w_attn_branch, w_conv_branch, w_out,
                   ln_ffn, w_router_group, b_router_group, w_router_expert, b_router_expert,
                   w_exp_gate, w_exp_up, w_exp_down, ln_ple, w_ple_gate, w_ple_proj):
    aw = ATTN_WIDTH
    pad_cols = lambda a: jnp.pad(a, ((0, 0), (0, LANES - a.shape[1])))
    head_id = jnp.arange(aw) // HEAD_DIM
    k_gain = jnp.tile(k_norm[i], N_HEADS)
    return dict(
        ln_mix=ln_mix[i][None, :], w_in=w_in[i].astype(BF16),
        w_kv_t=w_in[i][:, aw:3 * aw].T.astype(BF16),
        q_norm=jnp.tile(q_norm[i], N_HEADS)[None, :], k_norm=k_gain[None, :],
        k_norm_t=jnp.broadcast_to(k_gain[:, None], (aw, LANES)),
        head_sum=(head_id[:, None] == head_id[None, :]).astype(BF16),
        conv_w=conv_w[i], w_conv_branch=w_conv_branch[i].astype(BF16),
        w_attn_branch=w_attn_branch[i].astype(BF16), w_out=w_out[i].astype(BF16),
        ln_ffn=ln_ffn[i][None, :],
        w_router_group=pad_cols(w_router_group[i]).astype(BF16),
        b_router_group=pad_cols(b_router_group[i][None, :]),
        w_router_expert=pad_cols(w_router_expert[i]).astype(BF16),
        b_router_expert=pad_cols(b_router_expert[i][None, :]),
        w_exp_gate=w_exp_gate[i].astype(BF16), w_exp_up=w_exp_up[i].astype(BF16),
        w_exp_down=w_exp_down[i].astype(BF16),
        ln_ple=ln_ple[i][None, :], w_ple_gate=w_ple_gate[i].astype(BF16),
        w_ple_proj=w_ple_proj[i].astype(BF16))


def _tile_plan(n_prompt, seq, n_sample):
    largest = lambda n, cap: max(t for t in (1024, 512, 256, 128, 64, 32, 16, 8)
                                 if t <= cap and n % t == 0)
    return dict(input=largest(seq, 1024), channel=largest(n_prompt, 512),
                bucket=largest(n_prompt, 256), sample_bucket=largest(n_sample, 256))


def _feature_major(a):
    b, s, h, dh = a.shape
    return jnp.transpose(a, (0, 2, 3, 1)).reshape(b, h * dh, s)


def _position_major(a):
    b, _, s = a.shape
    return jnp.transpose(a.reshape(b, N_HEADS, HEAD_DIM, s), (0, 3, 1, 2))


def kernel(x_prompt, x_sample, cache_k, cache_v, state_conv, p_prompt, p_sample, ln_mix, w_in, q_norm, k_norm, conv_w, w_attn_branch, w_conv_branch, w_out, ln_ffn, w_router_group, b_router_group, w_router_expert, b_router_expert, w_exp_gate, w_exp_up, w_exp_down, ln_ple, w_ple_gate, w_ple_proj):
    depth = ln_mix.shape[0]
    nb, seq, d = x_prompt.shape
    nbs, seqs, _ = x_sample.shape
    aw = ATTN_WIDTH
    assert cache_k.shape[3:] == (N_HEADS, HEAD_DIM)
    assert w_router_group.shape[2] == N_GROUPS
    assert w_router_expert.shape[2] == N_GROUPS * EXPERTS_PER_GROUP
    tiles = _tile_plan(nb * seq, seq, nbs * seqs)
    tri = (jnp.arange(ATTN_BLOCK)[:, None] >= jnp.arange(ATTN_BLOCK)[None, :]).astype(BF16)

    xp = x_prompt.reshape(nb * seq, d)
    xs = x_sample.reshape(nbs * seqs, d)
    outs = [[] for _ in range(6)]
    for i in range(depth):
        wts = _layer_weights(i, ln_mix, w_in, q_norm, k_norm, conv_w, w_attn_branch,
                             w_conv_branch, w_out, ln_ffn, w_router_group, b_router_group,
                             w_router_expert, b_router_expert, w_exp_gate, w_exp_up, w_exp_down,
                             ln_ple, w_ple_gate, w_ple_proj)

        (q, kt, vt, ktb, vtb, sa, mb), conv_new = _input_stage_prompt(
            xp, wts, seq=seq, tile=tiles['input'])
        attn = _attn_prompt(q, ktb, vtb, tri)
        xp = _channel_stage(xp, attn, sa, mb, p_prompt[i].reshape(nb * seq, -1), wts,
                            tile=tiles['channel'], tile_m=tiles['bucket'], by_pair=True)
        outs[0].append(_position_major(kt))
        outs[1].append(_position_major(vt))
        outs[2].append(conv_new)

        (q, k, v, kb, vb, sa, mb), conv_new = _input_stage_sample(xs, state_conv[i], wts, seq=seqs)
        attn = _attn_sample(q.reshape(nbs, seqs, aw), kb.reshape(nbs, seqs, aw),
                            vb.reshape(nbs, seqs, aw), _feature_major(cache_k[i]),
                            _feature_major(cache_v[i]), tri)
        xs = _channel_stage(xs, attn.reshape(nbs * seqs, aw), sa, mb,
                            p_sample[i].reshape(nbs * seqs, -1), wts, tile=nbs * seqs,
                            tile_m=tiles['sample_bucket'], by_pair=False)
        outs[3].append(k.reshape(nbs, seqs, N_HEADS, HEAD_DIM))
        outs[4].append(v.reshape(nbs, seqs, N_HEADS, HEAD_DIM))
        outs[5].append(conv_new)

    kp, vp, cp, ks, vs, cs = [jnp.stack(o) for o in outs]
    return (xp.reshape(nb, seq, d), xs.reshape(nbs, seqs, d), kp, vp, cp, ks, vs, cs)
```

```python
import functools

import jax
import jax.numpy as jnp
from jax import lax
from jax.experimental import pallas as pl
from jax.experimental.pallas import tpu as pltpu

F32 = jnp.float32
BF16 = jnp.bfloat16

EPS = 1e-6
N_HEADS = 8
HEAD_DIM = 64
ATTN_WIDTH = N_HEADS * HEAD_DIM
N_GROUPS = 4
EXPERTS_PER_GROUP = 4
GROUP_SHIFT = 16
DISPATCH_RING = 3
LANES = 128
ATTN_BLOCK = 256
MASKED = -1e30
SP_CLAMP = 64.0
ATTN_PIPE = 8
ATTN_EAGER = 2
WEIGHT_GONE = 150.0
LOG2E_HI = 1.4426950216293335
LOG2E_LO = 1.925963033500011e-08
VMEM_LIMIT = 56 * 1024 * 1024


def _dot(a, b):
    return jnp.dot(a, b, preferred_element_type=F32)


def _dot_nt(a, b):
    return lax.dot_general(a, b, (((1,), (1,)), ((), ())), preferred_element_type=F32)


def _rms_scale(x):
    return lax.rsqrt(jnp.mean(x * x, axis=-1, keepdims=True) + EPS)


def _const_spec(shape):
    nd = len(shape)
    return pl.BlockSpec(shape, lambda *_: (0,) * nd, pipeline_mode=pl.Buffered(1))


def _head_norm(z, gain, hsum_ref):
    ss = _dot((z * z).astype(BF16), hsum_ref[...])
    return z * lax.rsqrt(ss * (1.0 / HEAD_DIM) + EPS) * gain


def _input_stage_body(x, past1, past2, lnmix_ref, win_ref, qn_ref, hsum_ref, convw_ref, wcb_ref,
                      q_ref, sa_ref, mb_ref):
    aw = ATTN_WIDTH
    cw = convw_ref.shape[1]
    d = x.shape[1]
    h = (x * _rms_scale(x) * lnmix_ref[...]).astype(BF16)

    def proj(lo, width):
        return _dot(h, win_ref[:, lo:lo + width])

    q = _head_norm(proj(0, aw), qn_ref[...], hsum_ref)
    q = q * (HEAD_DIM ** -0.5)
    q_ref[...] = (q * LOG2E_HI + q * LOG2E_LO).astype(BF16)

    cb = proj(3 * aw, cw)
    u = proj(3 * aw + cw, cw) * proj(3 * aw + 2 * cw, cw)
    u1, u2 = past1(u), past2(u)
    conv_y = convw_ref[0:1, :] * u2 + convw_ref[1:2, :] * u1 + convw_ref[2:3, :] * u
    yb = _dot((cb * conv_y).astype(BF16), wcb_ref[...])
    ga = proj(3 * aw + 3 * cw, d)
    gb = proj(3 * aw + 3 * cw + d, d)
    sa_ref[...] = jax.nn.sigmoid(ga).astype(BF16)
    mb_ref[...] = (jax.nn.sigmoid(gb) * yb).astype(BF16)
    return h, proj, u


def _input_stage_prompt_kernel(x_ref, lnmix_ref, win_ref, wkvt_ref, qn_ref, knt_ref, hsum_ref,
                               convw_ref, wcb_ref, q_ref, kt_ref, vt_ref, ktb_ref, vtb_ref,
                               sa_ref, mb_ref, cnew_ref, tail_ref, *, tiles_per_seq):
    t = x_ref.shape[0]
    aw = ATTN_WIDTH

    @pl.when(pl.program_id(0) % tiles_per_seq == 0)
    def _():
        tail_ref[...] = jnp.zeros_like(tail_ref)

    def shifted(u, n):
        ext = jnp.concatenate([tail_ref[...], u], axis=0)
        return pltpu.roll(ext, n, axis=0)[8:, :]

    h, _, u = _input_stage_body(x_ref[...], lambda u: shifted(u, 1), lambda u: shifted(u, 2),
                                lnmix_ref, win_ref, qn_ref, hsum_ref, convw_ref, wcb_ref,
                                q_ref, sa_ref, mb_ref)
    tail_ref[...] = u[t - 8:, :]
    cnew_ref[0] = u[t - 8:, :]

    kt = _dot_nt(wkvt_ref[0:aw, :], h)
    k3 = kt.reshape(N_HEADS, HEAD_DIM, t)
    scale = lax.rsqrt(jnp.mean(k3 * k3, axis=1, keepdims=True) + EPS)
    kt = (k3 * scale).reshape(aw, t) * jnp.tile(knt_ref[...], (1, t // LANES))
    kt_ref[0] = kt
    ktb_ref[0] = kt.astype(BF16)
    vt = _dot_nt(wkvt_ref[aw:2 * aw, :], h)
    vt_ref[0] = vt
    vtb_ref[0] = vt.astype(BF16)


def _input_stage_sample_kernel(x_ref, pe0_ref, pe1_ref, lnmix_ref, win_ref, qn_ref, kn_ref,
                               hsum_ref, convw_ref, wcb_ref, q_ref, k_ref, v_ref, kb_ref,
                               vb_ref, sa_ref, mb_ref, u_ref, *, seq):
    t = x_ref.shape[0]
    aw = ATTN_WIDTH
    pos = lax.broadcasted_iota(jnp.int32, (t, 1), 0) % seq

    def past1(u):
        return jnp.where(pos == 0, pe1_ref[...], pltpu.roll(u, 1, axis=0))

    def past2(u):
        return jnp.where(pos == 0, pe0_ref[...],
                         jnp.where(pos == 1, pe1_ref[...], pltpu.roll(u, 2, axis=0)))

    _, proj, u = _input_stage_body(x_ref[...], past1, past2, lnmix_ref, win_ref, qn_ref,
                                   hsum_ref, convw_ref, wcb_ref, q_ref, sa_ref, mb_ref)
    u_ref[...] = u
    k = _head_norm(proj(aw, aw), kn_ref[...], hsum_ref)
    k_ref[...] = k
    kb_ref[...] = k.astype(BF16)
    v = proj(2 * aw, aw)
    v_ref[...] = v
    vb_ref[...] = v.astype(BF16)


def _input_stage_prompt(x, wts, *, seq, tile):
    n, d = x.shape
    nb = n // seq
    aw = ATTN_WIDTH
    cw = wts['conv_w'].shape[1]
    tiles_per_seq = seq // tile
    row = lambda w: pl.BlockSpec((tile, w), lambda i: (i, 0))
    feat = pl.BlockSpec((1, aw, tile), lambda i: (i // tiles_per_seq, 0, i % tiles_per_seq))
    consts = [wts['ln_mix'], wts['w_in'], wts['w_kv_t'], wts['q_norm'], wts['k_norm_t'],
              wts['head_sum'], wts['conv_w'], wts['w_conv_branch']]
    sds = jax.ShapeDtypeStruct
    outs = pl.pallas_call(
        functools.partial(_input_stage_prompt_kernel, tiles_per_seq=tiles_per_seq),
        grid=(n // tile,),
        in_specs=[row(d)] + [_const_spec(c.shape) for c in consts],
        out_specs=[row(aw), feat, feat, feat, feat, row(d), row(d),
                   pl.BlockSpec((1, 8, cw), lambda i: (i // tiles_per_seq, 0, 0))],
        out_shape=[sds((n, aw), BF16), sds((nb, aw, seq), F32), sds((nb, aw, seq), F32),
                   sds((nb, aw, seq), BF16), sds((nb, aw, seq), BF16), sds((n, d), BF16),
                   sds((n, d), BF16), sds((nb, 8, cw), F32)],
        scratch_shapes=[pltpu.VMEM((8, cw), F32)],
        compiler_params=pltpu.CompilerParams(dimension_semantics=("arbitrary",),
                                             vmem_limit_bytes=VMEM_LIMIT),
        name="input_stage_prompt",
    )(x, *consts)
    return outs[:7], outs[7][:, 6:, :]


def _input_stage_sample(x, state, wts, *, seq):
    n, d = x.shape
    aw = ATTN_WIDTH
    cw = wts['conv_w'].shape[1]
    pe0 = jnp.repeat(state[:, 0, :], seq, axis=0)
    pe1 = jnp.repeat(state[:, 1, :], seq, axis=0)
    consts = [wts['ln_mix'], wts['w_in'], wts['q_norm'], wts['k_norm'], wts['head_sum'],
              wts['conv_w'], wts['w_conv_branch']]
    full = lambda a: pl.BlockSpec(a.shape, lambda i: (0,) * a.ndim)
    sds = jax.ShapeDtypeStruct
    out_shape = [sds((n, aw), BF16), sds((n, aw), F32), sds((n, aw), F32), sds((n, aw), BF16),
                 sds((n, aw), BF16), sds((n, d), BF16), sds((n, d), BF16), sds((n, cw), F32)]
    outs = pl.pallas_call(
        functools.partial(_input_stage_sample_kernel, seq=seq),
        grid=(1,),
        in_specs=[full(x), full(pe0), full(pe1)] + [full(c) for c in consts],
        out_specs=[full(o) for o in out_shape],
        out_shape=out_shape,
        compiler_params=pltpu.CompilerParams(dimension_semantics=("arbitrary",),
                                             vmem_limit_bytes=VMEM_LIMIT),
        name="input_stage_sample",
    )(x, pe0, pe1, *consts)
    conv_new = outs[7].reshape(n // seq, seq, cw)[:, seq - 2:, :]
    return outs[:7], conv_new


def _softplus2(z, keep=None):
    sp = jnp.maximum(z, jnp.log2(1.0 + jnp.exp2(jnp.minimum(z, SP_CLAMP))))
    if keep is not None:
        sp = jnp.where(keep, sp, 0.0)
    return sp.astype(BF16), jnp.sum(sp, axis=-1, keepdims=True)


def _stick_weights(z, sp, carry, tri_ref, keep=None):
    a = jnp.exp2(jnp.minimum(z - _dot(sp, tri_ref[...]), 0.0) - carry)
    if keep is not None:
        a = jnp.where(keep, a, 0.0)
    return a.astype(BF16)


def _stick_block(z, carry, tri_ref, keep):
    sp, row_sum = _softplus2(z, keep)
    return _stick_weights(z, sp, carry, tri_ref, keep), carry + row_sum


def _attn_items(nq):
    items = [(qi, j) for qi in range(nq) for j in range(qi, max(qi - ATTN_EAGER, -1), -1)]
    items = [(0, 0)] * (-len(items) % ATTN_PIPE) + items
    return (jnp.array([i[0] for i in items], jnp.int32),
            jnp.array([i[1] for i in items], jnp.int32))


def _attn_prompt_kernel(iq_ref, ik_ref, q_ref, kt_ref, vt_ref, tri_ref, mask_ref, o_ref,
                        z_ref, sp_ref, rs_ref, carry_ref, acc_ref):
    tq = tri_ref.shape[1]
    nq = q_ref.shape[0] // tq
    n_items = iq_ref.shape[0]
    heads = range(LANES // HEAD_DIM)
    lane_head = lax.broadcasted_iota(jnp.int32, (1, LANES), 1) // HEAD_DIM
    row_head = lax.broadcasted_iota(jnp.int32, (LANES, tq), 0) // HEAD_DIM

    def head_q(q, h):
        return q * (lane_head == h).astype(BF16)

    def weighted_v(a_parts, ks):
        vt = vt_ref[0, :, pl.ds(ks, tq)]
        vts = [jnp.where(row_head == h, vt, jnp.zeros_like(vt)) for h in heads]
        return _dot_nt(jnp.concatenate(a_parts, axis=1), jnp.concatenate(vts, axis=1))

    def blocks(item):
        qi, j = iq_ref[item], ik_ref[item]
        return qi, pl.multiple_of(qi * tq, tq), pl.multiple_of(j * tq, tq), qi == j

    def front(item, slot):
        _, qs, ks, own = blocks(item)
        q = q_ref[pl.ds(qs, tq), :]
        kt = kt_ref[0, :, pl.ds(ks, tq)]
        bias = mask_ref[own.astype(jnp.int32)]
        for h in heads:
            z = _dot(head_q(q, h), kt) + bias
            z_ref[slot, h] = z
            sp_ref[slot, h], rs_ref[slot, h] = _softplus2(z)

    def back(item, slot):
        qi, qs, ks, own = blocks(item)
        a_parts = []
        for h in heads:
            carry = jnp.where(own, 0.0, carry_ref[qi, h])
            a_parts.append(_stick_weights(z_ref[slot, h], sp_ref[slot, h], carry, tri_ref))
            carry_ref[qi, h] = carry + rs_ref[slot, h]
        acc_ref[pl.ds(qs, tq), :] = (jnp.where(own, 0.0, acc_ref[pl.ds(qs, tq), :])
                                     + weighted_v(a_parts, ks))

    carry_ref[...] = jnp.zeros_like(carry_ref)
    acc_ref[...] = jnp.zeros_like(acc_ref)
    depth = ATTN_PIPE // 2
    for s in range(depth):
        front(s, s)

    def trip(k, _):
        for s in range(ATTN_PIPE):
            item = ATTN_PIPE * k + s
            if isinstance(k, int) and item + depth >= n_items:
                pass
            else:
                front(item + depth, (s + depth) % ATTN_PIPE)
            back(item, s)
        return 0

    n_trips = n_items // ATTN_PIPE
    lax.fori_loop(0, n_trips - 1, trip, 0)
    trip(n_trips - 1, 0)

    def finish(qi, _):
        qs = pl.multiple_of(qi * tq, tq)

        def more(state):
            j, c0, c1 = state
            return jnp.logical_and(j >= 0, jnp.min(jnp.minimum(c0, c1)) < WEIGHT_GONE)

        def step(state):
            j, *carries = state
            ks = pl.multiple_of(j * tq, tq)
            q = q_ref[pl.ds(qs, tq), :]
            kt = kt_ref[0, :, pl.ds(ks, tq)]
            a_parts = []
            for h in heads:
                a, carries[h] = _stick_block(_dot(head_q(q, h), kt), carries[h], tri_ref, None)
                a_parts.append(a)
            acc_ref[pl.ds(qs, tq), :] += weighted_v(a_parts, ks)
            return (j - 1, *carries)

        lax.while_loop(more, step, (qi - ATTN_EAGER, carry_ref[qi, 0], carry_ref[qi, 1]))
        return 0

    if nq > ATTN_EAGER:
        @pl.when(jnp.min(carry_ref[ATTN_EAGER:]) < WEIGHT_GONE)
        def _():
            lax.fori_loop(ATTN_EAGER, nq, finish, 0)

    o_ref[...] = acc_ref[...].astype(o_ref.dtype)


def _attn_prompt(q, ktb, vtb, tri):
    n, aw = q.shape
    nb, _, seq = ktb.shape
    tq = ATTN_BLOCK
    row = jnp.arange(tq)[:, None]
    col = jnp.arange(tq)[None, :]
    mask = jnp.stack([jnp.zeros((tq, tq), F32), jnp.where(col < row, 0.0, MASKED).astype(F32)])
    iq, ik = _attn_items(seq // tq)
    qspec = pl.BlockSpec((seq, LANES), lambda b, hp, iq, ik: (b, hp))
    kvspec = pl.BlockSpec((1, LANES, seq), lambda b, hp, iq, ik: (b, hp, 0))
    const = lambda a: pl.BlockSpec(a.shape, lambda b, hp, iq, ik: (0,) * a.ndim)
    return pl.pallas_call(
        _attn_prompt_kernel,
        grid_spec=pltpu.PrefetchScalarGridSpec(
            num_scalar_prefetch=2,
            grid=(nb, aw // LANES),
            in_specs=[qspec, kvspec, kvspec, const(tri), const(mask)],
            out_specs=qspec,
            scratch_shapes=[pltpu.VMEM((ATTN_PIPE, 2, tq, tq), F32),
                            pltpu.VMEM((ATTN_PIPE, 2, tq, tq), BF16),
                            pltpu.VMEM((ATTN_PIPE, 2, tq, 1), F32),
                            pltpu.VMEM((seq // tq, 2, tq, 1), F32),
                            pltpu.VMEM((seq, LANES), F32)]),
        out_shape=jax.ShapeDtypeStruct((n, aw), BF16),
        compiler_params=pltpu.CompilerParams(dimension_semantics=("arbitrary", "arbitrary"),
                                             vmem_limit_bytes=VMEM_LIMIT),
        name="attn_prompt",
    )(iq, ik, q, ktb, vtb, tri, mask)


def _attn_sample_kernel(q_ref, kn_ref, vn_ref, tri_ref, ck_ref, cv_ref, o_ref,
                        kbuf_ref, vbuf_ref, acc_ref, sem):
    b = pl.program_id(0)
    tq, aw = q_ref.shape[1], q_ref.shape[2]
    rows = N_HEADS * tq
    tk = ATTN_BLOCK
    last = ck_ref.shape[2] // tk - 1
    slot = b % 2

    def fetch(seq_id, j, s):
        ks = pl.multiple_of(j * tk, tk)
        return (pltpu.make_async_copy(ck_ref.at[seq_id, :, pl.ds(ks, tk)], kbuf_ref.at[s],
                                      sem.at[2 * s]),
                pltpu.make_async_copy(cv_ref.at[seq_id, :, pl.ds(ks, tk)], vbuf_ref.at[s],
                                      sem.at[2 * s + 1]))

    @pl.when(b == 0)
    def _():
        for cp in fetch(0, last, 0):
            cp.start()

    @pl.when(b + 1 < pl.num_programs(0))
    def _():
        for cp in fetch(b + 1, last, 1 - slot):
            cp.start()

    row_head = lax.broadcasted_iota(jnp.int32, (rows, aw), 0) // tq
    lane_head = lax.broadcasted_iota(jnp.int32, (rows, aw), 1) // HEAD_DIM
    qt = jnp.concatenate([q_ref[0]] * N_HEADS, axis=0)
    qrow = jnp.where(row_head == lane_head, qt, jnp.zeros_like(qt))
    pad = jnp.zeros((tk - tq, aw), BF16)
    kn = jnp.concatenate([kn_ref[0], pad], axis=0)
    vn = jnp.concatenate([vn_ref[0], pad], axis=0)
    qpos = lax.broadcasted_iota(jnp.int32, (rows, tk), 0) % tq
    kpos = lax.broadcasted_iota(jnp.int32, (rows, tk), 1)
    a, carry = _stick_block(_dot_nt(qrow, kn), jnp.zeros((rows, 1), F32), tri_ref, kpos < qpos)
    acc_ref[...] = _dot(a, vn)
    for cp in fetch(b, last, slot):
        cp.wait()

    def step(state):
        j, _, carry = state
        a, carry = _stick_block(_dot(qrow, kbuf_ref[slot].astype(BF16)), carry, tri_ref, None)
        acc_ref[...] += _dot_nt(a, vbuf_ref[slot].astype(BF16))
        go_on = jnp.logical_and(j > 0, jnp.min(carry) < WEIGHT_GONE)

        @pl.when(go_on)
        def _():
            for cp in fetch(b, j - 1, slot):
                cp.start()
            for cp in fetch(b, j - 1, slot):
                cp.wait()

        return j - 1, go_on, carry

    lax.while_loop(lambda state: state[1], step, (jnp.int32(last), True, carry))

    lane_head = lax.broadcasted_iota(jnp.int32, (tq, aw), 1) // HEAD_DIM
    out = jnp.zeros((tq, aw), F32)
    for head in range(N_HEADS):
        out = out + jnp.where(lane_head == head, acc_ref[head * tq:(head + 1) * tq, :], 0.0)
    o_ref[0] = out.astype(o_ref.dtype)


def _attn_sample(q, kn, vn, cache_kt, cache_vt, tri):
    nb, tq, aw = q.shape
    new = pl.BlockSpec((1, tq, aw), lambda b: (b, 0, 0))
    hbm = pl.BlockSpec(memory_space=pl.ANY)
    return pl.pallas_call(
        _attn_sample_kernel,
        grid=(nb,),
        in_specs=[new, new, new, pl.BlockSpec(tri.shape, lambda b: (0, 0)), hbm, hbm],
        out_specs=new,
        out_shape=jax.ShapeDtypeStruct((nb, tq, aw), BF16),
        scratch_shapes=[pltpu.VMEM((2, aw, ATTN_BLOCK), F32), pltpu.VMEM((2, aw, ATTN_BLOCK), F32),
                        pltpu.VMEM((N_HEADS * tq, aw), F32), pltpu.SemaphoreType.DMA((4,))],
        compiler_params=pltpu.CompilerParams(dimension_semantics=("arbitrary",),
                                             vmem_limit_bytes=VMEM_LIMIT),
        name="attn_sample",
    )(q, kn, vn, tri, cache_kt, cache_vt)


def _first_max(vals, lane):
    m = jnp.max(vals, axis=-1, keepdims=True)
    idx = jnp.min(jnp.where(vals == m, lane, float(LANES)), axis=-1, keepdims=True)
    return m, idx


def _router_gates(h, wrg_ref, brg_ref, wre_ref, bre_ref):
    lane = lax.broadcasted_iota(jnp.int32, (1, LANES), 1).astype(F32)
    neg = -jnp.inf
    lg = _dot(h, wrg_ref[...]) + brg_ref[...]
    lg = jnp.where(lane < N_GROUPS, lg, neg)
    mg, gidx = _first_max(lg, lane)
    p_sel = 1.0 / jnp.sum(jnp.exp(lg - mg), axis=-1, keepdims=True)
    le = _dot(h, wre_ref[...]) + bre_ref[...]
    in_group = jnp.floor(lane * (1.0 / EXPERTS_PER_GROUP)) == gidx
    le = jnp.where(in_group, le, neg)
    t1, i1 = _first_max(le, lane)
    t2, i2 = _first_max(jnp.where(lane == i1, neg, le), lane)
    e2 = jnp.exp(t2 - t1)
    w1 = 1.0 / (1.0 + e2)
    gate = p_sel * (jnp.where(lane == i1, w1, 0.0) + jnp.where(lane == i2, e2 * w1, 0.0))
    return gate, gidx, i1, i2


def _expert_buckets(by_pair):
    epg = EXPERTS_PER_GROUP
    if not by_pair:
        return [[g * epg + j for j in range(epg)] for g in range(N_GROUPS)]
    return [[g * epg + lo, g * epg + hi] for g in range(N_GROUPS)
            for lo in range(epg) for hi in range(lo + 1, epg)]


def _merge_route_kernel(x_ref, attn_ref, sa_ref, mb_ref, wab_ref, wout_ref, lnffn_ref,
                        wrg_ref, brg_ref, wre_ref, bre_ref, ltri_ref,
                        x1_ref, pay_ref, key_ref, cnt_out_ref, cnt_ref, *, by_pair):
    t, d = x_ref.shape

    @pl.when(pl.program_id(0) == 0)
    def _():
        cnt_ref[...] = jnp.zeros_like(cnt_ref)

    ya = _dot(attn_ref[...], wab_ref[...])
    m = sa_ref[...].astype(F32) * ya + mb_ref[...].astype(F32)
    x1 = x_ref[...] + _dot(m.astype(BF16), wout_ref[...])
    x1_ref[...] = x1
    h = x1 * _rms_scale(x1) * lnffn_ref[...]
    gate, gidx, i1, i2 = _router_gates(h.astype(BF16), wrg_ref, brg_ref, wre_ref, bre_ref)
    pay_ref[:, 0:d] = h
    pay_ref[:, d:d + LANES] = gate
    bucket = gidx
    if by_pair:
        a, b = i1 - gidx * EXPERTS_PER_GROUP, i2 - gidx * EXPERTS_PER_GROUP
        lo, hi = jnp.minimum(a, b), jnp.maximum(a, b)
        pair = jnp.where(lo == 0.0, hi - 1.0, jnp.where(lo == 1.0, hi + 1.0, 5.0))
        bucket = gidx * 6.0 + pair

    lane = lax.broadcasted_iota(jnp.int32, (1, LANES), 1).astype(F32)
    onehot = lane == bucket
    before = _dot(ltri_ref[...], onehot.astype(BF16)) + cnt_ref[...]
    rank = jnp.sum(jnp.where(onehot, before, 0.0), axis=-1, keepdims=True)
    cnt = cnt_ref[...] + jnp.sum(onehot.astype(F32), axis=0, keepdims=True)
    cnt_ref[...] = cnt
    cnt_out_ref[...] = cnt.astype(jnp.int32)
    key = bucket * float(1 << GROUP_SHIFT) + rank
    key_ref[0] = jnp.broadcast_to(key, (t, LANES)).T[0:1, :].astype(jnp.int32)


def _merge_route(x, attn, sa, mb, wts, *, tile, by_pair):
    n, d = x.shape
    n_buckets = len(_expert_buckets(by_pair))
    assert EXPERTS_PER_GROUP == 4 and n_buckets <= LANES
    assert n < (1 << GROUP_SHIFT) and n_buckets << GROUP_SHIFT <= 1 << 24
    row = lambda c: pl.BlockSpec((tile, c), lambda i: (i, 0))
    ltri = (jnp.arange(tile)[:, None] > jnp.arange(tile)[None, :]).astype(BF16)
    consts = [wts['w_attn_branch'], wts['w_out'], wts['ln_ffn'], wts['w_router_group'],
              wts['b_router_group'], wts['w_router_expert'], wts['b_router_expert'], ltri]
    sds = jax.ShapeDtypeStruct
    x1, pay, keys, cnt = pl.pallas_call(
        functools.partial(_merge_route_kernel, by_pair=by_pair),
        grid=(n // tile,),
        in_specs=[row(d), row(attn.shape[1]), row(d), row(d)]
        + [_const_spec(c.shape) for c in consts],
        out_specs=[row(d), row(d + LANES), pl.BlockSpec((1, 1, tile), lambda i: (i, 0, 0)),
                   pl.BlockSpec((1, LANES), lambda i: (0, 0))],
        out_shape=[sds((n, d), F32), sds((n, d + LANES), F32),
                   sds((n // tile, 1, tile), jnp.int32), sds((1, LANES), jnp.int32)],
        scratch_shapes=[pltpu.VMEM((1, LANES), F32)],
        compiler_params=pltpu.CompilerParams(dimension_semantics=("arbitrary",),
                                             vmem_limit_bytes=VMEM_LIMIT),
        name="merge_route",
    )(x, attn, sa, mb, *consts)
    return x1, pay, keys, cnt[0, :n_buckets]


def _row_positions(key_row, off_ref):
    bucket = key_row >> GROUP_SHIFT
    pos = key_row & ((1 << GROUP_SHIFT) - 1)
    for g in range(off_ref.shape[0]):
        pos = pos + jnp.where(bucket == g, off_ref[g], 0)
    return pos


def _to_smem(src_ref, dst_ref, sem):
    cp = pltpu.make_async_copy(src_ref, dst_ref, sem)
    cp.start()
    cp.wait()


def _dispatch_kernel(off_ref, cnt_ref, key_ref, pay_ref, xs_ref, stage_ref, posv_ref, pos_smem,
                     zero_ref, sem_s, sem_in, sem_row, sem_pad, *, tile_m):
    i = pl.program_id(0)
    n = pl.num_programs(0)
    t = key_ref.shape[2]
    ring = stage_ref.shape[0]
    slot = lax.rem(i, ring)
    nxt = lax.rem(i + 1, ring)

    def load(tile, s):
        return pltpu.make_async_copy(pay_ref.at[pl.ds(tile * t, t), :], stage_ref.at[s],
                                     sem_in.at[s])

    def wait_rows(s):
        pltpu.make_async_copy(stage_ref.at[s], xs_ref.at[pl.ds(0, t), :], sem_row.at[s]).wait()

    @pl.when(i == 0)
    def _():
        load(0, 0).start()

    @pl.when(i >= ring - 1)
    def _():
        wait_rows(nxt)

    @pl.when(i + 1 < n)
    def _():
        load(i + 1, nxt).start()

    posv_ref[...] = _row_positions(key_ref[0], off_ref)
    _to_smem(posv_ref, pos_smem, sem_s)
    load(i, slot).wait()

    def copy_row(r, _):
        pltpu.make_async_copy(stage_ref.at[slot, pl.ds(r, 1), :],
                              xs_ref.at[pl.ds(pos_smem[0, r], 1), :], sem_row.at[slot]).start()
        return 0

    lax.fori_loop(0, t, copy_row, 0, unroll=True)

    @pl.when(i == n - 1)
    def _():
        for back in range(ring - 1):
            @pl.when(i >= back)
            def _():
                wait_rows(lax.rem(i - back + ring, ring))
        zero_ref[...] = jnp.zeros_like(zero_ref)
        pads = []

        def pad(cond, start, size):
            pads.append((cond, pltpu.make_async_copy(
                zero_ref.at[pl.ds(0, size), :], xs_ref.at[pl.ds(start, size), :], sem_pad)))

        used = 0
        n_buckets = cnt_ref.shape[0]
        for g in range(n_buckets):
            c = cnt_ref[g]
            c8 = (c + 7) >> 3 << 3
            end = (c + tile_m - 1) // tile_m * tile_m
            for k in range(7):
                pad(c + k < c8, off_ref[g] + c + k, 1)
            start, size = off_ref[g] + c8, 8
            while size < tile_m:
                cond = ((end - c8) & size) != 0
                pad(cond, pl.multiple_of(start, 8), size)
                start = start + jnp.where(cond, size, 0)
                size *= 2
            used = off_ref[g] + end
        for k in range(n_buckets):
            start = used + k * tile_m
            pad(start < xs_ref.shape[0], pl.multiple_of(start, tile_m), tile_m)
        for cond, cp in pads:
            pl.when(cond)(cp.start)
        for cond, cp in pads:
            pl.when(cond)(cp.wait)


def _dispatch(pay, keys, off, counts, *, tile_m):
    n, w = pay.shape
    nt, _, tile = keys.shape
    rows = n + counts.shape[0] * tile_m
    return pl.pallas_call(
        functools.partial(_dispatch_kernel, tile_m=tile_m),
        grid_spec=pltpu.PrefetchScalarGridSpec(
            num_scalar_prefetch=2,
            grid=(nt,),
            in_specs=[pl.BlockSpec((1, 1, tile), lambda i, off, cnt: (i, 0, 0)),
                      pl.BlockSpec(memory_space=pl.ANY)],
            out_specs=pl.BlockSpec(memory_space=pl.ANY),
            scratch_shapes=[pltpu.VMEM((DISPATCH_RING, tile, w), F32),
                            pltpu.VMEM((1, tile), jnp.int32), pltpu.SMEM((1, tile), jnp.int32),
                            pltpu.VMEM((tile_m, w), F32), pltpu.SemaphoreType.DMA,
                            pltpu.SemaphoreType.DMA((DISPATCH_RING,)),
                            pltpu.SemaphoreType.DMA((DISPATCH_RING,)),
                            pltpu.SemaphoreType.DMA]),
        out_shape=jax.ShapeDtypeStruct((rows, w), F32),
        compiler_params=pltpu.CompilerParams(dimension_semantics=("arbitrary",),
                                             vmem_limit_bytes=VMEM_LIMIT),
        name="dispatch",
    )(off, counts, keys, pay)


def _bucket_experts_kernel(exp_ref, xs_ref, *refs):
    *w_refs, y_ref = refs
    k = len(w_refs) // 3
    d = y_ref.shape[1]
    i = pl.program_id(0)
    h = xs_ref[:, 0:d].astype(BF16)
    gate = xs_ref[:, d:d + LANES]
    lane = lax.broadcasted_iota(jnp.int32, (1, LANES), 1)
    y = None
    for j in range(k):
        weg_ref, weu_ref, wed_ref = w_refs[3 * j:3 * j + 3]
        g = jnp.sum(jnp.where(lane == exp_ref[j, i], gate, 0.0), axis=-1, keepdims=True)
        act = jax.nn.silu(_dot(h, weg_ref[0])) * _dot(h, weu_ref[0])
        yj = _dot((g * act).astype(BF16), wed_ref[0])
        y = yj if y is None else y + yj
    y_ref[...] = y


def _bucket_experts(xs, tile_exp, wts, *, tile_m):
    d = xs.shape[1] - LANES
    _, _, d_exp = wts['w_exp_gate'].shape
    k = tile_exp.shape[0]
    w_specs, w_args = [], []
    for j in range(k):
        pick = lambda i, ex, j=j: (ex[j, i], 0, 0)
        w_specs += [pl.BlockSpec((1, d, d_exp), pick), pl.BlockSpec((1, d, d_exp), pick),
                    pl.BlockSpec((1, d_exp, d), pick)]
        w_args += [wts['w_exp_gate'], wts['w_exp_up'], wts['w_exp_down']]
    return pl.pallas_call(
        _bucket_experts_kernel,
        grid_spec=pltpu.PrefetchScalarGridSpec(
            num_scalar_prefetch=1,
            grid=(xs.shape[0] // tile_m,),
            in_specs=[pl.BlockSpec((tile_m, d + LANES), lambda i, ex: (i, 0))] + w_specs,
            out_specs=pl.BlockSpec((tile_m, d), lambda i, ex: (i, 0))),
        out_shape=jax.ShapeDtypeStruct((xs.shape[0], d), F32),
        compiler_params=pltpu.CompilerParams(dimension_semantics=("arbitrary",),
                                             vmem_limit_bytes=VMEM_LIMIT),
        name="bucket_experts",
    )(tile_exp, xs, *w_args)


def _combine_ple_kernel(off_ref, x1_ref, p_ref, keyc_ref, keyn_ref, lnple_ref, wpg_ref, wpp_ref,
                        y_ref, o_ref, ybuf_ref, posv_ref, pos_smem, sem_y, sem_s):
    i = pl.program_id(0)
    n = pl.num_programs(0)
    t = x1_ref.shape[0]
    slot = i % 2

    def fetch(key_ref, s):
        posv_ref[...] = _row_positions(key_ref[0], off_ref)
        _to_smem(posv_ref, pos_smem.at[pl.ds(s, 1), :], sem_s)

        def gather_row(r, _):
            pltpu.make_async_copy(y_ref.at[pl.ds(pos_smem[s, r], 1), :],
                                  ybuf_ref.at[s, pl.ds(r, 1), :], sem_y.at[s]).start()
            return 0

        lax.fori_loop(0, t, gather_row, 0, unroll=True)

    @pl.when(i == 0)
    def _():
        fetch(keyc_ref, 0)

    @pl.when(i + 1 < n)
    def _():
        fetch(keyn_ref, 1 - slot)

    pltpu.make_async_copy(y_ref.at[pl.ds(0, t), :], ybuf_ref.at[slot], sem_y.at[slot]).wait()
    x2 = x1_ref[...] + ybuf_ref[slot]
    hp = (x2 * _rms_scale(x2) * lnple_ref[...]).astype(BF16)
    gate = jax.nn.sigmoid(_dot(hp, wpg_ref[...]))
    o_ref[...] = x2 + gate * _dot(p_ref[...].astype(BF16), wpp_ref[...])


def _combine_ple(x1, p, keys, off, y, wts):
    n, d = x1.shape
    nt, _, tile = keys.shape
    row = lambda c: pl.BlockSpec((tile, c), lambda i, off: (i, 0))
    consts = [wts['ln_ple'], wts['w_ple_gate'], wts['w_ple_proj']]
    return pl.pallas_call(
        _combine_ple_kernel,
        grid_spec=pltpu.PrefetchScalarGridSpec(
            num_scalar_prefetch=1,
            grid=(nt,),
            in_specs=[row(d), row(p.shape[1]),
                      pl.BlockSpec((1, 1, tile), lambda i, off: (i, 0, 0)),
                      pl.BlockSpec((1, 1, tile),
                                   lambda i, off: (jnp.minimum(i + 1, nt - 1), 0, 0))]
            + [_const_spec(c.shape) for c in consts] + [pl.BlockSpec(memory_space=pl.ANY)],
            out_specs=row(d),
            scratch_shapes=[pltpu.VMEM((2, tile, d), F32), pltpu.VMEM((1, tile), jnp.int32),
                            pltpu.SMEM((2, tile), jnp.int32), pltpu.SemaphoreType.DMA((2,)),
                            pltpu.SemaphoreType.DMA]),
        out_shape=jax.ShapeDtypeStruct((n, d), F32),
        compiler_params=pltpu.CompilerParams(dimension_semantics=("arbitrary",),
                                             vmem_limit_bytes=VMEM_LIMIT),
        name="combine_ple",
    )(off, x1, p, keys, keys, *consts, y)


def _channel_stage(x, attn, sa, mb, p, wts, *, tile, tile_m, by_pair):
    n = x.shape[0]
    assert n % tile == 0 and n % tile_m == 0
    x1, pay, keys, counts = _merge_route(x, attn, sa, mb, wts, tile=tile, by_pair=by_pair)
    ends = jnp.cumsum((counts + tile_m - 1) // tile_m * tile_m)
    off = jnp.concatenate([jnp.zeros((1,), jnp.int32), ends[:-1]])
    xs = _dispatch(pay, keys, off, counts, tile_m=tile_m)
    tile_start = jnp.arange(xs.shape[0] // tile_m, dtype=jnp.int32) * tile_m
    tile_bucket = jnp.minimum(jnp.sum(tile_start[:, None] >= ends[None, :], axis=1),
                              counts.shape[0] - 1)
    tile_exp = jnp.array(_expert_buckets(by_pair), jnp.int32)[tile_bucket].T
    y = _bucket_experts(xs, tile_exp, wts, tile_m=tile_m)
    return _combine_ple(x1, p, keys, off, y, wts)


def _layer_weights(i, ln_mix, w_in, q_norm, k_norm, conv_w, w_attn_branch, w_conv_branch, w_out,
                   ln_ffn, w_router_group, b_router_group, w_router_expert, b_router_expert,
                   w_exp_gate, w_exp_up, w_exp_down, ln_ple, w_ple_gate, w_ple_proj):
    aw = ATTN_WIDTH
    pad_cols = lambda a: jnp.pad(a, ((0, 0), (0, LANES - a.shape[1])))
    head_id = jnp.arange(aw) // HEAD_DIM
    k_gain = jnp.tile(k_norm[i], N_HEADS)
    return dict(
        ln_mix=ln_mix[i][None, :], w_in=w_in[i].astype(BF16),
        w_kv_t=w_in[i][:, aw:3 * aw].T.astype(BF16),
        q_norm=jnp.tile(q_norm[i], N_HEADS)[None, :], k_norm=k_gain[None, :],
        k_norm_t=jnp.broadcast_to(k_gain[:, None], (aw, LANES)),
        head_sum=(head_id[:, None] == head_id[None, :]).astype(BF16),
        conv_w=conv_w[i], w_conv_branch=w_conv_branch[i].astype(BF16),
        w_attn_branch=w_attn_branch[i].astype(BF16), w_out=w_out[i].astype(BF16),
        ln_ffn=ln_ffn[i][None, :],
        w_router_group=pad_cols(w_router_group[i]).astype(BF16),
        b_router_group=pad_cols(b_router_group[i][None, :]),
        w_router_expert=pad_cols(w_router_expert[i]).astype(BF16),
        b_router_expert=pad_cols(b_router_expert[i][None, :]),
        w_exp_gate=w_exp_gate[i].astype(BF16), w_exp_up=w_exp_up[i].astype(BF16),
        w_exp_down=w_exp_down[i].astype(BF16),
        ln_ple=ln_ple[i][None, :], w_ple_gate=w_ple_gate[i].astype(BF16),
        w_ple_proj=w_ple_proj[i].astype(BF16))


def _tile_plan(n_prompt, seq, n_sample):
    largest = lambda n, cap: max(t for t in (1024, 512, 256, 128, 64, 32, 16, 8)
                                 if t <= cap and n % t == 0)
    return dict(input=largest(seq, 1024), channel=largest(n_prompt, 512),
                bucket=largest(n_prompt, 256), sample_bucket=largest(n_sample, 256))


def _feature_major(a):
    b, s, h, dh = a.shape
    return jnp.transpose(a, (0, 2, 3, 1)).reshape(b, h * dh, s)


def _position_major(a):
    b, _, s = a.shape
    return jnp.transpose(a.reshape(b, N_HEADS, HEAD_DIM, s), (0, 3, 1, 2))


def kernel(x_prompt, x_sample, cache_k, cache_v, state_conv, p_prompt, p_sample, ln_mix, w_in, q_norm, k_norm, conv_w, w_attn_branch, w_conv_branch, w_out, ln_ffn, w_router_group, b_router_group, w_router_expert, b_router_expert, w_exp_gate, w_exp_up, w_exp_down, ln_ple, w_ple_gate, w_ple_proj):
    depth = ln_mix.shape[0]
    nb, seq, d = x_prompt.shape
    nbs, seqs, _ = x_sample.shape
    aw = ATTN_WIDTH
    assert cache_k.shape[3:] == (N_HEADS, HEAD_DIM)
    assert w_router_group.shape[2] == N_GROUPS
    assert w_router_expert.shape[2] == N_GROUPS * EXPERTS_PER_GROUP
    tiles = _tile_plan(nb * seq, seq, nbs * seqs)
    tri = (jnp.arange(ATTN_BLOCK)[:, None] >= jnp.arange(ATTN_BLOCK)[None, :]).astype(BF16)

    xp = x_prompt.reshape(nb * seq, d)
    xs = x_sample.reshape(nbs * seqs, d)
    outs = [[] for _ in range(6)]
    for i in range(depth):
        wts = _layer_weights(i, ln_mix, w_in, q_norm, k_norm, conv_w, w_attn_branch,
                             w_conv_branch, w_out, ln_ffn, w_router_group, b_router_group,
                             w_router_expert, b_router_expert, w_exp_gate, w_exp_up, w_exp_down,
                             ln_ple, w_ple_gate, w_ple_proj)

        (q, kt, vt, ktb, vtb, sa, mb), conv_new = _input_stage_prompt(
            xp, wts, seq=seq, tile=tiles['input'])
        attn = _attn_prompt(q, ktb, vtb, tri)
        xp = _channel_stage(xp, attn, sa, mb, p_prompt[i].reshape(nb * seq, -1), wts,
                            tile=tiles['channel'], tile_m=tiles['bucket'], by_pair=True)
        outs[0].append(_position_major(kt))
        outs[1].append(_position_major(vt))
        outs[2].append(conv_new)

        (q, k, v, kb, vb, sa, mb), conv_new = _input_stage_sample(xs, state_conv[i], wts, seq=seqs)
        attn = _attn_sample(q.reshape(nbs, seqs, aw), kb.reshape(nbs, seqs, aw),
                            vb.reshape(nbs, seqs, aw), _feature_major(cache_k[i]),
                            _feature_major(cache_v[i]), tri)
        xs = _channel_stage(xs, attn.reshape(nbs * seqs, aw), sa, mb,
                            p_sample[i].reshape(nbs * seqs, -1), wts, tile=nbs * seqs,
                            tile_m=tiles['sample_bucket'], by_pair=False)
        outs[3].append(k.reshape(nbs, seqs, N_HEADS, HEAD_DIM))
        outs[4].append(v.reshape(nbs, seqs, N_HEADS, HEAD_DIM))
        outs[5].append(conv_new)

    kp, vp, cp, ks, vs, cs = [jnp.stack(o) for o in outs]
    return (xp.reshape(nb, seq, d), xs.reshape(nbs, seqs, d), kp, vp, cp, ks, vs, cs)
```

```python
import functools

import jax
import jax.numpy as jnp
from jax import lax
from jax.experimental import pallas as pl
from jax.experimental.pallas import tpu as pltpu

F32 = jnp.float32
BF16 = jnp.bfloat16

EPS = 1e-6
N_HEADS = 8
HEAD_DIM = 64
ATTN_WIDTH = N_HEADS * HEAD_DIM
N_GROUPS = 4
EXPERTS_PER_GROUP = 4
GROUP_SHIFT = 16
DISPATCH_RING = 3
LANES = 128
ATTN_BLOCK = 256
MASKED = -1e30
SP_CLAMP = 64.0
ATTN_PIPE = 16
ATTN_EAGER = 2
WEIGHT_GONE = 150.0
LOG2E_HI = 1.4426950216293335
LOG2E_LO = 1.925963033500011e-08
VMEM_LIMIT = 56 * 1024 * 1024


def _dot(a, b):
    return jnp.dot(a, b, preferred_element_type=F32)


def _dot_nt(a, b):
    return lax.dot_general(a, b, (((1,), (1,)), ((), ())), preferred_element_type=F32)


def _rms_scale(x):
    return lax.rsqrt(jnp.mean(x * x, axis=-1, keepdims=True) + EPS)


def _const_spec(shape):
    nd = len(shape)
    return pl.BlockSpec(shape, lambda *_: (0,) * nd, pipeline_mode=pl.Buffered(1))


def _head_norm(z, gain, hsum_ref):
    ss = _dot((z * z).astype(BF16), hsum_ref[...])
    return z * lax.rsqrt(ss * (1.0 / HEAD_DIM) + EPS) * gain


def _input_stage_body(x, past1, past2, lnmix_ref, win_ref, qn_ref, hsum_ref, convw_ref, wcb_ref,
                      q_ref, sa_ref, mb_ref):
    aw = ATTN_WIDTH
    cw = convw_ref.shape[1]
    d = x.shape[1]
    h = (x * _rms_scale(x) * lnmix_ref[...]).astype(BF16)

    def proj(lo, width):
        return _dot(h, win_ref[:, lo:lo + width])

    q = _head_norm(proj(0, aw), qn_ref[...], hsum_ref)
    q = q * (HEAD_DIM ** -0.5)
    q_ref[...] = (q * LOG2E_HI + q * LOG2E_LO).astype(BF16)

    cb = proj(3 * aw, cw)
    u = proj(3 * aw + cw, cw) * proj(3 * aw + 2 * cw, cw)
    u1, u2 = past1(u), past2(u)
    conv_y = convw_ref[0:1, :] * u2 + convw_ref[1:2, :] * u1 + convw_ref[2:3, :] * u
    yb = _dot((cb * conv_y).astype(BF16), wcb_ref[...])
    ga = proj(3 * aw + 3 * cw, d)
    gb = proj(3 * aw + 3 * cw + d, d)
    sa_ref[...] = jax.nn.sigmoid(ga).astype(BF16)
    mb_ref[...] = (jax.nn.sigmoid(gb) * yb).astype(BF16)
    return h, proj, u


def _input_stage_prompt_kernel(x_ref, lnmix_ref, win_ref, wkvt_ref, qn_ref, knt_ref, hsum_ref,
                               convw_ref, wcb_ref, q_ref, kt_ref, vt_ref, ktb_ref, vtb_ref,
                               sa_ref, mb_ref, cnew_ref, tail_ref, *, tiles_per_seq):
    t = x_ref.shape[0]
    aw = ATTN_WIDTH

    @pl.when(pl.program_id(0) % tiles_per_seq == 0)
    def _():
        tail_ref[...] = jnp.zeros_like(tail_ref)

    def shifted(u, n):
        ext = jnp.concatenate([tail_ref[...], u], axis=0)
        return pltpu.roll(ext, n, axis=0)[8:, :]

    h, _, u = _input_stage_body(x_ref[...], lambda u: shifted(u, 1), lambda u: shifted(u, 2),
                                lnmix_ref, win_ref, qn_ref, hsum_ref, convw_ref, wcb_ref,
                                q_ref, sa_ref, mb_ref)
    tail_ref[...] = u[t - 8:, :]
    cnew_ref[0] = u[t - 8:, :]

    kt = _dot_nt(wkvt_ref[0:aw, :], h)
    k3 = kt.reshape(N_HEADS, HEAD_DIM, t)
    scale = lax.rsqrt(jnp.mean(k3 * k3, axis=1, keepdims=True) + EPS)
    kt = (k3 * scale).reshape(aw, t) * jnp.tile(knt_ref[...], (1, t // LANES))
    kt_ref[0] = kt
    ktb_ref[0] = kt.astype(BF16)
    vt = _dot_nt(wkvt_ref[aw:2 * aw, :], h)
    vt_ref[0] = vt
    vtb_ref[0] = vt.astype(BF16)


def _input_stage_sample_kernel(x_ref, pe0_ref, pe1_ref, lnmix_ref, win_ref, qn_ref, kn_ref,
                               hsum_ref, convw_ref, wcb_ref, q_ref, k_ref, v_ref, kb_ref,
                               vb_ref, sa_ref, mb_ref, u_ref, *, seq):
    t = x_ref.shape[0]
    aw = ATTN_WIDTH
    pos = lax.broadcasted_iota(jnp.int32, (t, 1), 0) % seq

    def past1(u):
        return jnp.where(pos == 0, pe1_ref[...], pltpu.roll(u, 1, axis=0))

    def past2(u):
        return jnp.where(pos == 0, pe0_ref[...],
                         jnp.where(pos == 1, pe1_ref[...], pltpu.roll(u, 2, axis=0)))

    _, proj, u = _input_stage_body(x_ref[...], past1, past2, lnmix_ref, win_ref, qn_ref,
                                   hsum_ref, convw_ref, wcb_ref, q_ref, sa_ref, mb_ref)
    u_ref[...] = u
    k = _head_norm(proj(aw, aw), kn_ref[...], hsum_ref)
    k_ref[...] = k
    kb_ref[...] = k.astype(BF16)
    v = proj(2 * aw, aw)
    v_ref[...] = v
    vb_ref[...] = v.astype(BF16)


def _input_stage_prompt(x, wts, *, seq, tile):
    n, d = x.shape
    nb = n // seq
    aw = ATTN_WIDTH
    cw = wts['conv_w'].shape[1]
    tiles_per_seq = seq // tile
    row = lambda w: pl.BlockSpec((tile, w), lambda i: (i, 0))
    feat = pl.BlockSpec((1, aw, tile), lambda i: (i // tiles_per_seq, 0, i % tiles_per_seq))
    consts = [wts['ln_mix'], wts['w_in'], wts['w_kv_t'], wts['q_norm'], wts['k_norm_t'],
              wts['head_sum'], wts['conv_w'], wts['w_conv_branch']]
    sds = jax.ShapeDtypeStruct
    outs = pl.pallas_call(
        functools.partial(_input_stage_prompt_kernel, tiles_per_seq=tiles_per_seq),
        grid=(n // tile,),
        in_specs=[row(d)] + [_const_spec(c.shape) for c in consts],
        out_specs=[row(aw), feat, feat, feat, feat, row(d), row(d),
                   pl.BlockSpec((1, 8, cw), lambda i: (i // tiles_per_seq, 0, 0))],
        out_shape=[sds((n, aw), BF16), sds((nb, aw, seq), F32), sds((nb, aw, seq), F32),
                   sds((nb, aw, seq), BF16), sds((nb, aw, seq), BF16), sds((n, d), BF16),
                   sds((n, d), BF16), sds((nb, 8, cw), F32)],
        scratch_shapes=[pltpu.VMEM((8, cw), F32)],
        compiler_params=pltpu.CompilerParams(dimension_semantics=("arbitrary",),
                                             vmem_limit_bytes=VMEM_LIMIT),
        name="input_stage_prompt",
    )(x, *consts)
    return outs[:7], outs[7][:, 6:, :]


def _input_stage_sample(x, state, wts, *, seq):
    n, d = x.shape
    aw = ATTN_WIDTH
    cw = wts['conv_w'].shape[1]
    pe0 = jnp.repeat(state[:, 0, :], seq, axis=0)
    pe1 = jnp.repeat(state[:, 1, :], seq, axis=0)
    consts = [wts['ln_mix'], wts['w_in'], wts['q_norm'], wts['k_norm'], wts['head_sum'],
              wts['conv_w'], wts['w_conv_branch']]
    full = lambda a: pl.BlockSpec(a.shape, lambda i: (0,) * a.ndim)
    sds = jax.ShapeDtypeStruct
    out_shape = [sds((n, aw), BF16), sds((n, aw), F32), sds((n, aw), F32), sds((n, aw), BF16),
                 sds((n, aw), BF16), sds((n, d), BF16), sds((n, d), BF16), sds((n, cw), F32)]
    outs = pl.pallas_call(
        functools.partial(_input_stage_sample_kernel, seq=seq),
        grid=(1,),
        in_specs=[full(x), full(pe0), full(pe1)] + [full(c) for c in consts],
        out_specs=[full(o) for o in out_shape],
        out_shape=out_shape,
        compiler_params=pltpu.CompilerParams(dimension_semantics=("arbitrary",),
                                             vmem_limit_bytes=VMEM_LIMIT),
        name="input_stage_sample",
    )(x, pe0, pe1, *consts)
    conv_new = outs[7].reshape(n // seq, seq, cw)[:, seq - 2:, :]
    return outs[:7], conv_new


def _softplus2(z, keep=None):
    sp = jnp.maximum(z, jnp.log2(1.0 + jnp.exp2(jnp.minimum(z, SP_CLAMP))))
    if keep is not None:
        sp = jnp.where(keep, sp, 0.0)
    return sp.astype(BF16), jnp.sum(sp, axis=-1, keepdims=True)


def _stick_weights(z, sp, carry, tri_ref, keep=None):
    a = jnp.exp2(jnp.minimum(z - _dot(sp, tri_ref[...]), 0.0) - carry)
    if keep is not None:
        a = jnp.where(keep, a, 0.0)
    return a.astype(BF16)


def _stick_block(z, carry, tri_ref, keep):
    sp, row_sum = _softplus2(z, keep)
    return _stick_weights(z, sp, carry, tri_ref, keep), carry + row_sum


def _attn_items(nq):
    items = [(qi, j) for qi in range(nq) for j in range(qi, max(qi - ATTN_EAGER, -1), -1)]
    items = [(0, 0)] * (-len(items) % ATTN_PIPE) + items
    return (jnp.array([i[0] for i in items], jnp.int32),
            jnp.array([i[1] for i in items], jnp.int32))


def _attn_prompt_kernel(iq_ref, ik_ref, q_ref, kt_ref, vt_ref, tri_ref, mask_ref, o_ref,
                        z_ref, sp_ref, rs_ref, carry_ref, acc_ref):
    tq = tri_ref.shape[1]
    nq = q_ref.shape[0] // tq
    n_items = iq_ref.shape[0]
    heads = range(LANES // HEAD_DIM)
    lane_head = lax.broadcasted_iota(jnp.int32, (1, LANES), 1) // HEAD_DIM
    row_head = lax.broadcasted_iota(jnp.int32, (LANES, tq), 0) // HEAD_DIM

    def head_q(q, h):
        return q * (lane_head == h).astype(BF16)

    def weighted_v(a_parts, ks):
        vt = vt_ref[0, :, pl.ds(ks, tq)]
        vts = [jnp.where(row_head == h, vt, jnp.zeros_like(vt)) for h in heads]
        return _dot_nt(jnp.concatenate(a_parts, axis=1), jnp.concatenate(vts, axis=1))

    def blocks(item):
        qi, j = iq_ref[item], ik_ref[item]
        return qi, pl.multiple_of(qi * tq, tq), pl.multiple_of(j * tq, tq), qi == j

    def front(item, slot):
        _, qs, ks, own = blocks(item)
        q = q_ref[pl.ds(qs, tq), :]
        kt = kt_ref[0, :, pl.ds(ks, tq)]
        bias = mask_ref[own.astype(jnp.int32)]
        for h in heads:
            z = _dot(head_q(q, h), kt) + bias
            z_ref[slot, h] = z
            sp_ref[slot, h], rs_ref[slot, h] = _softplus2(z)

    def back(item, slot):
        qi, qs, ks, own = blocks(item)
        a_parts = []
        for h in heads:
            carry = jnp.where(own, 0.0, carry_ref[qi, h])
            a_parts.append(_stick_weights(z_ref[slot, h], sp_ref[slot, h], carry, tri_ref))
            carry_ref[qi, h] = carry + rs_ref[slot, h]
        acc_ref[pl.ds(qs, tq), :] = (jnp.where(own, 0.0, acc_ref[pl.ds(qs, tq), :])
                                     + weighted_v(a_parts, ks))

    carry_ref[...] = jnp.zeros_like(carry_ref)
    acc_ref[...] = jnp.zeros_like(acc_ref)
    depth = ATTN_PIPE // 2
    for s in range(depth):
        front(s, s)

    def trip(k, _):
        for s in range(ATTN_PIPE):
            item = ATTN_PIPE * k + s
            if isinstance(k, int) and item + depth >= n_items:
                pass
            else:
                front(item + depth, (s + depth) % ATTN_PIPE)
            back(item, s)
        return 0

    n_trips = n_items // ATTN_PIPE
    lax.fori_loop(0, n_trips - 1, trip, 0)
    trip(n_trips - 1, 0)

    def finish(qi, _):
        qs = pl.multiple_of(qi * tq, tq)

        def more(state):
            j, c0, c1 = state
            return jnp.logical_and(j >= 0, jnp.min(jnp.minimum(c0, c1)) < WEIGHT_GONE)

        def step(state):
            j, *carries = state
            ks = pl.multiple_of(j * tq, tq)
            q = q_ref[pl.ds(qs, tq), :]
            kt = kt_ref[0, :, pl.ds(ks, tq)]
            a_parts = []
            for h in heads:
                a, carries[h] = _stick_block(_dot(head_q(q, h), kt), carries[h], tri_ref, None)
                a_parts.append(a)
            acc_ref[pl.ds(qs, tq), :] += weighted_v(a_parts, ks)
            return (j - 1, *carries)

        lax.while_loop(more, step, (qi - ATTN_EAGER, carry_ref[qi, 0], carry_ref[qi, 1]))
        return 0

    if nq > ATTN_EAGER:
        @pl.when(jnp.min(carry_ref[ATTN_EAGER:]) < WEIGHT_GONE)
        def _():
            lax.fori_loop(ATTN_EAGER, nq, finish, 0)

    o_ref[...] = acc_ref[...].astype(o_ref.dtype)


def _attn_prompt(q, ktb, vtb, tri):
    n, aw = q.shape
    nb, _, seq = ktb.shape
    tq = ATTN_BLOCK
    row = jnp.arange(tq)[:, None]
    col = jnp.arange(tq)[None, :]
    mask = jnp.stack([jnp.zeros((tq, tq), F32), jnp.where(col < row, 0.0, MASKED).astype(F32)])
    iq, ik = _attn_items(seq // tq)
    qspec = pl.BlockSpec((seq, LANES), lambda b, hp, iq, ik: (b, hp))
    kvspec = pl.BlockSpec((1, LANES, seq), lambda b, hp, iq, ik: (b, hp, 0))
    const = lambda a: pl.BlockSpec(a.shape, lambda b, hp, iq, ik: (0,) * a.ndim)
    return pl.pallas_call(
        _attn_prompt_kernel,
        grid_spec=pltpu.PrefetchScalarGridSpec(
            num_scalar_prefetch=2,
            grid=(nb, aw // LANES),
            in_specs=[qspec, kvspec, kvspec, const(tri), const(mask)],
            out_specs=qspec,
            scratch_shapes=[pltpu.VMEM((ATTN_PIPE, 2, tq, tq), F32),
                            pltpu.VMEM((ATTN_PIPE, 2, tq, tq), BF16),
                            pltpu.VMEM((ATTN_PIPE, 2, tq, 1), F32),
                            pltpu.VMEM((seq // tq, 2, tq, 1), F32),
                            pltpu.VMEM((seq, LANES), F32)]),
        out_shape=jax.ShapeDtypeStruct((n, aw), BF16),
        compiler_params=pltpu.CompilerParams(dimension_semantics=("arbitrary", "arbitrary"),
                                             vmem_limit_bytes=VMEM_LIMIT),
        name="attn_prompt",
    )(iq, ik, q, ktb, vtb, tri, mask)


def _attn_sample_kernel(q_ref, kn_ref, vn_ref, tri_ref, ck_ref, cv_ref, o_ref,
                        kbuf_ref, vbuf_ref, acc_ref, sem):
    b = pl.program_id(0)
    tq, aw = q_ref.shape[1], q_ref.shape[2]
    rows = N_HEADS * tq
    tk = ATTN_BLOCK
    last = ck_ref.shape[2] // tk - 1
    slot = b % 2

    def fetch(seq_id, j, s):
        ks = pl.multiple_of(j * tk, tk)
        return (pltpu.make_async_copy(ck_ref.at[seq_id, :, pl.ds(ks, tk)], kbuf_ref.at[s],
                                      sem.at[2 * s]),
                pltpu.make_async_copy(cv_ref.at[seq_id, :, pl.ds(ks, tk)], vbuf_ref.at[s],
                                      sem.at[2 * s + 1]))

    @pl.when(b == 0)
    def _():
        for cp in fetch(0, last, 0):
            cp.start()

    @pl.when(b + 1 < pl.num_programs(0))
    def _():
        for cp in fetch(b + 1, last, 1 - slot):
            cp.start()

    row_head = lax.broadcasted_iota(jnp.int32, (rows, aw), 0) // tq
    lane_head = lax.broadcasted_iota(jnp.int32, (rows, aw), 1) // HEAD_DIM
    qt = jnp.concatenate([q_ref[0]] * N_HEADS, axis=0)
    qrow = jnp.where(row_head == lane_head, qt, jnp.zeros_like(qt))
    pad = jnp.zeros((tk - tq, aw), BF16)
    kn = jnp.concatenate([kn_ref[0], pad], axis=0)
    vn = jnp.concatenate([vn_ref[0], pad], axis=0)
    qpos = lax.broadcasted_iota(jnp.int32, (rows, tk), 0) % tq
    kpos = lax.broadcasted_iota(jnp.int32, (rows, tk), 1)
    a, carry = _stick_block(_dot_nt(qrow, kn), jnp.zeros((rows, 1), F32), tri_ref, kpos < qpos)
    acc_ref[...] = _dot(a, vn)
    for cp in fetch(b, last, slot):
        cp.wait()

    def step(state):
        j, _, carry = state
        a, carry = _stick_block(_dot(qrow, kbuf_ref[slot].astype(BF16)), carry, tri_ref, None)
        acc_ref[...] += _dot_nt(a, vbuf_ref[slot].astype(BF16))
        go_on = jnp.logical_and(j > 0, jnp.min(carry) < WEIGHT_GONE)

        @pl.when(go_on)
        def _():
            for cp in fetch(b, j - 1, slot):
                cp.start()
            for cp in fetch(b, j - 1, slot):
                cp.wait()

        return j - 1, go_on, carry

    lax.while_loop(lambda state: state[1], step, (jnp.int32(last), True, carry))

    lane_head = lax.broadcasted_iota(jnp.int32, (tq, aw), 1) // HEAD_DIM
    out = jnp.zeros((tq, aw), F32)
    for head in range(N_HEADS):
        out = out + jnp.where(lane_head == head, acc_ref[head * tq:(head + 1) * tq, :], 0.0)
    o_ref[0] = out.astype(o_ref.dtype)


def _attn_sample(q, kn, vn, cache_kt, cache_vt, tri):
    nb, tq, aw = q.shape
    new = pl.BlockSpec((1, tq, aw), lambda b: (b, 0, 0))
    hbm = pl.BlockSpec(memory_space=pl.ANY)
    return pl.pallas_call(
        _attn_sample_kernel,
        grid=(nb,),
        in_specs=[new, new, new, pl.BlockSpec(tri.shape, lambda b: (0, 0)), hbm, hbm],
        out_specs=new,
        out_shape=jax.ShapeDtypeStruct((nb, tq, aw), BF16),
        scratch_shapes=[pltpu.VMEM((2, aw, ATTN_BLOCK), F32), pltpu.VMEM((2, aw, ATTN_BLOCK), F32),
                        pltpu.VMEM((N_HEADS * tq, aw), F32), pltpu.SemaphoreType.DMA((4,))],
        compiler_params=pltpu.CompilerParams(dimension_semantics=("arbitrary",),
                                             vmem_limit_bytes=VMEM_LIMIT),
        name="attn_sample",
    )(q, kn, vn, tri, cache_kt, cache_vt)


def _first_max(vals, lane):
    m = jnp.max(vals, axis=-1, keepdims=True)
    idx = jnp.min(jnp.where(vals == m, lane, float(LANES)), axis=-1, keepdims=True)
    return m, idx


def _router_gates(h, wrg_ref, brg_ref, wre_ref, bre_ref):
    lane = lax.broadcasted_iota(jnp.int32, (1, LANES), 1).astype(F32)
    neg = -jnp.inf
    lg = _dot(h, wrg_ref[...]) + brg_ref[...]
    lg = jnp.where(lane < N_GROUPS, lg, neg)
    mg, gidx = _first_max(lg, lane)
    p_sel = 1.0 / jnp.sum(jnp.exp(lg - mg), axis=-1, keepdims=True)
    le = _dot(h, wre_ref[...]) + bre_ref[...]
    in_group = jnp.floor(lane * (1.0 / EXPERTS_PER_GROUP)) == gidx
    le = jnp.where(in_group, le, neg)
    t1, i1 = _first_max(le, lane)
    t2, i2 = _first_max(jnp.where(lane == i1, neg, le), lane)
    e2 = jnp.exp(t2 - t1)
    w1 = 1.0 / (1.0 + e2)
    gate = p_sel * (jnp.where(lane == i1, w1, 0.0) + jnp.where(lane == i2, e2 * w1, 0.0))
    return gate, gidx, i1, i2


def _expert_buckets(by_pair):
    epg = EXPERTS_PER_GROUP
    if not by_pair:
        return [[g * epg + j for j in range(epg)] for g in range(N_GROUPS)]
    return [[g * epg + lo, g * epg + hi] for g in range(N_GROUPS)
            for lo in range(epg) for hi in range(lo + 1, epg)]


def _merge_route_kernel(x_ref, attn_ref, sa_ref, mb_ref, wab_ref, wout_ref, lnffn_ref,
                        wrg_ref, brg_ref, wre_ref, bre_ref, ltri_ref,
                        x1_ref, pay_ref, key_ref, cnt_out_ref, cnt_ref, *, by_pair):
    t, d = x_ref.shape

    @pl.when(pl.program_id(0) == 0)
    def _():
        cnt_ref[...] = jnp.zeros_like(cnt_ref)

    ya = _dot(attn_ref[...], wab_ref[...])
    m = sa_ref[...].astype(F32) * ya + mb_ref[...].astype(F32)
    x1 = x_ref[...] + _dot(m.astype(BF16), wout_ref[...])
    x1_ref[...] = x1
    h = x1 * _rms_scale(x1) * lnffn_ref[...]
    gate, gidx, i1, i2 = _router_gates(h.astype(BF16), wrg_ref, brg_ref, wre_ref, bre_ref)
    pay_ref[:, 0:d] = h
    pay_ref[:, d:d + LANES] = gate
    bucket = gidx
    if by_pair:
        a, b = i1 - gidx * EXPERTS_PER_GROUP, i2 - gidx * EXPERTS_PER_GROUP
        lo, hi = jnp.minimum(a, b), jnp.maximum(a, b)
        pair = jnp.where(lo == 0.0, hi - 1.0, jnp.where(lo == 1.0, hi + 1.0, 5.0))
        bucket = gidx * 6.0 + pair

    lane = lax.broadcasted_iota(jnp.int32, (1, LANES), 1).astype(F32)
    onehot = lane == bucket
    before = _dot(ltri_ref[...], onehot.astype(BF16)) + cnt_ref[...]
    rank = jnp.sum(jnp.where(onehot, before, 0.0), axis=-1, keepdims=True)
    cnt = cnt_ref[...] + jnp.sum(onehot.astype(F32), axis=0, keepdims=True)
    cnt_ref[...] = cnt
    cnt_out_ref[...] = cnt.astype(jnp.int32)
    key = bucket * float(1 << GROUP_SHIFT) + rank
    key_ref[0] = jnp.broadcast_to(key, (t, LANES)).T[0:1, :].astype(jnp.int32)


def _merge_route(x, attn, sa, mb, wts, *, tile, by_pair):
    n, d = x.shape
    n_buckets = len(_expert_buckets(by_pair))
    assert EXPERTS_PER_GROUP == 4 and n_buckets <= LANES
    assert n < (1 << GROUP_SHIFT) and n_buckets << GROUP_SHIFT <= 1 << 24
    row = lambda c: pl.BlockSpec((tile, c), lambda i: (i, 0))
    ltri = (jnp.arange(tile)[:, None] > jnp.arange(tile)[None, :]).astype(BF16)
    consts = [wts['w_attn_branch'], wts['w_out'], wts['ln_ffn'], wts['w_router_group'],
              wts['b_router_group'], wts['w_router_expert'], wts['b_router_expert'], ltri]
    sds = jax.ShapeDtypeStruct
    x1, pay, keys, cnt = pl.pallas_call(
        functools.partial(_merge_route_kernel, by_pair=by_pair),
        grid=(n // tile,),
        in_specs=[row(d), row(attn.shape[1]), row(d), row(d)]
        + [_const_spec(c.shape) for c in consts],
        out_specs=[row(d), row(d + LANES), pl.BlockSpec((1, 1, tile), lambda i: (i, 0, 0)),
                   pl.BlockSpec((1, LANES), lambda i: (0, 0))],
        out_shape=[sds((n, d), F32), sds((n, d + LANES), F32),
                   sds((n // tile, 1, tile), jnp.int32), sds((1, LANES), jnp.int32)],
        scratch_shapes=[pltpu.VMEM((1, LANES), F32)],
        compiler_params=pltpu.CompilerParams(dimension_semantics=("arbitrary",),
                                             vmem_limit_bytes=VMEM_LIMIT),
        name="merge_route",
    )(x, attn, sa, mb, *consts)
    return x1, pay, keys, cnt[0, :n_buckets]


def _row_positions(key_row, off_ref):
    bucket = key_row >> GROUP_SHIFT
    pos = key_row & ((1 << GROUP_SHIFT) - 1)
    for g in range(off_ref.shape[0]):
        pos = pos + jnp.where(bucket == g, off_ref[g], 0)
    return pos


def _to_smem(src_ref, dst_ref, sem):
    cp = pltpu.make_async_copy(src_ref, dst_ref, sem)
    cp.start()
    cp.wait()


def _dispatch_kernel(off_ref, cnt_ref, key_ref, pay_ref, xs_ref, stage_ref, posv_ref, pos_smem,
                     zero_ref, sem_s, sem_in, sem_row, sem_pad, *, tile_m):
    i = pl.program_id(0)
    n = pl.num_programs(0)
    t = key_ref.shape[2]
    ring = stage_ref.shape[0]
    slot = lax.rem(i, ring)
    nxt = lax.rem(i + 1, ring)

    def load(tile, s):
        return pltpu.make_async_copy(pay_ref.at[pl.ds(tile * t, t), :], stage_ref.at[s],
                                     sem_in.at[s])

    def wait_rows(s):
        pltpu.make_async_copy(stage_ref.at[s], xs_ref.at[pl.ds(0, t), :], sem_row.at[s]).wait()

    @pl.when(i == 0)
    def _():
        load(0, 0).start()

    @pl.when(i >= ring - 1)
    def _():
        wait_rows(nxt)

    @pl.when(i + 1 < n)
    def _():
        load(i + 1, nxt).start()

    posv_ref[...] = _row_positions(key_ref[0], off_ref)
    _to_smem(posv_ref, pos_smem, sem_s)
    load(i, slot).wait()

    def copy_row(r, _):
        pltpu.make_async_copy(stage_ref.at[slot, pl.ds(r, 1), :],
                              xs_ref.at[pl.ds(pos_smem[0, r], 1), :], sem_row.at[slot]).start()
        return 0

    lax.fori_loop(0, t, copy_row, 0, unroll=True)

    @pl.when(i == n - 1)
    def _():
        for back in range(ring - 1):
            @pl.when(i >= back)
            def _():
                wait_rows(lax.rem(i - back + ring, ring))
        zero_ref[...] = jnp.zeros_like(zero_ref)
        pads = []

        def pad(cond, start, size):
            pads.append((cond, pltpu.make_async_copy(
                zero_ref.at[pl.ds(0, size), :], xs_ref.at[pl.ds(start, size), :], sem_pad)))

        used = 0
        n_buckets = cnt_ref.shape[0]
        for g in range(n_buckets):
            c = cnt_ref[g]
            c8 = (c + 7) >> 3 << 3
            end = (c + tile_m - 1) // tile_m * tile_m
            for k in range(7):
                pad(c + k < c8, off_ref[g] + c + k, 1)
            start, size = off_ref[g] + c8, 8
            while size < tile_m:
                cond = ((end - c8) & size) != 0
                pad(cond, pl.multiple_of(start, 8), size)
                start = start + jnp.where(cond, size, 0)
                size *= 2
            used = off_ref[g] + end
        for k in range(n_buckets):
            start = used + k * tile_m
            pad(start < xs_ref.shape[0], pl.multiple_of(start, tile_m), tile_m)
        for cond, cp in pads:
            pl.when(cond)(cp.start)
        for cond, cp in pads:
            pl.when(cond)(cp.wait)


def _dispatch(pay, keys, off, counts, *, tile_m):
    n, w = pay.shape
    nt, _, tile = keys.shape
    rows = n + counts.shape[0] * tile_m
    return pl.pallas_call(
        functools.partial(_dispatch_kernel, tile_m=tile_m),
        grid_spec=pltpu.PrefetchScalarGridSpec(
            num_scalar_prefetch=2,
            grid=(nt,),
            in_specs=[pl.BlockSpec((1, 1, tile), lambda i, off, cnt: (i, 0, 0)),
                      pl.BlockSpec(memory_space=pl.ANY)],
            out_specs=pl.BlockSpec(memory_space=pl.ANY),
            scratch_shapes=[pltpu.VMEM((DISPATCH_RING, tile, w), F32),
                            pltpu.VMEM((1, tile), jnp.int32), pltpu.SMEM((1, tile), jnp.int32),
                            pltpu.VMEM((tile_m, w), F32), pltpu.SemaphoreType.DMA,
                            pltpu.SemaphoreType.DMA((DISPATCH_RING,)),
                            pltpu.SemaphoreType.DMA((DISPATCH_RING,)),
                            pltpu.SemaphoreType.DMA]),
        out_shape=jax.ShapeDtypeStruct((rows, w), F32),
        compiler_params=pltpu.CompilerParams(dimension_semantics=("arbitrary",),
                                             vmem_limit_bytes=VMEM_LIMIT),
        name="dispatch",
    )(off, counts, keys, pay)


def _bucket_experts_kernel(exp_ref, xs_ref, *refs):
    *w_refs, y_ref = refs
    k = len(w_refs) // 3
    d = y_ref.shape[1]
    i = pl.program_id(0)
    h = xs_ref[:, 0:d].astype(BF16)
    gate = xs_ref[:, d:d + LANES]
    lane = lax.broadcasted_iota(jnp.int32, (1, LANES), 1)
    y = None
    for j in range(k):
        weg_ref, weu_ref, wed_ref = w_refs[3 * j:3 * j + 3]
        g = jnp.sum(jnp.where(lane == exp_ref[j, i], gate, 0.0), axis=-1, keepdims=True)
        act = jax.nn.silu(_dot(h, weg_ref[0])) * _dot(h, weu_ref[0])
        yj = _dot((g * act).astype(BF16), wed_ref[0])
        y = yj if y is None else y + yj
    y_ref[...] = y


def _bucket_experts(xs, tile_exp, wts, *, tile_m):
    d = xs.shape[1] - LANES
    _, _, d_exp = wts['w_exp_gate'].shape
    k = tile_exp.shape[0]
    w_specs, w_args = [], []
    for j in range(k):
        pick = lambda i, ex, j=j: (ex[j, i], 0, 0)
        w_specs += [pl.BlockSpec((1, d, d_exp), pick), pl.BlockSpec((1, d, d_exp), pick),
                    pl.BlockSpec((1, d_exp, d), pick)]
        w_args += [wts['w_exp_gate'], wts['w_exp_up'], wts['w_exp_down']]
    return pl.pallas_call(
        _bucket_experts_kernel,
        grid_spec=pltpu.PrefetchScalarGridSpec(
            num_scalar_prefetch=1,
            grid=(xs.shape[0] // tile_m,),
            in_specs=[pl.BlockSpec((tile_m, d + LANES), lambda i, ex: (i, 0))] + w_specs,
            out_specs=pl.BlockSpec((tile_m, d), lambda i, ex: (i, 0))),
        out_shape=jax.ShapeDtypeStruct((xs.shape[0], d), F32),
        compiler_params=pltpu.CompilerParams(dimension_semantics=("arbitrary",),
                                             vmem_limit_bytes=VMEM_LIMIT),
        name="bucket_experts",
    )(tile_exp, xs, *w_args)


def _combine_ple_kernel(off_ref, x1_ref, p_ref, keyc_ref, keyn_ref, lnple_ref, wpg_ref, wpp_ref,
                        y_ref, o_ref, ybuf_ref, posv_ref, pos_smem, sem_y, sem_s):
    i = pl.program_id(0)
    n = pl.num_programs(0)
    t = x1_ref.shape[0]
    slot = i % 2

    def fetch(key_ref, s):
        posv_ref[...] = _row_positions(key_ref[0], off_ref)
        _to_smem(posv_ref, pos_smem.at[pl.ds(s, 1), :], sem_s)

        def gather_row(r, _):
            pltpu.make_async_copy(y_ref.at[pl.ds(pos_smem[s, r], 1), :],
                                  ybuf_ref.at[s, pl.ds(r, 1), :], sem_y.at[s]).start()
            return 0

        lax.fori_loop(0, t, gather_row, 0, unroll=True)

    @pl.when(i == 0)
    def _():
        fetch(keyc_ref, 0)

    @pl.when(i + 1 < n)
    def _():
        fetch(keyn_ref, 1 - slot)

    pltpu.make_async_copy(y_ref.at[pl.ds(0, t), :], ybuf_ref.at[slot], sem_y.at[slot]).wait()
    x2 = x1_ref[...] + ybuf_ref[slot]
    hp = (x2 * _rms_scale(x2) * lnple_ref[...]).astype(BF16)
    gate = jax.nn.sigmoid(_dot(hp, wpg_ref[...]))
    o_ref[...] = x2 + gate * _dot(p_ref[...].astype(BF16), wpp_ref[...])


def _combine_ple(x1, p, keys, off, y, wts):
    n, d = x1.shape
    nt, _, tile = keys.shape
    row = lambda c: pl.BlockSpec((tile, c), lambda i, off: (i, 0))
    consts = [wts['ln_ple'], wts['w_ple_gate'], wts['w_ple_proj']]
    return pl.pallas_call(
        _combine_ple_kernel,
        grid_spec=pltpu.PrefetchScalarGridSpec(
            num_scalar_prefetch=1,
            grid=(nt,),
            in_specs=[row(d), row(p.shape[1]),
                      pl.BlockSpec((1, 1, tile), lambda i, off: (i, 0, 0)),
                      pl.BlockSpec((1, 1, tile),
                                   lambda i, off: (jnp.minimum(i + 1, nt - 1), 0, 0))]
            + [_const_spec(c.shape) for c in consts] + [pl.BlockSpec(memory_space=pl.ANY)],
            out_specs=row(d),
            scratch_shapes=[pltpu.VMEM((2, tile, d), F32), pltpu.VMEM((1, tile), jnp.int32),
                            pltpu.SMEM((2, tile), jnp.int32), pltpu.SemaphoreType.DMA((2,)),
                            pltpu.SemaphoreType.DMA]),
        out_shape=jax.ShapeDtypeStruct((n, d), F32),
        compiler_params=pltpu.CompilerParams(dimension_semantics=("arbitrary",),
                                             vmem_limit_bytes=VMEM_LIMIT),
        name="combine_ple",
    )(off, x1, p, keys, keys, *consts, y)


def _channel_stage(x, attn, sa, mb, p, wts, *, tile, tile_m, by_pair):
    n = x.shape[0]
    assert n % tile == 0 and n % tile_m == 0
    x1, pay, keys, counts = _merge_route(x, attn, sa, mb, wts, tile=tile, by_pair=by_pair)
    ends = jnp.cumsum((counts + tile_m - 1) // tile_m * tile_m)
    off = jnp.concatenate([jnp.zeros((1,), jnp.int32), ends[:-1]])
    xs = _dispatch(pay, keys, off, counts, tile_m=tile_m)
    tile_start = jnp.arange(xs.shape[0] // tile_m, dtype=jnp.int32) * tile_m
    tile_bucket = jnp.minimum(jnp.sum(tile_start[:, None] >= ends[None, :], axis=1),
                              counts.shape[0] - 1)
    tile_exp = jnp.array(_expert_buckets(by_pair), jnp.int32)[tile_bucket].T
    y = _bucket_experts(xs, tile_exp, wts, tile_m=tile_m)
    return _combine_ple(x1, p, keys, off, y, wts)


def _layer_weights(i, ln_mix, w_in, q_norm, k_norm, conv_w, w_attn_branch, w_conv_branch, w_out,
                   ln_ffn, w_router_group, b_router_group, w_router_expert, b_router_expert,
                   w_exp_gate, w_exp_up, w_exp_down, ln_ple, w_ple_gate, w_ple_proj):
    aw = ATTN_WIDTH
    pad_cols = lambda a: jnp.pad(a, ((0, 0), (0, LANES - a.shape[1])))
    head_id = jnp.arange(aw) // HEAD_DIM
    k_gain = jnp.tile(k_norm[i], N_HEADS)
    return dict(
        ln_mix=ln_mix[i][None, :], w_in=w_in[i].astype(BF16),
        w_kv_t=w_in[i][:, aw:3 * aw].T.astype(BF16),
        q_norm=jnp.tile(q_norm[i], N_HEADS)[None, :], k_norm=k_gain[None, :],
        k_norm_t=jnp.broadcast_to(k_gain[:, None], (aw, LANES)),
        head_sum=(head_id[:, None] == head_id[None, :]).astype(BF16),
        conv_w=conv_w[i], w_conv_branch=w_conv_branch[i].astype(BF16),
        w_attn_branch=w_attn_branch[i].astype(BF16), w_out=w_out[i].astype(BF16),
        ln_ffn=ln_ffn[i][None, :],
        w_router_group=pad_cols(w_router_group[i]).astype(BF16),
        b_router_group=pad_cols(b_router_group[i][None, :]),
        w_router_expert=pad_cols(w_router_expert[i]).astype(BF16),
        b_router_expert=pad_cols(b_router_expert[i][None, :]),
        w_exp_gate=w_exp_gate[i].astype(BF16), w_exp_up=w_exp_up[i].astype(BF16),
        w_exp_down=w_exp_down[i].astype(BF16),
        ln_ple=ln_ple[i][None, :], w_ple_gate=w_ple_gate[i].astype(BF16),
        w_ple_proj=w_ple_proj[i].astype(BF16))


def _tile_plan(n_prompt, seq, n_sample):
    largest = lambda n, cap: max(t for t in (1024, 512, 256, 128, 64, 32, 16, 8)
                                 if t <= cap and n % t == 0)
    return dict(input=largest(seq, 1024), channel=largest(n_prompt, 512),
                bucket=largest(n_prompt, 256), sample_bucket=largest(n_sample, 256))


def _feature_major(a):
    b, s, h, dh = a.shape
    return jnp.transpose(a, (0, 2, 3, 1)).reshape(b, h * dh, s)


def _position_major(a):
    b, _, s = a.shape
    return jnp.transpose(a.reshape(b, N_HEADS, HEAD_DIM, s), (0, 3, 1, 2))


def kernel(x_prompt, x_sample, cache_k, cache_v, state_conv, p_prompt, p_sample, ln_mix, w_in, q_norm, k_norm, conv_w, w_attn_branch, w_conv_branch, w_out, ln_ffn, w_router_group, b_router_group, w_router_expert, b_router_expert, w_exp_gate, w_exp_up, w_exp_down, ln_ple, w_ple_gate, w_ple_proj):
    depth = ln_mix.shape[0]
    nb, seq, d = x_prompt.shape
    nbs, seqs, _ = x_sample.shape
    aw = ATTN_WIDTH
    assert cache_k.shape[3:] == (N_HEADS, HEAD_DIM)
    assert w_router_group.shape[2] == N_GROUPS
    assert w_router_expert.shape[2] == N_GROUPS * EXPERTS_PER_GROUP
    tiles = _tile_plan(nb * seq, seq, nbs * seqs)
    tri = (jnp.arange(ATTN_BLOCK)[:, None] >= jnp.arange(ATTN_BLOCK)[None, :]).astype(BF16)

    xp = x_prompt.reshape(nb * seq, d)
    xs = x_sample.reshape(nbs * seqs, d)
    outs = [[] for _ in range(6)]
    for i in range(depth):
        wts = _layer_weights(i, ln_mix, w_in, q_norm, k_norm, conv_w, w_attn_branch,
                             w_conv_branch, w_out, ln_ffn, w_router_group, b_router_group,
                             w_router_expert, b_router_expert, w_exp_gate, w_exp_up, w_exp_down,
                             ln_ple, w_ple_gate, w_ple_proj)

        (q, kt, vt, ktb, vtb, sa, mb), conv_new = _input_stage_prompt(
            xp, wts, seq=seq, tile=tiles['input'])
        attn = _attn_prompt(q, ktb, vtb, tri)
        xp = _channel_stage(xp, attn, sa, mb, p_prompt[i].reshape(nb * seq, -1), wts,
                            tile=tiles['channel'], tile_m=tiles['bucket'], by_pair=True)
        outs[0].append(_position_major(kt))
        outs[1].append(_position_major(vt))
        outs[2].append(conv_new)

        (q, k, v, kb, vb, sa, mb), conv_new = _input_stage_sample(xs, state_conv[i], wts, seq=seqs)
        attn = _attn_sample(q.reshape(nbs, seqs, aw), kb.reshape(nbs, seqs, aw),
                            vb.reshape(nbs, seqs, aw), _feature_major(cache_k[i]),
                            _feature_major(cache_v[i]), tri)
        xs = _channel_stage(xs, attn.reshape(nbs * seqs, aw), sa, mb,
                            p_sample[i].reshape(nbs * seqs, -1), wts, tile=nbs * seqs,
                            tile_m=tiles['sample_bucket'], by_pair=False)
        outs[3].append(k.reshape(nbs, seqs, N_HEADS, HEAD_DIM))
        outs[4].append(v.reshape(nbs, seqs, N_HEADS, HEAD_DIM))
        outs[5].append(conv_new)

    kp, vp, cp, ks, vs, cs = [jnp.stack(o) for o in outs]
    return (xp.reshape(nb, seq, d), xs.reshape(nbs, seqs, d), kp, vp, cp, ks, vs, cs)
```

```python
import functools

import jax
import jax.numpy as jnp
from jax import lax
from jax.experimental import pallas as pl
from jax.experimental.pallas import tpu as pltpu

F32 = jnp.float32
BF16 = jnp.bfloat16

EPS = 1e-6
N_HEADS = 8
HEAD_DIM = 64
ATTN_WIDTH = N_HEADS * HEAD_DIM
N_GROUPS = 4
EXPERTS_PER_GROUP = 4
GROUP_SHIFT = 16
DISPATCH_RING = 3
LANES = 128
ATTN_BLOCK = 256
MASKED = -1e30
SP_CLAMP = 64.0
ATTN_PIPE = 16
ATTN_EAGER = 2
WEIGHT_GONE = 150.0
LOG2E_HI = 1.4426950216293335
LOG2E_LO = 1.925963033500011e-08
VMEM_LIMIT = 56 * 1024 * 1024


def _dot(a, b):
    return jnp.dot(a, b, preferred_element_type=F32)


def _dot_nt(a, b):
    return lax.dot_general(a, b, (((1,), (1,)), ((), ())), preferred_element_type=F32)


def _rms_scale(x):
    return lax.rsqrt(jnp.mean(x * x, axis=-1, keepdims=True) + EPS)


def _const_spec(shape):
    nd = len(shape)
    return pl.BlockSpec(shape, lambda *_: (0,) * nd, pipeline_mode=pl.Buffered(1))


def _head_norm(z, gain, hsum_ref):
    ss = _dot((z * z).astype(BF16), hsum_ref[...])
    return z * lax.rsqrt(ss * (1.0 / HEAD_DIM) + EPS) * gain


def _input_stage_body(x, past1, past2, lnmix_ref, win_ref, qn_ref, hsum_ref, convw_ref, wcb_ref,
                      q_ref, sa_ref, mb_ref):
    aw = ATTN_WIDTH
    cw = convw_ref.shape[1]
    d = x.shape[1]
    h = (x * _rms_scale(x) * lnmix_ref[...]).astype(BF16)

    def proj(lo, width):
        return _dot(h, win_ref[:, lo:lo + width])

    q = _head_norm(proj(0, aw), qn_ref[...], hsum_ref)
    q = q * (HEAD_DIM ** -0.5)
    q_ref[...] = (q * LOG2E_HI + q * LOG2E_LO).astype(BF16)

    cb = proj(3 * aw, cw)
    u = proj(3 * aw + cw, cw) * proj(3 * aw + 2 * cw, cw)
    u1, u2 = past1(u), past2(u)
    conv_y = convw_ref[0:1, :] * u2 + convw_ref[1:2, :] * u1 + convw_ref[2:3, :] * u
    yb = _dot((cb * conv_y).astype(BF16), wcb_ref[...])
    ga = proj(3 * aw + 3 * cw, d)
    gb = proj(3 * aw + 3 * cw + d, d)
    sa_ref[...] = jax.nn.sigmoid(ga).astype(BF16)
    mb_ref[...] = (jax.nn.sigmoid(gb) * yb).astype(BF16)
    return h, proj, u


def _input_stage_prompt_kernel(x_ref, lnmix_ref, win_ref, wkvt_ref, qn_ref, knt_ref, hsum_ref,
                               convw_ref, wcb_ref, q_ref, kt_ref, vt_ref, ktb_ref, vtb_ref,
                               sa_ref, mb_ref, cnew_ref, tail_ref, *, tiles_per_seq):
    t = x_ref.shape[0]
    aw = ATTN_WIDTH

    @pl.when(pl.program_id(0) % tiles_per_seq == 0)
    def _():
        tail_ref[...] = jnp.zeros_like(tail_ref)

    def shifted(u, n):
        ext = jnp.concatenate([tail_ref[...], u], axis=0)
        return pltpu.roll(ext, n, axis=0)[8:, :]

    h, _, u = _input_stage_body(x_ref[...], lambda u: shifted(u, 1), lambda u: shifted(u, 2),
                                lnmix_ref, win_ref, qn_ref, hsum_ref, convw_ref, wcb_ref,
                                q_ref, sa_ref, mb_ref)
    tail_ref[...] = u[t - 8:, :]
    cnew_ref[0] = u[t - 8:, :]

    kt = _dot_nt(wkvt_ref[0:aw, :], h)
    k3 = kt.reshape(N_HEADS, HEAD_DIM, t)
    scale = lax.rsqrt(jnp.mean(k3 * k3, axis=1, keepdims=True) + EPS)
    kt = (k3 * scale).reshape(aw, t) * jnp.tile(knt_ref[...], (1, t // LANES))
    kt_ref[0] = kt
    ktb_ref[0] = kt.astype(BF16)
    vt = _dot_nt(wkvt_ref[aw:2 * aw, :], h)
    vt_ref[0] = vt
    vtb_ref[0] = vt.astype(BF16)


def _input_stage_sample_kernel(x_ref, pe0_ref, pe1_ref, lnmix_ref, win_ref, qn_ref, kn_ref,
                               hsum_ref, convw_ref, wcb_ref, q_ref, k_ref, v_ref, kb_ref,
                               vb_ref, sa_ref, mb_ref, u_ref, *, seq):
    t = x_ref.shape[0]
    aw = ATTN_WIDTH
    pos = lax.broadcasted_iota(jnp.int32, (t, 1), 0) % seq

    def past1(u):
        return jnp.where(pos == 0, pe1_ref[...], pltpu.roll(u, 1, axis=0))

    def past2(u):
        return jnp.where(pos == 0, pe0_ref[...],
                         jnp.where(pos == 1, pe1_ref[...], pltpu.roll(u, 2, axis=0)))

    _, proj, u = _input_stage_body(x_ref[...], past1, past2, lnmix_ref, win_ref, qn_ref,
                                   hsum_ref, convw_ref, wcb_ref, q_ref, sa_ref, mb_ref)
    u_ref[...] = u
    k = _head_norm(proj(aw, aw), kn_ref[...], hsum_ref)
    k_ref[...] = k
    kb_ref[...] = k.astype(BF16)
    v = proj(2 * aw, aw)
    v_ref[...] = v
    vb_ref[...] = v.astype(BF16)


def _input_stage_prompt(x, wts, *, seq, tile):
    n, d = x.shape
    nb = n // seq
    aw = ATTN_WIDTH
    cw = wts['conv_w'].shape[1]
    tiles_per_seq = seq // tile
    row = lambda w: pl.BlockSpec((tile, w), lambda i: (i, 0))
    feat = pl.BlockSpec((1, aw, tile), lambda i: (i // tiles_per_seq, 0, i % tiles_per_seq))
    consts = [wts['ln_mix'], wts['w_in'], wts['w_kv_t'], wts['q_norm'], wts['k_norm_t'],
              wts['head_sum'], wts['conv_w'], wts['w_conv_branch']]
    sds = jax.ShapeDtypeStruct
    outs = pl.pallas_call(
        functools.partial(_input_stage_prompt_kernel, tiles_per_seq=tiles_per_seq),
        grid=(n // tile,),
        in_specs=[row(d)] + [_const_spec(c.shape) for c in consts],
        out_specs=[row(aw), feat, feat, feat, feat, row(d), row(d),
                   pl.BlockSpec((1, 8, cw), lambda i: (i // tiles_per_seq, 0, 0))],
        out_shape=[sds((n, aw), BF16), sds((nb, aw, seq), F32), sds((nb, aw, seq), F32),
                   sds((nb, aw, seq), BF16), sds((nb, aw, seq), BF16), sds((n, d), BF16),
                   sds((n, d), BF16), sds((nb, 8, cw), F32)],
        scratch_shapes=[pltpu.VMEM((8, cw), F32)],
        compiler_params=pltpu.CompilerParams(dimension_semantics=("arbitrary",),
                                             vmem_limit_bytes=VMEM_LIMIT),
        name="input_stage_prompt",
    )(x, *consts)
    return outs[:7], outs[7][:, 6:, :]


def _input_stage_sample(x, state, wts, *, seq):
    n, d = x.shape
    aw = ATTN_WIDTH
    cw = wts['conv_w'].shape[1]
    pe0 = jnp.repeat(state[:, 0, :], seq, axis=0)
    pe1 = jnp.repeat(state[:, 1, :], seq, axis=0)
    consts = [wts['ln_mix'], wts['w_in'], wts['q_norm'], wts['k_norm'], wts['head_sum'],
              wts['conv_w'], wts['w_conv_branch']]
    full = lambda a: pl.BlockSpec(a.shape, lambda i: (0,) * a.ndim)
    sds = jax.ShapeDtypeStruct
    out_shape = [sds((n, aw), BF16), sds((n, aw), F32), sds((n, aw), F32), sds((n, aw), BF16),
                 sds((n, aw), BF16), sds((n, d), BF16), sds((n, d), BF16), sds((n, cw), F32)]
    outs = pl.pallas_call(
        functools.partial(_input_stage_sample_kernel, seq=seq),
        grid=(1,),
        in_specs=[full(x), full(pe0), full(pe1)] + [full(c) for c in consts],
        out_specs=[full(o) for o in out_shape],
        out_shape=out_shape,
        compiler_params=pltpu.CompilerParams(dimension_semantics=("arbitrary",),
                                             vmem_limit_bytes=VMEM_LIMIT),
        name="input_stage_sample",
    )(x, pe0, pe1, *consts)
    conv_new = outs[7].reshape(n // seq, seq, cw)[:, seq - 2:, :]
    return outs[:7], conv_new


def _softplus2(z, keep=None):
    sp = jnp.maximum(z, jnp.log2(1.0 + jnp.exp2(jnp.minimum(z, SP_CLAMP))))
    if keep is not None:
        sp = jnp.where(keep, sp, 0.0)
    return sp.astype(BF16), jnp.sum(sp, axis=-1, keepdims=True)


def _stick_weights(z, sp, carry, tri_ref, keep=None):
    a = jnp.exp2(jnp.minimum(z - _dot(sp, tri_ref[...]), 0.0) - carry)
    if keep is not None:
        a = jnp.where(keep, a, 0.0)
    return a.astype(BF16)


def _stick_block(z, carry, tri_ref, keep):
    sp, row_sum = _softplus2(z, keep)
    return _stick_weights(z, sp, carry, tri_ref, keep), carry + row_sum


def _attn_items(nq):
    assert ATTN_EAGER == 2 and ATTN_PIPE % 2 == 0
    items = []
    for qi in range(nq):
        items += [(qi, qi, 1), (qi, qi - 1, 0) if qi else (0, 0, 2)]
    items = [(0, 0, 1), (0, 0, 2)] * ((-len(items) % ATTN_PIPE) // 2) + items
    return tuple(jnp.array([i[c] for i in items], jnp.int32) for c in range(3))


def _attn_prompt_kernel(iq_ref, ik_ref, im_ref, q_ref, kt_ref, vt_ref, tri_ref, mask_ref, o_ref,
                        z_ref, sp_ref, rs_ref, carry_ref, acc_ref):
    tq = tri_ref.shape[1]
    nq = q_ref.shape[0] // tq
    n_items = iq_ref.shape[0]
    heads = range(LANES // HEAD_DIM)
    lane_head = lax.broadcasted_iota(jnp.int32, (1, LANES), 1) // HEAD_DIM
    row_head = lax.broadcasted_iota(jnp.int32, (LANES, tq), 0) // HEAD_DIM

    def head_q(q, h):
        return q * (lane_head == h).astype(BF16)

    def weighted_v(a_parts, ks):
        vt = vt_ref[0, :, pl.ds(ks, tq)]
        vts = [jnp.where(row_head == h, vt, jnp.zeros_like(vt)) for h in heads]
        return _dot_nt(jnp.concatenate(a_parts, axis=1), jnp.concatenate(vts, axis=1))

    def blocks(item):
        qi, j = iq_ref[item], ik_ref[item]
        return qi, pl.multiple_of(qi * tq, tq), pl.multiple_of(j * tq, tq)

    def front(item, slot):
        _, qs, ks = blocks(item)
        q = q_ref[pl.ds(qs, tq), :]
        kt = kt_ref[0, :, pl.ds(ks, tq)]
        bias = mask_ref[im_ref[item]]
        for h in heads:
            z = _dot(head_q(q, h), kt) + bias
            z_ref[slot, h] = z
            sp_ref[slot, h], rs_ref[slot, h] = _softplus2(z)

    def back(item, slot):
        qi, qs, ks = blocks(item)
        own = slot % 2 == 0
        a_parts = []
        for h in heads:
            carry = 0.0 if own else carry_ref[qi, h]
            a_parts.append(_stick_weights(z_ref[slot, h], sp_ref[slot, h], carry, tri_ref))
            carry_ref[qi, h] = carry + rs_ref[slot, h]
        if own:
            acc_ref[pl.ds(qs, tq), :] = weighted_v(a_parts, ks)
        else:
            acc_ref[pl.ds(qs, tq), :] += weighted_v(a_parts, ks)

    depth = ATTN_PIPE // 2
    for s in range(depth):
        front(s, s)

    def trip(k, _):
        for s in range(ATTN_PIPE):
            item = ATTN_PIPE * k + s
            if isinstance(k, int) and item + depth >= n_items:
                pass
            else:
                front(item + depth, (s + depth) % ATTN_PIPE)
            back(item, s)
        return 0

    n_trips = n_items // ATTN_PIPE
    lax.fori_loop(0, n_trips - 1, trip, 0)
    trip(n_trips - 1, 0)

    def finish(qi, _):
        qs = pl.multiple_of(qi * tq, tq)

        def more(state):
            j, c0, c1 = state
            return jnp.logical_and(j >= 0, jnp.min(jnp.minimum(c0, c1)) < WEIGHT_GONE)

        def step(state):
            j, *carries = state
            ks = pl.multiple_of(j * tq, tq)
            q = q_ref[pl.ds(qs, tq), :]
            kt = kt_ref[0, :, pl.ds(ks, tq)]
            a_parts = []
            for h in heads:
                a, carries[h] = _stick_block(_dot(head_q(q, h), kt), carries[h], tri_ref, None)
                a_parts.append(a)
            acc_ref[pl.ds(qs, tq), :] += weighted_v(a_parts, ks)
            return (j - 1, *carries)

        lax.while_loop(more, step, (qi - ATTN_EAGER, carry_ref[qi, 0], carry_ref[qi, 1]))
        return 0

    if nq > ATTN_EAGER:
        @pl.when(jnp.min(carry_ref[ATTN_EAGER:]) < WEIGHT_GONE)
        def _():
            lax.fori_loop(ATTN_EAGER, nq, finish, 0)

    o_ref[...] = acc_ref[...].astype(o_ref.dtype)


def _attn_prompt(q, ktb, vtb, tri):
    n, aw = q.shape
    nb, _, seq = ktb.shape
    tq = ATTN_BLOCK
    row = jnp.arange(tq)[:, None]
    col = jnp.arange(tq)[None, :]
    mask = jnp.stack([jnp.zeros((tq, tq), F32), jnp.where(col < row, 0.0, MASKED).astype(F32),
                      jnp.full((tq, tq), MASKED, F32)])
    iq, ik, im = _attn_items(seq // tq)
    qspec = pl.BlockSpec((seq, LANES), lambda b, hp, *_: (b, hp))
    kvspec = pl.BlockSpec((1, LANES, seq), lambda b, hp, *_: (b, hp, 0))
    const = lambda a: pl.BlockSpec(a.shape, lambda b, hp, *_: (0,) * a.ndim)
    return pl.pallas_call(
        _attn_prompt_kernel,
        grid_spec=pltpu.PrefetchScalarGridSpec(
            num_scalar_prefetch=3,
            grid=(nb, aw // LANES),
            in_specs=[qspec, kvspec, kvspec, const(tri), const(mask)],
            out_specs=qspec,
            scratch_shapes=[pltpu.VMEM((ATTN_PIPE, 2, tq, tq), F32),
                            pltpu.VMEM((ATTN_PIPE, 2, tq, tq), BF16),
                            pltpu.VMEM((ATTN_PIPE, 2, tq, 1), F32),
                            pltpu.VMEM((seq // tq, 2, tq, 1), F32),
                            pltpu.VMEM((seq, LANES), F32)]),
        out_shape=jax.ShapeDtypeStruct((n, aw), BF16),
        compiler_params=pltpu.CompilerParams(dimension_semantics=("arbitrary", "arbitrary"),
                                             vmem_limit_bytes=VMEM_LIMIT),
        name="attn_prompt",
    )(iq, ik, im, q, ktb, vtb, tri, mask)


def _attn_sample_kernel(q_ref, kn_ref, vn_ref, tri_ref, ck_ref, cv_ref, o_ref,
                        kbuf_ref, vbuf_ref, acc_ref, sem):
    b = pl.program_id(0)
    tq, aw = q_ref.shape[1], q_ref.shape[2]
    rows = N_HEADS * tq
    tk = ATTN_BLOCK
    last = ck_ref.shape[2] // tk - 1
    slot = b % 2

    def fetch(seq_id, j, s):
        ks = pl.multiple_of(j * tk, tk)
        return (pltpu.make_async_copy(ck_ref.at[seq_id, :, pl.ds(ks, tk)], kbuf_ref.at[s],
                                      sem.at[2 * s]),
                pltpu.make_async_copy(cv_ref.at[seq_id, :, pl.ds(ks, tk)], vbuf_ref.at[s],
                                      sem.at[2 * s + 1]))

    @pl.when(b == 0)
    def _():
        for cp in fetch(0, last, 0):
            cp.start()

    @pl.when(b + 1 < pl.num_programs(0))
    def _():
        for cp in fetch(b + 1, last, 1 - slot):
            cp.start()

    row_head = lax.broadcasted_iota(jnp.int32, (rows, aw), 0) // tq
    lane_head = lax.broadcasted_iota(jnp.int32, (rows, aw), 1) // HEAD_DIM
    qt = jnp.concatenate([q_ref[0]] * N_HEADS, axis=0)
    qrow = jnp.where(row_head == lane_head, qt, jnp.zeros_like(qt))
    pad = jnp.zeros((tk - tq, aw), BF16)
    kn = jnp.concatenate([kn_ref[0], pad], axis=0)
    vn = jnp.concatenate([vn_ref[0], pad], axis=0)
    qpos = lax.broadcasted_iota(jnp.int32, (rows, tk), 0) % tq
    kpos = lax.broadcasted_iota(jnp.int32, (rows, tk), 1)
    a, carry = _stick_block(_dot_nt(qrow, kn), jnp.zeros((rows, 1), F32), tri_ref, kpos < qpos)
    acc_ref[...] = _dot(a, vn)
    for cp in fetch(b, last, slot):
        cp.wait()

    def step(state):
        j, _, carry = state
        a, carry = _stick_block(_dot(qrow, kbuf_ref[slot].astype(BF16)), carry, tri_ref, None)
        acc_ref[...] += _dot_nt(a, vbuf_ref[slot].astype(BF16))
        go_on = jnp.logical_and(j > 0, jnp.min(carry) < WEIGHT_GONE)

        @pl.when(go_on)
        def _():
            for cp in fetch(b, j - 1, slot):
                cp.start()
            for cp in fetch(b, j - 1, slot):
                cp.wait()

        return j - 1, go_on, carry

    lax.while_loop(lambda state: state[1], step, (jnp.int32(last), True, carry))

    lane_head = lax.broadcasted_iota(jnp.int32, (tq, aw), 1) // HEAD_DIM
    out = jnp.zeros((tq, aw), F32)
    for head in range(N_HEADS):
        out = out + jnp.where(lane_head == head, acc_ref[head * tq:(head + 1) * tq, :], 0.0)
    o_ref[0] = out.astype(o_ref.dtype)


def _attn_sample(q, kn, vn, cache_kt, cache_vt, tri):
    nb, tq, aw = q.shape
    new = pl.BlockSpec((1, tq, aw), lambda b: (b, 0, 0))
    hbm = pl.BlockSpec(memory_space=pl.ANY)
    return pl.pallas_call(
        _attn_sample_kernel,
        grid=(nb,),
        in_specs=[new, new, new, pl.BlockSpec(tri.shape, lambda b: (0, 0)), hbm, hbm],
        out_specs=new,
        out_shape=jax.ShapeDtypeStruct((nb, tq, aw), BF16),
        scratch_shapes=[pltpu.VMEM((2, aw, ATTN_BLOCK), F32), pltpu.VMEM((2, aw, ATTN_BLOCK), F32),
                        pltpu.VMEM((N_HEADS * tq, aw), F32), pltpu.SemaphoreType.DMA((4,))],
        compiler_params=pltpu.CompilerParams(dimension_semantics=("arbitrary",),
                                             vmem_limit_bytes=VMEM_LIMIT),
        name="attn_sample",
    )(q, kn, vn, tri, cache_kt, cache_vt)


def _first_max(vals, lane):
    m = jnp.max(vals, axis=-1, keepdims=True)
    idx = jnp.min(jnp.where(vals == m, lane, float(LANES)), axis=-1, keepdims=True)
    return m, idx


def _router_gates(h, wrg_ref, brg_ref, wre_ref, bre_ref):
    lane = lax.broadcasted_iota(jnp.int32, (1, LANES), 1).astype(F32)
    neg = -jnp.inf
    lg = _dot(h, wrg_ref[...]) + brg_ref[...]
    lg = jnp.where(lane < N_GROUPS, lg, neg)
    mg, gidx = _first_max(lg, lane)
    p_sel = 1.0 / jnp.sum(jnp.exp(lg - mg), axis=-1, keepdims=True)
    le = _dot(h, wre_ref[...]) + bre_ref[...]
    in_group = jnp.floor(lane * (1.0 / EXPERTS_PER_GROUP)) == gidx
    le = jnp.where(in_group, le, neg)
    t1, i1 = _first_max(le, lane)
    t2, i2 = _first_max(jnp.where(lane == i1, neg, le), lane)
    e2 = jnp.exp(t2 - t1)
    w1 = 1.0 / (1.0 + e2)
    gate = p_sel * (jnp.where(lane == i1, w1, 0.0) + jnp.where(lane == i2, e2 * w1, 0.0))
    return gate, gidx, i1, i2


def _expert_buckets(by_pair):
    epg = EXPERTS_PER_GROUP
    if not by_pair:
        return [[g * epg + j for j in range(epg)] for g in range(N_GROUPS)]
    return [[g * epg + lo, g * epg + hi] for g in range(N_GROUPS)
            for lo in range(epg) for hi in range(lo + 1, epg)]


def _merge_route_kernel(x_ref, attn_ref, sa_ref, mb_ref, wab_ref, wout_ref, lnffn_ref,
                        wrg_ref, brg_ref, wre_ref, bre_ref, ltri_ref,
                        x1_ref, pay_ref, key_ref, cnt_out_ref, cnt_ref, *, by_pair):
    t, d = x_ref.shape

    @pl.when(pl.program_id(0) == 0)
    def _():
        cnt_ref[...] = jnp.zeros_like(cnt_ref)

    ya = _dot(attn_ref[...], wab_ref[...])
    m = sa_ref[...].astype(F32) * ya + mb_ref[...].astype(F32)
    x1 = x_ref[...] + _dot(m.astype(BF16), wout_ref[...])
    x1_ref[...] = x1
    h = x1 * _rms_scale(x1) * lnffn_ref[...]
    gate, gidx, i1, i2 = _router_gates(h.astype(BF16), wrg_ref, brg_ref, wre_ref, bre_ref)
    pay_ref[:, 0:d] = h
    pay_ref[:, d:d + LANES] = gate
    bucket = gidx
    if by_pair:
        a, b = i1 - gidx * EXPERTS_PER_GROUP, i2 - gidx * EXPERTS_PER_GROUP
        lo, hi = jnp.minimum(a, b), jnp.maximum(a, b)
        pair = jnp.where(lo == 0.0, hi - 1.0, jnp.where(lo == 1.0, hi + 1.0, 5.0))
        bucket = gidx * 6.0 + pair

    lane = lax.broadcasted_iota(jnp.int32, (1, LANES), 1).astype(F32)
    onehot = lane == bucket
    before = _dot(ltri_ref[...], onehot.astype(BF16)) + cnt_ref[...]
    rank = jnp.sum(jnp.where(onehot, before, 0.0), axis=-1, keepdims=True)
    cnt = cnt_ref[...] + jnp.sum(onehot.astype(F32), axis=0, keepdims=True)
    cnt_ref[...] = cnt
    cnt_out_ref[...] = cnt.astype(jnp.int32)
    key = bucket * float(1 << GROUP_SHIFT) + rank
    key_ref[0] = jnp.broadcast_to(key, (t, LANES)).T[0:1, :].astype(jnp.int32)


def _merge_route(x, attn, sa, mb, wts, *, tile, by_pair):
    n, d = x.shape
    n_buckets = len(_expert_buckets(by_pair))
    assert EXPERTS_PER_GROUP == 4 and n_buckets <= LANES
    assert n < (1 << GROUP_SHIFT) and n_buckets << GROUP_SHIFT <= 1 << 24
    row = lambda c: pl.BlockSpec((tile, c), lambda i: (i, 0))
    ltri = (jnp.arange(tile)[:, None] > jnp.arange(tile)[None, :]).astype(BF16)
    consts = [wts['w_attn_branch'], wts['w_out'], wts['ln_ffn'], wts['w_router_group'],
              wts['b_router_group'], wts['w_router_expert'], wts['b_router_expert'], ltri]
    sds = jax.ShapeDtypeStruct
    x1, pay, keys, cnt = pl.pallas_call(
        functools.partial(_merge_route_kernel, by_pair=by_pair),
        grid=(n // tile,),
        in_specs=[row(d), row(attn.shape[1]), row(d), row(d)]
        + [_const_spec(c.shape) for c in consts],
        out_specs=[row(d), row(d + LANES), pl.BlockSpec((1, 1, tile), lambda i: (i, 0, 0)),
                   pl.BlockSpec((1, LANES), lambda i: (0, 0))],
        out_shape=[sds((n, d), F32), sds((n, d + LANES), F32),
                   sds((n // tile, 1, tile), jnp.int32), sds((1, LANES), jnp.int32)],
        scratch_shapes=[pltpu.VMEM((1, LANES), F32)],
        compiler_params=pltpu.CompilerParams(dimension_semantics=("arbitrary",),
                                             vmem_limit_bytes=VMEM_LIMIT),
        name="merge_route",
    )(x, attn, sa, mb, *consts)
    return x1, pay, keys, cnt[0, :n_buckets]


def _row_positions(key_row, off_ref):
    bucket = key_row >> GROUP_SHIFT
    pos = key_row & ((1 << GROUP_SHIFT) - 1)
    for g in range(off_ref.shape[0]):
        pos = pos + jnp.where(bucket == g, off_ref[g], 0)
    return pos


def _to_smem(src_ref, dst_ref, sem):
    cp = pltpu.make_async_copy(src_ref, dst_ref, sem)
    cp.start()
    cp.wait()


def _dispatch_kernel(off_ref, cnt_ref, key_ref, pay_ref, xs_ref, stage_ref, posv_ref, pos_smem,
                     zero_ref, sem_s, sem_in, sem_row, sem_pad, *, tile_m):
    i = pl.program_id(0)
    n = pl.num_programs(0)
    t = key_ref.shape[2]
    ring = stage_ref.shape[0]
    slot = lax.rem(i, ring)
    nxt = lax.rem(i + 1, ring)

    def load(tile, s):
        return pltpu.make_async_copy(pay_ref.at[pl.ds(tile * t, t), :], stage_ref.at[s],
                                     sem_in.at[s])

    def wait_rows(s):
        pltpu.make_async_copy(stage_ref.at[s], xs_ref.at[pl.ds(0, t), :], sem_row.at[s]).wait()

    @pl.when(i == 0)
    def _():
        load(0, 0).start()

    @pl.when(i >= ring - 1)
    def _():
        wait_rows(nxt)

    @pl.when(i + 1 < n)
    def _():
        load(i + 1, nxt).start()

    posv_ref[...] = _row_positions(key_ref[0], off_ref)
    _to_smem(posv_ref, pos_smem, sem_s)
    load(i, slot).wait()

    def copy_row(r, _):
        pltpu.make_async_copy(stage_ref.at[slot, pl.ds(r, 1), :],
                              xs_ref.at[pl.ds(pos_smem[0, r], 1), :], sem_row.at[slot]).start()
        return 0

    lax.fori_loop(0, t, copy_row, 0, unroll=True)

    @pl.when(i == n - 1)
    def _():
        for back in range(ring - 1):
            @pl.when(i >= back)
            def _():
                wait_rows(lax.rem(i - back + ring, ring))
        zero_ref[...] = jnp.zeros_like(zero_ref)
        pads = []

        def pad(cond, start, size):
            pads.append((cond, pltpu.make_async_copy(
                zero_ref.at[pl.ds(0, size), :], xs_ref.at[pl.ds(start, size), :], sem_pad)))

        used = 0
        n_buckets = cnt_ref.shape[0]
        for g in range(n_buckets):
            c = cnt_ref[g]
            c8 = (c + 7) >> 3 << 3
            end = (c + tile_m - 1) // tile_m * tile_m
            for k in range(7):
                pad(c + k < c8, off_ref[g] + c + k, 1)
            start, size = off_ref[g] + c8, 8
            while size < tile_m:
                cond = ((end - c8) & size) != 0
                pad(cond, pl.multiple_of(start, 8), size)
                start = start + jnp.where(cond, size, 0)
                size *= 2
            used = off_ref[g] + end
        for k in range(n_buckets):
            start = used + k * tile_m
            pad(start < xs_ref.shape[0], pl.multiple_of(start, tile_m), tile_m)
        for cond, cp in pads:
            pl.when(cond)(cp.start)
        for cond, cp in pads:
            pl.when(cond)(cp.wait)


def _dispatch(pay, keys, off, counts, *, tile_m):
    n, w = pay.shape
    nt, _, tile = keys.shape
    rows = n + counts.shape[0] * tile_m
    return pl.pallas_call(
        functools.partial(_dispatch_kernel, tile_m=tile_m),
        grid_spec=pltpu.PrefetchScalarGridSpec(
            num_scalar_prefetch=2,
            grid=(nt,),
            in_specs=[pl.BlockSpec((1, 1, tile), lambda i, off, cnt: (i, 0, 0)),
                      pl.BlockSpec(memory_space=pl.ANY)],
            out_specs=pl.BlockSpec(memory_space=pl.ANY),
            scratch_shapes=[pltpu.VMEM((DISPATCH_RING, tile, w), F32),
                            pltpu.VMEM((1, tile), jnp.int32), pltpu.SMEM((1, tile), jnp.int32),
                            pltpu.VMEM((tile_m, w), F32), pltpu.SemaphoreType.DMA,
                            pltpu.SemaphoreType.DMA((DISPATCH_RING,)),
                            pltpu.SemaphoreType.DMA((DISPATCH_RING,)),
                            pltpu.SemaphoreType.DMA]),
        out_shape=jax.ShapeDtypeStruct((rows, w), F32),
        compiler_params=pltpu.CompilerParams(dimension_semantics=("arbitrary",),
                                             vmem_limit_bytes=VMEM_LIMIT),
        name="dispatch",
    )(off, counts, keys, pay)


def _bucket_experts_kernel(exp_ref, xs_ref, *refs):
    *w_refs, y_ref = refs
    k = len(w_refs) // 3
    d = y_ref.shape[1]
    i = pl.program_id(0)
    h = xs_ref[:, 0:d].astype(BF16)
    gate = xs_ref[:, d:d + LANES]
    lane = lax.broadcasted_iota(jnp.int32, (1, LANES), 1)
    y = None
    for j in range(k):
        weg_ref, weu_ref, wed_ref = w_refs[3 * j:3 * j + 3]
        g = jnp.sum(jnp.where(lane == exp_ref[j, i], gate, 0.0), axis=-1, keepdims=True)
        act = jax.nn.silu(_dot(h, weg_ref[0])) * _dot(h, weu_ref[0])
        yj = _dot((g * act).astype(BF16), wed_ref[0])
        y = yj if y is None else y + yj
    y_ref[...] = y


def _bucket_experts(xs, tile_exp, wts, *, tile_m):
    d = xs.shape[1] - LANES
    _, _, d_exp = wts['w_exp_gate'].shape
    k = tile_exp.shape[0]
    w_specs, w_args = [], []
    for j in range(k):
        pick = lambda i, ex, j=j: (ex[j, i], 0, 0)
        w_specs += [pl.BlockSpec((1, d, d_exp), pick), pl.BlockSpec((1, d, d_exp), pick),
                    pl.BlockSpec((1, d_exp, d), pick)]
        w_args += [wts['w_exp_gate'], wts['w_exp_up'], wts['w_exp_down']]
    return pl.pallas_call(
        _bucket_experts_kernel,
        grid_spec=pltpu.PrefetchScalarGridSpec(
            num_scalar_prefetch=1,
            grid=(xs.shape[0] // tile_m,),
            in_specs=[pl.BlockSpec((tile_m, d + LANES), lambda i, ex: (i, 0))] + w_specs,
            out_specs=pl.BlockSpec((tile_m, d), lambda i, ex: (i, 0))),
        out_shape=jax.ShapeDtypeStruct((xs.shape[0], d), F32),
        compiler_params=pltpu.CompilerParams(dimension_semantics=("arbitrary",),
                                             vmem_limit_bytes=VMEM_LIMIT),
        name="bucket_experts",
    )(tile_exp, xs, *w_args)


def _combine_ple_kernel(off_ref, x1_ref, p_ref, keyc_ref, keyn_ref, lnple_ref, wpg_ref, wpp_ref,
                        y_ref, o_ref, ybuf_ref, posv_ref, pos_smem, sem_y, sem_s):
    i = pl.program_id(0)
    n = pl.num_programs(0)
    t = x1_ref.shape[0]
    slot = i % 2

    def fetch(key_ref, s):
        posv_ref[...] = _row_positions(key_ref[0], off_ref)
        _to_smem(posv_ref, pos_smem.at[pl.ds(s, 1), :], sem_s)

        def gather_row(r, _):
            pltpu.make_async_copy(y_ref.at[pl.ds(pos_smem[s, r], 1), :],
                                  ybuf_ref.at[s, pl.ds(r, 1), :], sem_y.at[s]).start()
            return 0

        lax.fori_loop(0, t, gather_row, 0, unroll=True)

    @pl.when(i == 0)
    def _():
        fetch(keyc_ref, 0)

    @pl.when(i + 1 < n)
    def _():
        fetch(keyn_ref, 1 - slot)

    pltpu.make_async_copy(y_ref.at[pl.ds(0, t), :], ybuf_ref.at[slot], sem_y.at[slot]).wait()
    x2 = x1_ref[...] + ybuf_ref[slot]
    hp = (x2 * _rms_scale(x2) * lnple_ref[...]).astype(BF16)
    gate = jax.nn.sigmoid(_dot(hp, wpg_ref[...]))
    o_ref[...] = x2 + gate * _dot(p_ref[...].astype(BF16), wpp_ref[...])


def _combine_ple(x1, p, keys, off, y, wts):
    n, d = x1.shape
    nt, _, tile = keys.shape
    row = lambda c: pl.BlockSpec((tile, c), lambda i, off: (i, 0))
    consts = [wts['ln_ple'], wts['w_ple_gate'], wts['w_ple_proj']]
    return pl.pallas_call(
        _combine_ple_kernel,
        grid_spec=pltpu.PrefetchScalarGridSpec(
            num_scalar_prefetch=1,
            grid=(nt,),
            in_specs=[row(d), row(p.shape[1]),
                      pl.BlockSpec((1, 1, tile), lambda i, off: (i, 0, 0)),
                      pl.BlockSpec((1, 1, tile),
                                   lambda i, off: (jnp.minimum(i + 1, nt - 1), 0, 0))]
            + [_const_spec(c.shape) for c in consts] + [pl.BlockSpec(memory_space=pl.ANY)],
            out_specs=row(d),
            scratch_shapes=[pltpu.VMEM((2, tile, d), F32), pltpu.VMEM((1, tile), jnp.int32),
                            pltpu.SMEM((2, tile), jnp.int32), pltpu.SemaphoreType.DMA((2,)),
                            pltpu.SemaphoreType.DMA]),
        out_shape=jax.ShapeDtypeStruct((n, d), F32),
        compiler_params=pltpu.CompilerParams(dimension_semantics=("arbitrary",),
                                             vmem_limit_bytes=VMEM_LIMIT),
        name="combine_ple",
    )(off, x1, p, keys, keys, *consts, y)


def _channel_stage(x, attn, sa, mb, p, wts, *, tile, tile_m, by_pair):
    n = x.shape[0]
    assert n % tile == 0 and n % tile_m == 0
    x1, pay, keys, counts = _merge_route(x, attn, sa, mb, wts, tile=tile, by_pair=by_pair)
    ends = jnp.cumsum((counts + tile_m - 1) // tile_m * tile_m)
    off = jnp.concatenate([jnp.zeros((1,), jnp.int32), ends[:-1]])
    xs = _dispatch(pay, keys, off, counts, tile_m=tile_m)
    tile_start = jnp.arange(xs.shape[0] // tile_m, dtype=jnp.int32) * tile_m
    tile_bucket = jnp.minimum(jnp.sum(tile_start[:, None] >= ends[None, :], axis=1),
                              counts.shape[0] - 1)
    tile_exp = jnp.array(_expert_buckets(by_pair), jnp.int32)[tile_bucket].T
    y = _bucket_experts(xs, tile_exp, wts, tile_m=tile_m)
    return _combine_ple(x1, p, keys, off, y, wts)


def _layer_weights(i, ln_mix, w_in, q_norm, k_norm, conv_w, w_attn_branch, w_conv_branch, w_out,
                   ln_ffn, w_router_group, b_router_group, w_router_expert, b_router_expert,
                   w_exp_gate, w_exp_up, w_exp_down, ln_ple, w_ple_gate, w_ple_proj):
    aw = ATTN_WIDTH
    pad_cols = lambda a: jnp.pad(a, ((0, 0), (0, LANES - a.shape[1])))
    head_id = jnp.arange(aw) // HEAD_DIM
    k_gain = jnp.tile(k_norm[i], N_HEADS)
    return dict(
        ln_mix=ln_mix[i][None, :], w_in=w_in[i].astype(BF16),
        w_kv_t=w_in[i][:, aw:3 * aw].T.astype(BF16),
        q_norm=jnp.tile(q_norm[i], N_HEADS)[None, :], k_norm=k_gain[None, :],
        k_norm_t=jnp.broadcast_to(k_gain[:, None], (aw, LANES)),
        head_sum=(head_id[:, None] == head_id[None, :]).astype(BF16),
        conv_w=conv_w[i], w_conv_branch=w_conv_branch[i].astype(BF16),
        w_attn_branch=w_attn_branch[i].astype(BF16), w_out=w_out[i].astype(BF16),
        ln_ffn=ln_ffn[i][None, :],
        w_router_group=pad_cols(w_router_group[i]).astype(BF16),
        b_router_group=pad_cols(b_router_group[i][None, :]),
        w_router_expert=pad_cols(w_router_expert[i]).astype(BF16),
        b_router_expert=pad_cols(b_router_expert[i][None, :]),
        w_exp_gate=w_exp_gate[i].astype(BF16), w_exp_up=w_exp_up[i].astype(BF16),
        w_exp_down=w_exp_down[i].astype(BF16),
        ln_ple=ln_ple[i][None, :], w_ple_gate=w_ple_gate[i].astype(BF16),
        w_ple_proj=w_ple_proj[i].astype(BF16))


def _tile_plan(n_prompt, seq, n_sample):
    largest = lambda n, cap: max(t for t in (1024, 512, 256, 128, 64, 32, 16, 8)
                                 if t <= cap and n % t == 0)
    return dict(input=largest(seq, 1024), channel=largest(n_prompt, 512),
                bucket=largest(n_prompt, 256), sample_bucket=largest(n_sample, 256))


def _feature_major(a):
    b, s, h, dh = a.shape
    return jnp.transpose(a, (0, 2, 3, 1)).reshape(b, h * dh, s)


def _position_major(a):
    b, _, s = a.shape
    return jnp.transpose(a.reshape(b, N_HEADS, HEAD_DIM, s), (0, 3, 1, 2))


def kernel(x_prompt, x_sample, cache_k, cache_v, state_conv, p_prompt, p_sample, ln_mix, w_in, q_norm, k_norm, conv_w, w_attn_branch, w_conv_branch, w_out, ln_ffn, w_router_group, b_router_group, w_router_expert, b_router_expert, w_exp_gate, w_exp_up, w_exp_down, ln_ple, w_ple_gate, w_ple_proj):
    depth = ln_mix.shape[0]
    nb, seq, d = x_prompt.shape
    nbs, seqs, _ = x_sample.shape
    aw = ATTN_WIDTH
    assert cache_k.shape[3:] == (N_HEADS, HEAD_DIM)
    assert w_router_group.shape[2] == N_GROUPS
    assert w_router_expert.shape[2] == N_GROUPS * EXPERTS_PER_GROUP
    tiles = _tile_plan(nb * seq, seq, nbs * seqs)
    tri = (jnp.arange(ATTN_BLOCK)[:, None] >= jnp.arange(ATTN_BLOCK)[None, :]).astype(BF16)

    xp = x_prompt.reshape(nb * seq, d)
    xs = x_sample.reshape(nbs * seqs, d)
    outs = [[] for _ in range(6)]
    for i in range(depth):
        wts = _layer_weights(i, ln_mix, w_in, q_norm, k_norm, conv_w, w_attn_branch,
                             w_conv_branch, w_out, ln_ffn, w_router_group, b_router_group,
                             w_router_expert, b_router_expert, w_exp_gate, w_exp_up, w_exp_down,
                             ln_ple, w_ple_gate, w_ple_proj)

        (q, kt, vt, ktb, vtb, sa, mb), conv_new = _input_stage_prompt(
            xp, wts, seq=seq, tile=tiles['input'])
        attn = _attn_prompt(q, ktb, vtb, tri)
        xp = _channel_stage(xp, attn, sa, mb, p_prompt[i].reshape(nb * seq, -1), wts,
                            tile=tiles['channel'], tile_m=tiles['bucket'], by_pair=True)
        outs[0].append(_position_major(kt))
        outs[1].append(_position_major(vt))
        outs[2].append(conv_new)

        (q, k, v, kb, vb, sa, mb), conv_new = _input_stage_sample(xs, state_conv[i], wts, seq=seqs)
        attn = _attn_sample(q.reshape(nbs, seqs, aw), kb.reshape(nbs, seqs, aw),
                            vb.reshape(nbs, seqs, aw), _feature_major(cache_k[i]),
                            _feature_major(cache_v[i]), tri)
        xs = _channel_stage(xs, attn.reshape(nbs * seqs, aw), sa, mb,
                            p_sample[i].reshape(nbs * seqs, -1), wts, tile=nbs * seqs,
                            tile_m=tiles['sample_bucket'], by_pair=False)
        outs[3].append(k.reshape(nbs, seqs, N_HEADS, HEAD_DIM))
        outs[4].append(v.reshape(nbs, seqs, N_HEADS, HEAD_DIM))
        outs[5].append(conv_new)

    kp, vp, cp, ks, vs, cs = [jnp.stack(o) for o in outs]
    return (xp.reshape(nb, seq, d), xs.reshape(nbs, seqs, d), kp, vp, cp, ks, vs, cs)
```

```python
import functools

import jax
import jax.numpy as jnp
from jax import lax
from jax.experimental import pallas as pl
from jax.experimental.pallas import tpu as pltpu

F32 = jnp.float32
BF16 = jnp.bfloat16

EPS = 1e-6
N_HEADS = 8
HEAD_DIM = 64
ATTN_WIDTH = N_HEADS * HEAD_DIM
N_GROUPS = 4
EXPERTS_PER_GROUP = 4
GROUP_SHIFT = 16
DISPATCH_RING = 3
LANES = 128
ATTN_BLOCK = 256
MASKED = -1e30
SP_CLAMP = 64.0
ATTN_PIPE = 16
ATTN_EAGER = 2
WEIGHT_GONE = 150.0
LOG2E_HI = 1.4426950216293335
LOG2E_LO = 1.925963033500011e-08
VMEM_LIMIT = 56 * 1024 * 1024


def _dot(a, b):
    return jnp.dot(a, b, preferred_element_type=F32)


def _dot_nt(a, b):
    return lax.dot_general(a, b, (((1,), (1,)), ((), ())), preferred_element_type=F32)


def _rms_scale(x):
    return lax.rsqrt(jnp.mean(x * x, axis=-1, keepdims=True) + EPS)


def _const_spec(shape):
    nd = len(shape)
    return pl.BlockSpec(shape, lambda *_: (0,) * nd, pipeline_mode=pl.Buffered(1))


def _head_norm(z, gain, hsum_ref):
    ss = _dot((z * z).astype(BF16), hsum_ref[...])
    return z * lax.rsqrt(ss * (1.0 / HEAD_DIM) + EPS) * gain


def _input_stage_body(x, past1, past2, lnmix_ref, win_ref, qn_ref, hsum_ref, convw_ref, wcb_ref,
                      q_ref, sa_ref, mb_ref):
    aw = ATTN_WIDTH
    cw = convw_ref.shape[1]
    d = x.shape[1]
    h = (x * _rms_scale(x) * lnmix_ref[...]).astype(BF16)

    def proj(lo, width):
        return _dot(h, win_ref[:, lo:lo + width])

    q = _head_norm(proj(0, aw), qn_ref[...], hsum_ref)
    q = q * (HEAD_DIM ** -0.5)
    q_ref[...] = (q * LOG2E_HI + q * LOG2E_LO).astype(BF16)

    cb = proj(3 * aw, cw)
    u = proj(3 * aw + cw, cw) * proj(3 * aw + 2 * cw, cw)
    u1, u2 = past1(u), past2(u)
    conv_y = convw_ref[0:1, :] * u2 + convw_ref[1:2, :] * u1 + convw_ref[2:3, :] * u
    yb = _dot((cb * conv_y).astype(BF16), wcb_ref[...])
    ga = proj(3 * aw + 3 * cw, d)
    gb = proj(3 * aw + 3 * cw + d, d)
    sa_ref[...] = jax.nn.sigmoid(ga).astype(BF16)
    mb_ref[...] = (jax.nn.sigmoid(gb) * yb).astype(BF16)
    return h, proj, u


def _input_stage_prompt_kernel(x_ref, lnmix_ref, win_ref, wkvt_ref, qn_ref, knt_ref, hsum_ref,
                               convw_ref, wcb_ref, q_ref, kt_ref, vt_ref, ktb_ref, vtb_ref,
                               sa_ref, mb_ref, cnew_ref, tail_ref, *, tiles_per_seq):
    t = x_ref.shape[0]
    aw = ATTN_WIDTH

    @pl.when(pl.program_id(0) % tiles_per_seq == 0)
    def _():
        tail_ref[...] = jnp.zeros_like(tail_ref)

    def shifted(u, n):
        ext = jnp.concatenate([tail_ref[...], u], axis=0)
        return pltpu.roll(ext, n, axis=0)[8:, :]

    h, _, u = _input_stage_body(x_ref[...], lambda u: shifted(u, 1), lambda u: shifted(u, 2),
                                lnmix_ref, win_ref, qn_ref, hsum_ref, convw_ref, wcb_ref,
                                q_ref, sa_ref, mb_ref)
    tail_ref[...] = u[t - 8:, :]
    cnew_ref[0] = u[t - 8:, :]

    kt = _dot_nt(wkvt_ref[0:aw, :], h)
    k3 = kt.reshape(N_HEADS, HEAD_DIM, t)
    scale = lax.rsqrt(jnp.mean(k3 * k3, axis=1, keepdims=True) + EPS)
    kt = (k3 * scale).reshape(aw, t) * jnp.tile(knt_ref[...], (1, t // LANES))
    kt_ref[0] = kt
    ktb_ref[0] = kt.astype(BF16)
    vt = _dot_nt(wkvt_ref[aw:2 * aw, :], h)
    vt_ref[0] = vt
    vtb_ref[0] = vt.astype(BF16)


def _input_stage_sample_kernel(x_ref, pe0_ref, pe1_ref, lnmix_ref, win_ref, qn_ref, kn_ref,
                               hsum_ref, convw_ref, wcb_ref, q_ref, k_ref, v_ref, kb_ref,
                               vb_ref, sa_ref, mb_ref, u_ref, *, seq):
    t = x_ref.shape[0]
    aw = ATTN_WIDTH
    pos = lax.broadcasted_iota(jnp.int32, (t, 1), 0) % seq

    def past1(u):
        return jnp.where(pos == 0, pe1_ref[...], pltpu.roll(u, 1, axis=0))

    def past2(u):
        return jnp.where(pos == 0, pe0_ref[...],
                         jnp.where(pos == 1, pe1_ref[...], pltpu.roll(u, 2, axis=0)))

    _, proj, u = _input_stage_body(x_ref[...], past1, past2, lnmix_ref, win_ref, qn_ref,
                                   hsum_ref, convw_ref, wcb_ref, q_ref, sa_ref, mb_ref)
    u_ref[...] = u
    k = _head_norm(proj(aw, aw), kn_ref[...], hsum_ref)
    k_ref[...] = k
    kb_ref[...] = k.astype(BF16)
    v = proj(2 * aw, aw)
    v_ref[...] = v
    vb_ref[...] = v.astype(BF16)


def _input_stage_prompt(x, wts, *, seq, tile):
    n, d = x.shape
    nb = n // seq
    aw = ATTN_WIDTH
    cw = wts['conv_w'].shape[1]
    tiles_per_seq = seq // tile
    row = lambda w: pl.BlockSpec((tile, w), lambda i: (i, 0))
    feat = pl.BlockSpec((1, aw, tile), lambda i: (i // tiles_per_seq, 0, i % tiles_per_seq))
    consts = [wts['ln_mix'], wts['w_in'], wts['w_kv_t'], wts['q_norm'], wts['k_norm_t'],
              wts['head_sum'], wts['conv_w'], wts['w_conv_branch']]
    sds = jax.ShapeDtypeStruct
    outs = pl.pallas_call(
        functools.partial(_input_stage_prompt_kernel, tiles_per_seq=tiles_per_seq),
        grid=(n // tile,),
        in_specs=[row(d)] + [_const_spec(c.shape) for c in consts],
        out_specs=[row(aw), feat, feat, feat, feat, row(d), row(d),
                   pl.BlockSpec((1, 8, cw), lambda i: (i // tiles_per_seq, 0, 0))],
        out_shape=[sds((n, aw), BF16), sds((nb, aw, seq), F32), sds((nb, aw, seq), F32),
                   sds((nb, aw, seq), BF16), sds((nb, aw, seq), BF16), sds((n, d), BF16),
                   sds((n, d), BF16), sds((nb, 8, cw), F32)],
        scratch_shapes=[pltpu.VMEM((8, cw), F32)],
        compiler_params=pltpu.CompilerParams(dimension_semantics=("arbitrary",),
                                             vmem_limit_bytes=VMEM_LIMIT),
        name="input_stage_prompt",
    )(x, *consts)
    return outs[:7], outs[7][:, 6:, :]


def _input_stage_sample(x, state, wts, *, seq):
    n, d = x.shape
    aw = ATTN_WIDTH
    cw = wts['conv_w'].shape[1]
    pe0 = jnp.repeat(state[:, 0, :], seq, axis=0)
    pe1 = jnp.repeat(state[:, 1, :], seq, axis=0)
    consts = [wts['ln_mix'], wts['w_in'], wts['q_norm'], wts['k_norm'], wts['head_sum'],
              wts['conv_w'], wts['w_conv_branch']]
    full = lambda a: pl.BlockSpec(a.shape, lambda i: (0,) * a.ndim)
    sds = jax.ShapeDtypeStruct
    out_shape = [sds((n, aw), BF16), sds((n, aw), F32), sds((n, aw), F32), sds((n, aw), BF16),
                 sds((n, aw), BF16), sds((n, d), BF16), sds((n, d), BF16), sds((n, cw), F32)]
    outs = pl.pallas_call(
        functools.partial(_input_stage_sample_kernel, seq=seq),
        grid=(1,),
        in_specs=[full(x), full(pe0), full(pe1)] + [full(c) for c in consts],
        out_specs=[full(o) for o in out_shape],
        out_shape=out_shape,
        compiler_params=pltpu.CompilerParams(dimension_semantics=("arbitrary",),
                                             vmem_limit_bytes=VMEM_LIMIT),
        name="input_stage_sample",
    )(x, pe0, pe1, *consts)
    conv_new = outs[7].reshape(n // seq, seq, cw)[:, seq - 2:, :]
    return outs[:7], conv_new


def _softplus2(z, keep=None):
    sp = jnp.maximum(z, jnp.log2(1.0 + jnp.exp2(jnp.minimum(z, SP_CLAMP))))
    if keep is not None:
        sp = jnp.where(keep, sp, 0.0)
    return sp.astype(BF16), jnp.sum(sp, axis=-1, keepdims=True)


def _stick_weights(z, sp, carry, tri_ref, keep=None):
    a = jnp.exp2(jnp.minimum(z - _dot(sp, tri_ref[...]), 0.0) - carry)
    if keep is not None:
        a = jnp.where(keep, a, 0.0)
    return a.astype(BF16)


def _stick_block(z, carry, tri_ref, keep):
    sp, row_sum = _softplus2(z, keep)
    return _stick_weights(z, sp, carry, tri_ref, keep), carry + row_sum


def _attn_items(nq):
    assert ATTN_EAGER == 2 and ATTN_PIPE % 2 == 0
    items = []
    for qi in range(nq):
        items += [(qi, qi, 1), (qi, qi - 1, 0) if qi else (0, 0, 2)]
    items = [(0, 0, 1), (0, 0, 2)] * ((-len(items) % ATTN_PIPE) // 2) + items
    return tuple(jnp.array([i[c] for i in items], jnp.int32) for c in range(3))


def _attn_prompt_kernel(iq_ref, ik_ref, im_ref, q_ref, kt_ref, vt_ref, tri_ref, mask_ref, o_ref,
                        z_ref, sp_ref, rs_ref, carry_ref, acc_ref):
    tq = tri_ref.shape[1]
    nq = q_ref.shape[0] // tq
    n_items = iq_ref.shape[0]
    heads = range(LANES // HEAD_DIM)
    lane_head = lax.broadcasted_iota(jnp.int32, (1, LANES), 1) // HEAD_DIM
    row_head = lax.broadcasted_iota(jnp.int32, (LANES, tq), 0) // HEAD_DIM

    def head_q(q, h):
        return q * (lane_head == h).astype(BF16)

    def weighted_v(a_parts, ks):
        vt = vt_ref[0, :, pl.ds(ks, tq)]
        vts = [jnp.where(row_head == h, vt, jnp.zeros_like(vt)) for h in heads]
        return _dot_nt(jnp.concatenate(a_parts, axis=1), jnp.concatenate(vts, axis=1))

    def blocks(item):
        qi, j = iq_ref[item], ik_ref[item]
        return qi, pl.multiple_of(qi * tq, tq), pl.multiple_of(j * tq, tq)

    def front(item, slot):
        _, qs, ks = blocks(item)
        q = q_ref[pl.ds(qs, tq), :]
        kt = kt_ref[0, :, pl.ds(ks, tq)]
        bias = mask_ref[im_ref[item]]
        for h in heads:
            z = _dot(head_q(q, h), kt) + bias
            z_ref[slot, h] = z
            sp_ref[slot, h], rs_ref[slot, h] = _softplus2(z)

    def back(item, slot):
        qi, qs, ks = blocks(item)
        own = slot % 2 == 0
        a_parts = []
        for h in heads:
            carry = 0.0 if own else carry_ref[qi, h]
            a_parts.append(_stick_weights(z_ref[slot, h], sp_ref[slot, h], carry, tri_ref))
            carry_ref[qi, h] = carry + rs_ref[slot, h]
        if own:
            acc_ref[pl.ds(qs, tq), :] = weighted_v(a_parts, ks)
        else:
            acc_ref[pl.ds(qs, tq), :] += weighted_v(a_parts, ks)

    depth = ATTN_PIPE // 2
    for s in range(depth):
        front(s, s)

    def trip(k, _):
        for s in range(ATTN_PIPE):
            item = ATTN_PIPE * k + s
            if isinstance(k, int) and item + depth >= n_items:
                pass
            else:
                front(item + depth, (s + depth) % ATTN_PIPE)
            back(item, s)
        return 0

    n_trips = n_items // ATTN_PIPE
    lax.fori_loop(0, n_trips - 1, trip, 0)
    trip(n_trips - 1, 0)

    def finish(qi, _):
        qs = pl.multiple_of(qi * tq, tq)

        def more(state):
            j, c0, c1 = state
            return jnp.logical_and(j >= 0, jnp.min(jnp.minimum(c0, c1)) < WEIGHT_GONE)

        def step(state):
            j, *carries = state
            ks = pl.multiple_of(j * tq, tq)
            q = q_ref[pl.ds(qs, tq), :]
            kt = kt_ref[0, :, pl.ds(ks, tq)]
            a_parts = []
            for h in heads:
                a, carries[h] = _stick_block(_dot(head_q(q, h), kt), carries[h], tri_ref, None)
                a_parts.append(a)
            acc_ref[pl.ds(qs, tq), :] += weighted_v(a_parts, ks)
            return (j - 1, *carries)

        lax.while_loop(more, step, (qi - ATTN_EAGER, carry_ref[qi, 0], carry_ref[qi, 1]))
        return 0

    if nq > ATTN_EAGER:
        @pl.when(jnp.min(carry_ref[ATTN_EAGER:]) < WEIGHT_GONE)
        def _():
            lax.fori_loop(ATTN_EAGER, nq, finish, 0)

    o_ref[...] = acc_ref[...].astype(o_ref.dtype)


def _attn_prompt(q, ktb, vtb, tri):
    n, aw = q.shape
    nb, _, seq = ktb.shape
    tq = ATTN_BLOCK
    row = jnp.arange(tq)[:, None]
    col = jnp.arange(tq)[None, :]
    mask = jnp.stack([jnp.zeros((tq, tq), F32), jnp.where(col < row, 0.0, MASKED).astype(F32),
                      jnp.full((tq, tq), MASKED, F32)])
    iq, ik, im = _attn_items(seq // tq)
    qspec = pl.BlockSpec((seq, LANES), lambda b, hp, *_: (b, hp))
    kvspec = pl.BlockSpec((1, LANES, seq), lambda b, hp, *_: (b, hp, 0))
    const = lambda a: pl.BlockSpec(a.shape, lambda b, hp, *_: (0,) * a.ndim)
    return pl.pallas_call(
        _attn_prompt_kernel,
        grid_spec=pltpu.PrefetchScalarGridSpec(
            num_scalar_prefetch=3,
            grid=(nb, aw // LANES),
            in_specs=[qspec, kvspec, kvspec, const(tri), const(mask)],
            out_specs=qspec,
            scratch_shapes=[pltpu.VMEM((ATTN_PIPE, 2, tq, tq), F32),
                            pltpu.VMEM((ATTN_PIPE, 2, tq, tq), BF16),
                            pltpu.VMEM((ATTN_PIPE, 2, tq, 1), F32),
                            pltpu.VMEM((seq // tq, 2, tq, 1), F32),
                            pltpu.VMEM((seq, LANES), F32)]),
        out_shape=jax.ShapeDtypeStruct((n, aw), BF16),
        compiler_params=pltpu.CompilerParams(dimension_semantics=("arbitrary", "arbitrary"),
                                             vmem_limit_bytes=VMEM_LIMIT),
        name="attn_prompt",
    )(iq, ik, im, q, ktb, vtb, tri, mask)


def _attn_sample_kernel(q_ref, kn_ref, vn_ref, tri_ref, ck_ref, cv_ref, o_ref,
                        kbuf_ref, vbuf_ref, acc_ref, sem):
    b = pl.program_id(0)
    tq, aw = q_ref.shape[1], q_ref.shape[2]
    rows = N_HEADS * tq
    tk = ATTN_BLOCK
    last = ck_ref.shape[2] // tk - 1
    slot = b % 2

    def fetch(seq_id, j, s):
        ks = pl.multiple_of(j * tk, tk)
        return (pltpu.make_async_copy(ck_ref.at[seq_id, :, pl.ds(ks, tk)], kbuf_ref.at[s],
                                      sem.at[2 * s]),
                pltpu.make_async_copy(cv_ref.at[seq_id, :, pl.ds(ks, tk)], vbuf_ref.at[s],
                                      sem.at[2 * s + 1]))

    @pl.when(b == 0)
    def _():
        for cp in fetch(0, last, 0):
            cp.start()

    @pl.when(b + 1 < pl.num_programs(0))
    def _():
        for cp in fetch(b + 1, last, 1 - slot):
            cp.start()

    row_head = lax.broadcasted_iota(jnp.int32, (rows, aw), 0) // tq
    lane_head = lax.broadcasted_iota(jnp.int32, (rows, aw), 1) // HEAD_DIM
    qt = jnp.concatenate([q_ref[0]] * N_HEADS, axis=0)
    qrow = jnp.where(row_head == lane_head, qt, jnp.zeros_like(qt))
    pad = jnp.zeros((tk - tq, aw), BF16)
    kn = jnp.concatenate([kn_ref[0], pad], axis=0)
    vn = jnp.concatenate([vn_ref[0], pad], axis=0)
    qpos = lax.broadcasted_iota(jnp.int32, (rows, tk), 0) % tq
    kpos = lax.broadcasted_iota(jnp.int32, (rows, tk), 1)
    a, carry = _stick_block(_dot_nt(qrow, kn), jnp.zeros((rows, 1), F32), tri_ref, kpos < qpos)
    acc_ref[...] = _dot(a, vn)
    for cp in fetch(b, last, slot):
        cp.wait()

    def step(state):
        j, _, carry = state
        a, carry = _stick_block(_dot(qrow, kbuf_ref[slot].astype(BF16)), carry, tri_ref, None)
        acc_ref[...] += _dot_nt(a, vbuf_ref[slot].astype(BF16))
        go_on = jnp.logical_and(j > 0, jnp.min(carry) < WEIGHT_GONE)

        @pl.when(go_on)
        def _():
            for cp in fetch(b, j - 1, slot):
                cp.start()
            for cp in fetch(b, j - 1, slot):
                cp.wait()

        return j - 1, go_on, carry

    lax.while_loop(lambda state: state[1], step, (jnp.int32(last), True, carry))

    lane_head = lax.broadcasted_iota(jnp.int32, (tq, aw), 1) // HEAD_DIM
    out = jnp.zeros((tq, aw), F32)
    for head in range(N_HEADS):
        out = out + jnp.where(lane_head == head, acc_ref[head * tq:(head + 1) * tq, :], 0.0)
    o_ref[0] = out.astype(o_ref.dtype)


def _attn_sample(q, kn, vn, cache_kt, cache_vt, tri):
    nb, tq, aw = q.shape
    new = pl.BlockSpec((1, tq, aw), lambda b: (b, 0, 0))
    hbm = pl.BlockSpec(memory_space=pl.ANY)
    return pl.pallas_call(
        _attn_sample_kernel,
        grid=(nb,),
        in_specs=[new, new, new, pl.BlockSpec(tri.shape, lambda b: (0, 0)), hbm, hbm],
        out_specs=new,
        out_shape=jax.ShapeDtypeStruct((nb, tq, aw), BF16),
        scratch_shapes=[pltpu.VMEM((2, aw, ATTN_BLOCK), F32), pltpu.VMEM((2, aw, ATTN_BLOCK), F32),
                        pltpu.VMEM((N_HEADS * tq, aw), F32), pltpu.SemaphoreType.DMA((4,))],
        compiler_params=pltpu.CompilerParams(dimension_semantics=("arbitrary",),
                                             vmem_limit_bytes=VMEM_LIMIT),
        name="attn_sample",
    )(q, kn, vn, tri, cache_kt, cache_vt)


def _first_max(vals, lane):
    m = jnp.max(vals, axis=-1, keepdims=True)
    idx = jnp.min(jnp.where(vals == m, lane, float(LANES)), axis=-1, keepdims=True)
    return m, idx


def _router_gates(h, wrg_ref, brg_ref, wre_ref, bre_ref):
    lane = lax.broadcasted_iota(jnp.int32, (1, LANES), 1).astype(F32)
    neg = -jnp.inf
    lg = _dot(h, wrg_ref[...]) + brg_ref[...]
    lg = jnp.where(lane < N_GROUPS, lg, neg)
    mg, gidx = _first_max(lg, lane)
    p_sel = 1.0 / jnp.sum(jnp.exp(lg - mg), axis=-1, keepdims=True)
    le = _dot(h, wre_ref[...]) + bre_ref[...]
    in_group = jnp.floor(lane * (1.0 / EXPERTS_PER_GROUP)) == gidx
    le = jnp.where(in_group, le, neg)
    t1, i1 = _first_max(le, lane)
    t2, i2 = _first_max(jnp.where(lane == i1, neg, le), lane)
    e2 = jnp.exp(t2 - t1)
    w1 = 1.0 / (1.0 + e2)
    gate = p_sel * (jnp.where(lane == i1, w1, 0.0) + jnp.where(lane == i2, e2 * w1, 0.0))
    return gate, gidx, i1, i2


def _expert_buckets(by_pair):
    epg = EXPERTS_PER_GROUP
    if not by_pair:
        return [[g * epg + j for j in range(epg)] for g in range(N_GROUPS)]
    return [[g * epg + lo, g * epg + hi] for g in range(N_GROUPS)
            for lo in range(epg) for hi in range(lo + 1, epg)]


def _merge_route_kernel(x_ref, attn_ref, sa_ref, mb_ref, wab_ref, wout_ref, lnffn_ref,
                        wrg_ref, brg_ref, wre_ref, bre_ref, ltri_ref,
                        x1_ref, pay_ref, key_ref, cnt_out_ref, cnt_ref, *, by_pair):
    t, d = x_ref.shape

    @pl.when(pl.program_id(0) == 0)
    def _():
        cnt_ref[...] = jnp.zeros_like(cnt_ref)

    ya = _dot(attn_ref[...], wab_ref[...])
    m = sa_ref[...].astype(F32) * ya + mb_ref[...].astype(F32)
    x1 = x_ref[...] + _dot(m.astype(BF16), wout_ref[...])
    x1_ref[...] = x1
    h = x1 * _rms_scale(x1) * lnffn_ref[...]
    gate, gidx, i1, i2 = _router_gates(h.astype(BF16), wrg_ref, brg_ref, wre_ref, bre_ref)
    pay_ref[:, 0:d] = h
    pay_ref[:, d:d + LANES] = gate
    bucket = gidx
    if by_pair:
        a, b = i1 - gidx * EXPERTS_PER_GROUP, i2 - gidx * EXPERTS_PER_GROUP
        lo, hi = jnp.minimum(a, b), jnp.maximum(a, b)
        pair = jnp.where(lo == 0.0, hi - 1.0, jnp.where(lo == 1.0, hi + 1.0, 5.0))
        bucket = gidx * 6.0 + pair

    lane = lax.broadcasted_iota(jnp.int32, (1, LANES), 1).astype(F32)
    onehot = lane == bucket
    before = _dot(ltri_ref[...], onehot.astype(BF16)) + cnt_ref[...]
    rank = jnp.sum(jnp.where(onehot, before, 0.0), axis=-1, keepdims=True)
    cnt = cnt_ref[...] + jnp.sum(onehot.astype(F32), axis=0, keepdims=True)
    cnt_ref[...] = cnt
    cnt_out_ref[...] = cnt.astype(jnp.int32)
    key = bucket * float(1 << GROUP_SHIFT) + rank
    key_ref[0] = jnp.broadcast_to(key, (t, LANES)).T[0:1, :].astype(jnp.int32)


def _merge_route(x, attn, sa, mb, wts, *, tile, by_pair):
    n, d = x.shape
    n_buckets = len(_expert_buckets(by_pair))
    assert EXPERTS_PER_GROUP == 4 and n_buckets <= LANES
    assert n < (1 << GROUP_SHIFT) and n_buckets << GROUP_SHIFT <= 1 << 24
    row = lambda c: pl.BlockSpec((tile, c), lambda i: (i, 0))
    ltri = (jnp.arange(tile)[:, None] > jnp.arange(tile)[None, :]).astype(BF16)
    consts = [wts['w_attn_branch'], wts['w_out'], wts['ln_ffn'], wts['w_router_group'],
              wts['b_router_group'], wts['w_router_expert'], wts['b_router_expert'], ltri]
    sds = jax.ShapeDtypeStruct
    x1, pay, keys, cnt = pl.pallas_call(
        functools.partial(_merge_route_kernel, by_pair=by_pair),
        grid=(n // tile,),
        in_specs=[row(d), row(attn.shape[1]), row(d), row(d)]
        + [_const_spec(c.shape) for c in consts],
        out_specs=[row(d), row(d + LANES), pl.BlockSpec((1, 1, tile), lambda i: (i, 0, 0)),
                   pl.BlockSpec((1, LANES), lambda i: (0, 0))],
        out_shape=[sds((n, d), F32), sds((n, d + LANES), F32),
                   sds((n // tile, 1, tile), jnp.int32), sds((1, LANES), jnp.int32)],
        scratch_shapes=[pltpu.VMEM((1, LANES), F32)],
        compiler_params=pltpu.CompilerParams(dimension_semantics=("arbitrary",),
                                             vmem_limit_bytes=VMEM_LIMIT),
        name="merge_route",
    )(x, attn, sa, mb, *consts)
    return x1, pay, keys, cnt[0, :n_buckets]


def _row_positions(key_row, off_ref):
    bucket = key_row >> GROUP_SHIFT
    pos = key_row & ((1 << GROUP_SHIFT) - 1)
    for g in range(off_ref.shape[0]):
        pos = pos + jnp.where(bucket == g, off_ref[g], 0)
    return pos


def _to_smem(src_ref, dst_ref, sem):
    cp = pltpu.make_async_copy(src_ref, dst_ref, sem)
    cp.start()
    cp.wait()


def _dispatch_kernel(off_ref, cnt_ref, key_ref, pay_ref, xs_ref, stage_ref, posv_ref, pos_smem,
                     zero_ref, sem_s, sem_in, sem_row, sem_pad, *, tile_m):
    i = pl.program_id(0)
    n = pl.num_programs(0)
    t = key_ref.shape[2]
    ring = stage_ref.shape[0]
    slot = lax.rem(i, ring)
    nxt = lax.rem(i + 1, ring)

    def load(tile, s):
        return pltpu.make_async_copy(pay_ref.at[pl.ds(tile * t, t), :], stage_ref.at[s],
                                     sem_in.at[s])

    def wait_rows(s):
        pltpu.make_async_copy(stage_ref.at[s], xs_ref.at[pl.ds(0, t), :], sem_row.at[s]).wait()

    @pl.when(i == 0)
    def _():
        load(0, 0).start()

    @pl.when(i >= ring - 1)
    def _():
        wait_rows(nxt)

    @pl.when(i + 1 < n)
    def _():
        load(i + 1, nxt).start()

    posv_ref[...] = _row_positions(key_ref[0], off_ref)
    _to_smem(posv_ref, pos_smem, sem_s)
    load(i, slot).wait()

    for r in range(t):
        pltpu.make_async_copy(stage_ref.at[slot, pl.ds(r, 1), :],
                              xs_ref.at[pl.ds(pos_smem[0, r], 1), :],
                              sem_row.at[slot]).start(priority=r % 2)

    @pl.when(i == n - 1)
    def _():
        for back in range(ring - 1):
            @pl.when(i >= back)
            def _():
                wait_rows(lax.rem(i - back + ring, ring))
        zero_ref[...] = jnp.zeros_like(zero_ref)
        pads = []

        def pad(cond, start, size):
            pads.append((cond, pltpu.make_async_copy(
                zero_ref.at[pl.ds(0, size), :], xs_ref.at[pl.ds(start, size), :], sem_pad)))

        used = 0
        n_buckets = cnt_ref.shape[0]
        for g in range(n_buckets):
            c = cnt_ref[g]
            c8 = (c + 7) >> 3 << 3
            end = (c + tile_m - 1) // tile_m * tile_m
            for k in range(7):
                pad(c + k < c8, off_ref[g] + c + k, 1)
            start, size = off_ref[g] + c8, 8
            while size < tile_m:
                cond = ((end - c8) & size) != 0
                pad(cond, pl.multiple_of(start, 8), size)
                start = start + jnp.where(cond, size, 0)
                size *= 2
            used = off_ref[g] + end
        for k in range(n_buckets):
            start = used + k * tile_m
            pad(start < xs_ref.shape[0], pl.multiple_of(start, tile_m), tile_m)
        for cond, cp in pads:
            pl.when(cond)(cp.start)
        for cond, cp in pads:
            pl.when(cond)(cp.wait)


def _dispatch(pay, keys, off, counts, *, tile_m):
    n, w = pay.shape
    nt, _, tile = keys.shape
    rows = n + counts.shape[0] * tile_m
    return pl.pallas_call(
        functools.partial(_dispatch_kernel, tile_m=tile_m),
        grid_spec=pltpu.PrefetchScalarGridSpec(
            num_scalar_prefetch=2,
            grid=(nt,),
            in_specs=[pl.BlockSpec((1, 1, tile), lambda i, off, cnt: (i, 0, 0)),
                      pl.BlockSpec(memory_space=pl.ANY)],
            out_specs=pl.BlockSpec(memory_space=pl.ANY),
            scratch_shapes=[pltpu.VMEM((DISPATCH_RING, tile, w), F32),
                            pltpu.VMEM((1, tile), jnp.int32), pltpu.SMEM((1, tile), jnp.int32),
                            pltpu.VMEM((tile_m, w), F32), pltpu.SemaphoreType.DMA,
                            pltpu.SemaphoreType.DMA((DISPATCH_RING,)),
                            pltpu.SemaphoreType.DMA((DISPATCH_RING,)),
                            pltpu.SemaphoreType.DMA]),
        out_shape=jax.ShapeDtypeStruct((rows, w), F32),
        compiler_params=pltpu.CompilerParams(dimension_semantics=("arbitrary",),
                                             vmem_limit_bytes=VMEM_LIMIT),
        name="dispatch",
    )(off, counts, keys, pay)


def _bucket_experts_kernel(exp_ref, xs_ref, *refs):
    *w_refs, y_ref = refs
    k = len(w_refs) // 3
    d = y_ref.shape[1]
    i = pl.program_id(0)
    h = xs_ref[:, 0:d].astype(BF16)
    gate = xs_ref[:, d:d + LANES]
    lane = lax.broadcasted_iota(jnp.int32, (1, LANES), 1)
    y = None
    for j in range(k):
        weg_ref, weu_ref, wed_ref = w_refs[3 * j:3 * j + 3]
        g = jnp.sum(jnp.where(lane == exp_ref[j, i], gate, 0.0), axis=-1, keepdims=True)
        act = jax.nn.silu(_dot(h, weg_ref[0])) * _dot(h, weu_ref[0])
        yj = _dot((g * act).astype(BF16), wed_ref[0])
        y = yj if y is None else y + yj
    y_ref[...] = y


def _bucket_experts(xs, tile_exp, wts, *, tile_m):
    d = xs.shape[1] - LANES
    _, _, d_exp = wts['w_exp_gate'].shape
    k = tile_exp.shape[0]
    w_specs, w_args = [], []
    for j in range(k):
        pick = lambda i, ex, j=j: (ex[j, i], 0, 0)
        w_specs += [pl.BlockSpec((1, d, d_exp), pick), pl.BlockSpec((1, d, d_exp), pick),
                    pl.BlockSpec((1, d_exp, d), pick)]
        w_args += [wts['w_exp_gate'], wts['w_exp_up'], wts['w_exp_down']]
    return pl.pallas_call(
        _bucket_experts_kernel,
        grid_spec=pltpu.PrefetchScalarGridSpec(
            num_scalar_prefetch=1,
            grid=(xs.shape[0] // tile_m,),
            in_specs=[pl.BlockSpec((tile_m, d + LANES), lambda i, ex: (i, 0))] + w_specs,
            out_specs=pl.BlockSpec((tile_m, d), lambda i, ex: (i, 0))),
        out_shape=jax.ShapeDtypeStruct((xs.shape[0], d), F32),
        compiler_params=pltpu.CompilerParams(dimension_semantics=("arbitrary",),
                                             vmem_limit_bytes=VMEM_LIMIT),
        name="bucket_experts",
    )(tile_exp, xs, *w_args)


def _combine_ple_kernel(off_ref, x1_ref, p_ref, keyc_ref, keyn_ref, lnple_ref, wpg_ref, wpp_ref,
                        y_ref, o_ref, ybuf_ref, posv_ref, pos_smem, sem_y, sem_s):
    i = pl.program_id(0)
    n = pl.num_programs(0)
    t = x1_ref.shape[0]
    slot = i % 2

    def fetch(key_ref, s):
        posv_ref[...] = _row_positions(key_ref[0], off_ref)
        _to_smem(posv_ref, pos_smem.at[pl.ds(s, 1), :], sem_s)

        for r in range(t):
            pltpu.make_async_copy(y_ref.at[pl.ds(pos_smem[s, r], 1), :],
                                  ybuf_ref.at[s, pl.ds(r, 1), :],
                                  sem_y.at[s]).start(priority=r % 2)

    @pl.when(i == 0)
    def _():
        fetch(keyc_ref, 0)

    @pl.when(i + 1 < n)
    def _():
        fetch(keyn_ref, 1 - slot)

    pltpu.make_async_copy(y_ref.at[pl.ds(0, t), :], ybuf_ref.at[slot], sem_y.at[slot]).wait()
    x2 = x1_ref[...] + ybuf_ref[slot]
    hp = (x2 * _rms_scale(x2) * lnple_ref[...]).astype(BF16)
    gate = jax.nn.sigmoid(_dot(hp, wpg_ref[...]))
    o_ref[...] = x2 + gate * _dot(p_ref[...].astype(BF16), wpp_ref[...])


def _combine_ple(x1, p, keys, off, y, wts):
    n, d = x1.shape
    nt, _, tile = keys.shape
    row = lambda c: pl.BlockSpec((tile, c), lambda i, off: (i, 0))
    consts = [wts['ln_ple'], wts['w_ple_gate'], wts['w_ple_proj']]
    return pl.pallas_call(
        _combine_ple_kernel,
        grid_spec=pltpu.PrefetchScalarGridSpec(
            num_scalar_prefetch=1,
            grid=(nt,),
            in_specs=[row(d), row(p.shape[1]),
                      pl.BlockSpec((1, 1, tile), lambda i, off: (i, 0, 0)),
                      pl.BlockSpec((1, 1, tile),
                                   lambda i, off: (jnp.minimum(i + 1, nt - 1), 0, 0))]
            + [_const_spec(c.shape) for c in consts] + [pl.BlockSpec(memory_space=pl.ANY)],
            out_specs=row(d),
            scratch_shapes=[pltpu.VMEM((2, tile, d), F32), pltpu.VMEM((1, tile), jnp.int32),
                            pltpu.SMEM((2, tile), jnp.int32), pltpu.SemaphoreType.DMA((2,)),
                            pltpu.SemaphoreType.DMA]),
        out_shape=jax.ShapeDtypeStruct((n, d), F32),
        compiler_params=pltpu.CompilerParams(dimension_semantics=("arbitrary",),
                                             vmem_limit_bytes=VMEM_LIMIT),
        name="combine_ple",
    )(off, x1, p, keys, keys, *consts, y)


def _channel_stage(x, attn, sa, mb, p, wts, *, tile, tile_m, by_pair):
    n = x.shape[0]
    assert n % tile == 0 and n % tile_m == 0
    x1, pay, keys, counts = _merge_route(x, attn, sa, mb, wts, tile=tile, by_pair=by_pair)
    ends = jnp.cumsum((counts + tile_m - 1) // tile_m * tile_m)
    off = jnp.concatenate([jnp.zeros((1,), jnp.int32), ends[:-1]])
    xs = _dispatch(pay, keys, off, counts, tile_m=tile_m)
    tile_start = jnp.arange(xs.shape[0] // tile_m, dtype=jnp.int32) * tile_m
    tile_bucket = jnp.minimum(jnp.sum(tile_start[:, None] >= ends[None, :], axis=1),
                              counts.shape[0] - 1)
    tile_exp = jnp.array(_expert_buckets(by_pair), jnp.int32)[tile_bucket].T
    y = _bucket_experts(xs, tile_exp, wts, tile_m=tile_m)
    return _combine_ple(x1, p, keys, off, y, wts)


def _layer_weights(i, ln_mix, w_in, q_norm, k_norm, conv_w, w_attn_branch, w_conv_branch, w_out,
                   ln_ffn, w_router_group, b_router_group, w_router_expert, b_router_expert,
                   w_exp_gate, w_exp_up, w_exp_down, ln_ple, w_ple_gate, w_ple_proj):
    aw = ATTN_WIDTH
    pad_cols = lambda a: jnp.pad(a, ((0, 0), (0, LANES - a.shape[1])))
    head_id = jnp.arange(aw) // HEAD_DIM
    k_gain = jnp.tile(k_norm[i], N_HEADS)
    return dict(
        ln_mix=ln_mix[i][None, :], w_in=w_in[i].astype(BF16),
        w_kv_t=w_in[i][:, aw:3 * aw].T.astype(BF16),
        q_norm=jnp.tile(q_norm[i], N_HEADS)[None, :], k_norm=k_gain[None, :],
        k_norm_t=jnp.broadcast_to(k_gain[:, None], (aw, LANES)),
        head_sum=(head_id[:, None] == head_id[None, :]).astype(BF16),
        conv_w=conv_w[i], w_conv_branch=w_conv_branch[i].astype(BF16),
        w_attn_branch=w_attn_branch[i].astype(BF16), w_out=w_out[i].astype(BF16),
        ln_ffn=ln_ffn[i][None, :],
        w_router_group=pad_cols(w_router_group[i]).astype(BF16),
        b_router_group=pad_cols(b_router_group[i][None, :]),
        w_router_expert=pad_cols(w_router_expert[i]).astype(BF16),
        b_router_expert=pad_cols(b_router_expert[i][None, :]),
        w_exp_gate=w_exp_gate[i].astype(BF16), w_exp_up=w_exp_up[i].astype(BF16),
        w_exp_down=w_exp_down[i].astype(BF16),
        ln_ple=ln_ple[i][None, :], w_ple_gate=w_ple_gate[i].astype(BF16),
        w_ple_proj=w_ple_proj[i].astype(BF16))


def _tile_plan(n_prompt, seq, n_sample):
    largest = lambda n, cap: max(t for t in (1024, 512, 256, 128, 64, 32, 16, 8)
                                 if t <= cap and n % t == 0)
    return dict(input=largest(seq, 1024), channel=largest(n_prompt, 512),
                bucket=largest(n_prompt, 256), sample_bucket=largest(n_sample, 256))


def _feature_major(a):
    b, s, h, dh = a.shape
    return jnp.transpose(a, (0, 2, 3, 1)).reshape(b, h * dh, s)


def _position_major(a):
    b, _, s = a.shape
    return jnp.transpose(a.reshape(b, N_HEADS, HEAD_DIM, s), (0, 3, 1, 2))


def kernel(x_prompt, x_sample, cache_k, cache_v, state_conv, p_prompt, p_sample, ln_mix, w_in, q_norm, k_norm, conv_w, w_attn_branch, w_conv_branch, w_out, ln_ffn, w_router_group, b_router_group, w_router_expert, b_router_expert, w_exp_gate, w_exp_up, w_exp_down, ln_ple, w_ple_gate, w_ple_proj):
    depth = ln_mix.shape[0]
    nb, seq, d = x_prompt.shape
    nbs, seqs, _ = x_sample.shape
    aw = ATTN_WIDTH
    assert cache_k.shape[3:] == (N_HEADS, HEAD_DIM)
    assert w_router_group.shape[2] == N_GROUPS
    assert w_router_expert.shape[2] == N_GROUPS * EXPERTS_PER_GROUP
    tiles = _tile_plan(nb * seq, seq, nbs * seqs)
    tri = (jnp.arange(ATTN_BLOCK)[:, None] >= jnp.arange(ATTN_BLOCK)[None, :]).astype(BF16)

    xp = x_prompt.reshape(nb * seq, d)
    xs = x_sample.reshape(nbs * seqs, d)
    outs = [[] for _ in range(6)]
    for i in range(depth):
        wts = _layer_weights(i, ln_mix, w_in, q_norm, k_norm, conv_w, w_attn_branch,
                             w_conv_branch, w_out, ln_ffn, w_router_group, b_router_group,
                             w_router_expert, b_router_expert, w_exp_gate, w_exp_up, w_exp_down,
                             ln_ple, w_ple_gate, w_ple_proj)

        (q, kt, vt, ktb, vtb, sa, mb), conv_new = _input_stage_prompt(
            xp, wts, seq=seq, tile=tiles['input'])
        attn = _attn_prompt(q, ktb, vtb, tri)
        xp = _channel_stage(xp, attn, sa, mb, p_prompt[i].reshape(nb * seq, -1), wts,
                            tile=tiles['channel'], tile_m=tiles['bucket'], by_pair=True)
        outs[0].append(_position_major(kt))
        outs[1].append(_position_major(vt))
        outs[2].append(conv_new)

        (q, k, v, kb, vb, sa, mb), conv_new = _input_stage_sample(xs, state_conv[i], wts, seq=seqs)
        attn = _attn_sample(q.reshape(nbs, seqs, aw), kb.reshape(nbs, seqs, aw),
                            vb.reshape(nbs, seqs, aw), _feature_major(cache_k[i]),
                            _feature_major(cache_v[i]), tri)
        xs = _channel_stage(xs, attn.reshape(nbs * seqs, aw), sa, mb,
                            p_sample[i].reshape(nbs * seqs, -1), wts, tile=nbs * seqs,
                            tile_m=tiles['sample_bucket'], by_pair=False)
        outs[3].append(k.reshape(nbs, seqs, N_HEADS, HEAD_DIM))
        outs[4].append(v.reshape(nbs, seqs, N_HEADS, HEAD_DIM))
        outs[5].append(conv_new)

    kp, vp, cp, ks, vs, cs = [jnp.stack(o) for o in outs]
    return (xp.reshape(nb, seq, d), xs.reshape(nbs, seqs, d), kp, vp, cp, ks, vs, cs)
```
